```python
import jax, jax.numpy as jnp
from jax import lax
import numpy as np

D_MODEL = 1024
BATCH = 8
SEQ = 8192
DEPTH = 1

HEAD_DIM = 64
NSA_HEADS = 8
NSA_KV_HEADS = 2
NSA_GROUP = NSA_HEADS // NSA_KV_HEADS
NSA_WIDTH = NSA_HEADS * HEAD_DIM
KV_WIDTH = NSA_KV_HEADS * HEAD_DIM
N_BRANCH = 3
CMP_LEN = 32
CMP_STRIDE = 16
SEL_BLOCK = 64
SEL_TOPK = 16
WINDOW = 512
FORCE_SCORE = 1e4
POOL_WINDOWS = (2, 4, 8, 16)
POOL_GROUPS = 4
POOL_GROUP_DIM = 64
POOL_WIDTH = POOL_GROUPS * POOL_GROUP_DIM
MEM_HEADS = 4
MEM_LEN = 256
MEM_WIDTH = MEM_HEADS * HEAD_DIM
MIX_WIDTH = NSA_WIDTH + POOL_WIDTH + MEM_WIDTH
ROPE_THETA = 500000.0
ROPE_DIM = HEAD_DIM // 4
EPS = 1e-6
Q_BLOCK = 32
IN_SPLITS = (NSA_WIDTH, 6 * KV_WIDTH, NSA_HEADS * N_BRANCH, NSA_WIDTH,
             POOL_WIDTH, POOL_WIDTH, MEM_WIDTH, MEM_WIDTH)
IN_WIDTH = sum(IN_SPLITS)

kernel_name = "hymba_nsa_pool_memory_layer"


def rms_norm(x, g):
    x32 = x.astype(jnp.float32)
    y = x32 * lax.rsqrt(jnp.mean(x32 * x32, axis=-1, keepdims=True) + EPS)
    return (y * g.astype(jnp.float32)).astype(x.dtype)


def partial_rope(x, pos):
    half = ROPE_DIM // 2
    inv_freq = ROPE_THETA ** (-jnp.arange(half, dtype=jnp.float32) / half)
    ang = pos.astype(jnp.float32)[:, None, :, None] * inv_freq
    cos, sin = jnp.cos(ang), jnp.sin(ang)
    x32 = x.astype(jnp.float32)
    x1, x2 = x32[..., :half], x32[..., half:ROPE_DIM]
    out = jnp.concatenate([x1 * cos - x2 * sin, x1 * sin + x2 * cos, x32[..., ROPE_DIM:]], axis=-1)
    return out.astype(x.dtype)


def masked_softmax(s, mask):
    s = jnp.where(mask, s, -jnp.inf)
    m = jnp.max(s, axis=-1, keepdims=True)
    m = jnp.where(jnp.isfinite(m), m, 0.0)
    e = jnp.where(mask, jnp.exp(s - m), 0.0)
    return e / jnp.maximum(jnp.sum(e, axis=-1, keepdims=True), jnp.finfo(jnp.float32).tiny)


def compress_blocks(kv_raw, pos_emb, w1, w2):
    S = kv_raw.shape[1]
    nc = (S - CMP_LEN) // CMP_STRIDE + 1
    idx = np.arange(nc)[:, None] * CMP_STRIDE + np.arange(CMP_LEN)[None, :]
    blocks = kv_raw[:, idx] + pos_emb[None, None, :, None, :]
    hid = jax.nn.gelu(jnp.einsum('bnlgd,lde->bgne', blocks, w1))
    return jnp.einsum('bgne,ef->bgnf', hid, w2)


def cmp_to_sel_map(nc, ns):
    c0 = np.arange(nc) * CMP_STRIDE
    c1 = c0 + CMP_LEN
    s0 = np.arange(ns) * SEL_BLOCK
    s1 = s0 + SEL_BLOCK
    ov = np.clip(np.minimum(c1[:, None], s1[None, :]) - np.maximum(c0[:, None], s0[None, :]), 0, None)
    return (ov / CMP_LEN).astype(np.float32)


def nsa_mixer(q_n, kv_n, gate_n, positions, g_q_nsa, g_k_cmp, g_k_slc, g_k_win,
              cmp_pos_k, w_cmp_k1, w_cmp_k2, cmp_pos_v, w_cmp_v1, w_cmp_v2):
    B, S, _ = q_n.shape
    G, R, hd = NSA_KV_HEADS, NSA_GROUP, HEAD_DIM
    dt = q_n.dtype
    scale = HEAD_DIM ** -0.5
    nc = (S - CMP_LEN) // CMP_STRIDE + 1
    ns = S // SEL_BLOCK
    top_n = min(SEL_TOPK, ns)

    q = q_n.reshape(B, S, NSA_HEADS, hd).transpose(0, 2, 1, 3)
    q = partial_rope(rms_norm(q, g_q_nsa), positions)
    kv = kv_n.reshape(B, S, 6, G, hd)
    kc_raw, vc_raw = kv[:, :, 0], kv[:, :, 1]
    k_s, v_s = kv[:, :, 2].transpose(0, 2, 1, 3), kv[:, :, 3].transpose(0, 2, 1, 3)
    k_w, v_w = kv[:, :, 4].transpose(0, 2, 1, 3), kv[:, :, 5].transpose(0, 2, 1, 3)

    end_idx = np.arange(nc) * CMP_STRIDE + CMP_LEN - 1
    k_c = compress_blocks(kc_raw, cmp_pos_k, w_cmp_k1, w_cmp_k2)
    k_c = partial_rope(rms_norm(k_c, g_k_cmp), positions[:, end_idx])
    v_c = compress_blocks(vc_raw, cmp_pos_v, w_cmp_v1, w_cmp_v2)

    k_s = partial_rope(rms_norm(k_s, g_k_slc), positions)
    kb = k_s.reshape(B, G, ns, SEL_BLOCK, hd)
    vb = v_s.reshape(B, G, ns, SEL_BLOCK, hd)

    k_w = partial_rope(rms_norm(k_w, g_k_win), positions)
    pad = ((0, 0), (0, 0), (WINDOW, 0), (0, 0))
    kw_pad, vw_pad = jnp.pad(k_w, pad), jnp.pad(v_w, pad)

    gates = jax.nn.sigmoid(gate_n.astype(jnp.float32)).reshape(B, S, NSA_HEADS, N_BRANCH)
    gates = gates.transpose(0, 2, 1, 3).astype(dt)

    sel_map = jnp.asarray(cmp_to_sel_map(nc, ns))
    end_idx_j = jnp.asarray(end_idx)
    blk = jnp.arange(ns)
    b_ix = jnp.arange(B)[:, None, None, None]
    g_ix = jnp.arange(G)[None, :, None, None]

    def query_block(i):
        start = i * Q_BLOCK
        t = start + jnp.arange(Q_BLOCK)
        qb = lax.dynamic_slice_in_dim(q, start, Q_BLOCK, axis=2).reshape(B, G, R, Q_BLOCK, hd)

        s_c = jnp.einsum('bgrqd,bgnd->bgrqn', qb, k_c, preferred_element_type=jnp.float32) * scale
        c_mask = end_idx_j[None, :] <= t[:, None]
        p_c = masked_softmax(s_c, c_mask)
        o_c = jnp.einsum('bgrqn,bgnd->bgrqd', p_c.astype(dt), v_c)

        imp = jnp.einsum('bgqn,nj->bgqj', jnp.sum(p_c, axis=2), sel_map)
        cur = t // SEL_BLOCK
        forced = (blk[None, :] == 0) | (blk[None, :] == cur[:, None]) | (blk[None, :] == cur[:, None] - 1)
        valid = blk[None, :] <= cur[:, None]
        imp = jnp.where(forced, FORCE_SCORE, jnp.where(valid, imp, -1.0))
        _, sel = lax.top_k(imp, top_n)
        ks_g = kb[b_ix, g_ix, sel]
        vs_g = vb[b_ix, g_ix, sel]
        s_pos = (sel[..., None] * SEL_BLOCK + jnp.arange(SEL_BLOCK)).reshape(B, G, Q_BLOCK, top_n * SEL_BLOCK)
        s_mask = (s_pos <= t[None, None, :, None])[:, :, None]
        s_s = jnp.einsum('bgrqd,bgqkld->bgrqkl', qb, ks_g, preferred_element_type=jnp.float32)
        s_s = s_s.reshape(B, G, R, Q_BLOCK, top_n * SEL_BLOCK) * scale
        p_s = masked_softmax(s_s, s_mask).reshape(B, G, R, Q_BLOCK, top_n, SEL_BLOCK)
        o_s = jnp.einsum('bgrqkl,bgqkld->bgrqd', p_s.astype(dt), vs_g)

        kw_blk = lax.dynamic_slice_in_dim(kw_pad, start, WINDOW + Q_BLOCK, axis=2)
        vw_blk = lax.dynamic_slice_in_dim(vw_pad, start, WINDOW + Q_BLOCK, axis=2)
        kpos = start - WINDOW + jnp.arange(WINDOW + Q_BLOCK)
        w_mask = (kpos[None, :] <= t[:, None]) & (kpos[None, :] > t[:, None] - WINDOW) & (kpos[None, :] >= 0)
        s_w = jnp.einsum('bgrqd,bgkd->bgrqk', qb, kw_blk, preferred_element_type=jnp.float32) * scale
        p_w = masked_softmax(s_w, w_mask)
        o_w = jnp.einsum('bgrqk,bgkd->bgrqd', p_w.astype(dt), vw_blk)

        gb = lax.dynamic_slice_in_dim(gates, start, Q_BLOCK, axis=2).reshape(B, G, R, Q_BLOCK, N_BRANCH)
        o = gb[..., 0:1] * o_c + gb[..., 1:2] * o_s + gb[..., 2:3] * o_w
        return o.reshape(B, NSA_HEADS, Q_BLOCK, hd)

    o_blocks = lax.map(query_block, jnp.arange(S // Q_BLOCK))
    return o_blocks.transpose(1, 0, 3, 2, 4).reshape(B, S, NSA_WIDTH)


def pool_mixer(v_p, w_pool, pool_scale):
    B, S, _ = v_p.shape
    v32 = v_p.reshape(B, S, POOL_GROUPS, POOL_GROUP_DIM).astype(jnp.float32)
    cs = jnp.cumsum(v32, axis=1)
    count_base = jnp.arange(1, S + 1, dtype=jnp.float32)
    pooled = []
    for g, w in enumerate(POOL_WINDOWS):
        c = cs[:, :, g]
        lower = jnp.pad(c[:, :S - w], ((0, 0), (w, 0), (0, 0)))
        cnt = jnp.minimum(count_base, float(w))[None, :, None]
        pooled.append((c - lower) / cnt - v32[:, :, g])
    pooled = jnp.stack(pooled, axis=2).astype(v_p.dtype)
    out = jnp.einsum('bsgc,gce->bsge', pooled, w_pool).reshape(B, S, POOL_WIDTH)
    return out * pool_scale


def memory_mixer(q_m, mem, g_mem, w_mem_kv, g_q_mem, g_k_mem):
    B, S, _ = q_m.shape
    M = mem.shape[1]
    scale = HEAD_DIM ** -0.5
    m_h = rms_norm(mem, g_mem)
    mkv = jnp.einsum('bmd,de->bme', m_h, w_mem_kv).reshape(B, M, 2, MEM_HEADS, HEAD_DIM)
    mk = rms_norm(mkv[:, :, 0], g_k_mem)
    mv = mkv[:, :, 1]
    mq = rms_norm(q_m.reshape(B, S, MEM_HEADS, HEAD_DIM), g_q_mem)
    s_m = jnp.einsum('bshd,bmhd->bhsm', mq, mk, preferred_element_type=jnp.float32) * scale
    p_m = jax.nn.softmax(s_m, axis=-1)
    return jnp.einsum('bhsm,bmhd->bshd', p_m.astype(q_m.dtype), mv).reshape(B, S, MEM_WIDTH)


def hybrid_layer(x, mem, positions, g_norm, w_in, g_q_nsa, g_k_cmp, g_k_slc, g_k_win,
                 cmp_pos_k, w_cmp_k1, w_cmp_k2, cmp_pos_v, w_cmp_v1, w_cmp_v2,
                 w_pool, pool_scale, g_mem, w_mem_kv, g_q_mem, g_k_mem, w_out):
    h = rms_norm(x, g_norm)
    proj = jnp.einsum('bsd,de->bse', h, w_in)
    split_at = np.cumsum(IN_SPLITS)[:-1].tolist()
    q_n, kv_n, gate_n, z_n, v_p, z_p, q_m, z_m = jnp.split(proj, split_at, axis=-1)
    o_nsa = nsa_mixer(q_n, kv_n, gate_n, positions, g_q_nsa, g_k_cmp, g_k_slc, g_k_win,
                      cmp_pos_k, w_cmp_k1, w_cmp_k2, cmp_pos_v, w_cmp_v1, w_cmp_v2)
    o_pool = pool_mixer(v_p, w_pool, pool_scale)
    o_mem = memory_mixer(q_m, mem, g_mem, w_mem_kv, g_q_mem, g_k_mem)
    y = jnp.concatenate([o_nsa * jax.nn.silu(z_n),
                         o_pool * jax.nn.silu(z_p),
                         o_mem * jax.nn.silu(z_m)], axis=-1)
    return x + jnp.einsum('bse,ed->bsd', y, w_out)


def setup_inputs(seed: int = 0) -> dict:
    key = jax.random.key(seed)
    ks = jax.random.split(key, 24)
    f32 = jnp.float32
    L = DEPTH

    def nrm(k, shape, s):
        return jax.random.normal(k, shape, f32) * s

    def gain(k, shape):
        return 1.0 + 0.01 * jax.random.normal(k, shape, f32)

    x = nrm(ks[0], (BATCH, SEQ, D_MODEL), 1.0)
    mem = nrm(ks[1], (BATCH, MEM_LEN, D_MODEL), 1.0)
    positions = (jnp.arange(SEQ, dtype=jnp.int32)[None, :]
                 + jax.random.randint(ks[2], (BATCH, 1), 0, 1024, dtype=jnp.int32))
    return {
        "x": x,
        "mem": mem,
        "positions": positions,
        "g_norm": gain(ks[3], (L, D_MODEL)),
        "w_in": nrm(ks[4], (L, D_MODEL, IN_WIDTH), D_MODEL ** -0.5),
        "g_q_nsa": gain(ks[5], (L, HEAD_DIM)),
        "g_k_cmp": gain(ks[6], (L, HEAD_DIM)),
        "g_k_slc": gain(ks[7], (L, HEAD_DIM)),
        "g_k_win": gain(ks[8], (L, HEAD_DIM)),
        "cmp_pos_k": nrm(ks[9], (L, CMP_LEN, HEAD_DIM), 0.1),
        "w_cmp_k1": nrm(ks[10], (L, CMP_LEN, HEAD_DIM, HEAD_DIM), (CMP_LEN * HEAD_DIM) ** -0.5),
        "w_cmp_k2": nrm(ks[11], (L, HEAD_DIM, HEAD_DIM), HEAD_DIM ** -0.5),
        "cmp_pos_v": nrm(ks[12], (L, CMP_LEN, HEAD_DIM), 0.1),
        "w_cmp_v1": nrm(ks[13], (L, CMP_LEN, HEAD_DIM, HEAD_DIM), (CMP_LEN * HEAD_DIM) ** -0.5),
        "w_cmp_v2": nrm(ks[14], (L, HEAD_DIM, HEAD_DIM), HEAD_DIM ** -0.5),
        "w_pool": nrm(ks[15], (L, POOL_GROUPS, POOL_GROUP_DIM, POOL_GROUP_DIM), POOL_GROUP_DIM ** -0.5),
        "pool_scale": gain(ks[16], (L, POOL_WIDTH)),
        "g_mem": gain(ks[17], (L, D_MODEL)),
        "w_mem_kv": nrm(ks[18], (L, D_MODEL, 2 * MEM_WIDTH), D_MODEL ** -0.5),
        "g_q_mem": gain(ks[19], (L, HEAD_DIM)),
        "g_k_mem": gain(ks[20], (L, HEAD_DIM)),
        "w_out": nrm(ks[21], (L, MIX_WIDTH, D_MODEL), MIX_WIDTH ** -0.5),
    }


def reference(x, mem, positions, g_norm, w_in, g_q_nsa, g_k_cmp, g_k_slc, g_k_win,
              cmp_pos_k, w_cmp_k1, w_cmp_k2, cmp_pos_v, w_cmp_v1, w_cmp_v2,
              w_pool, pool_scale, g_mem, w_mem_kv, g_q_mem, g_k_mem, w_out):
    for l in range(DEPTH):
        x = hybrid_layer(x, mem, positions, g_norm[l], w_in[l], g_q_nsa[l], g_k_cmp[l],
                         g_k_slc[l], g_k_win[l], cmp_pos_k[l], w_cmp_k1[l], w_cmp_k2[l],
                         cmp_pos_v[l], w_cmp_v1[l], w_cmp_v2[l], w_pool[l], pool_scale[l],
                         g_mem[l], w_mem_kv[l], g_q_mem[l], g_k_mem[l], w_out[l])
    return x
```

```python
import functools

import numpy as np
import jax
import jax.numpy as jnp
from jax import lax
from jax.experimental import pallas as pl
from jax.experimental.pallas import tpu as pltpu

F32 = jnp.float32
BF16 = jnp.bfloat16

D_MODEL = 1024
HEAD_DIM = 64
NSA_HEADS = 8
NSA_KV_HEADS = 2
NSA_GROUP = NSA_HEADS // NSA_KV_HEADS
NSA_WIDTH = NSA_HEADS * HEAD_DIM
KV_WIDTH = NSA_KV_HEADS * HEAD_DIM
N_BRANCH = 3
CMP_LEN = 32
CMP_STRIDE = 16
SEL_BLOCK = 64
SEL_TOPK = 16
WINDOW = 512
FORCE_SCORE = 1e4
POOL_WINDOWS = (2, 4, 8, 16)
POOL_WIDTH = 256
MEM_HEADS = 4
MEM_WIDTH = MEM_HEADS * HEAD_DIM
ROPE_THETA = 500000.0
ROPE_DIM = HEAD_DIM // 4
ROPE_HALF = ROPE_DIM // 2
EPS = 1e-6
SCALE = HEAD_DIM ** -0.5

LANES = 128
NS_PAD = LANES
MASK_BIAS = -1e9
POOL_HALO = 16
TINY = float(np.finfo(np.float32).tiny)

_SEG = dict(q=(0, 512), kv=(512, 768), gate=(1280, 256), zn=(1536, 512), vp=(2048, 256),
            zp=(2304, 256), qm=(2560, 256), zm=(2816, 256))
W_ALL_WIDTH = 3072

_NT = (((1,), (1,)), ((), ()))


def _dot(a, b):
    return jnp.dot(a, b, preferred_element_type=F32)


def _dot_nt(a, b):
    return lax.dot_general(a, b, _NT, preferred_element_type=F32)


def _vmem_params(sem):
    return pltpu.CompilerParams(dimension_semantics=sem, vmem_limit_bytes=56 * 1024 * 1024)


def _rope_tables(pos_f32, invf):
    ang = pos_f32 * invf
    c, s = jnp.cos(ang), jnp.sin(ang)
    d = lax.broadcasted_iota(jnp.int32, ang.shape, 1) % HEAD_DIM
    tc = jnp.where(d < ROPE_DIM, c, 1.0)
    ts_up = jnp.where((d >= ROPE_HALF) & (d < ROPE_DIM), s, 0.0)
    ts_dn = jnp.where(d < ROPE_HALF, -s, 0.0)
    return tc, ts_up, ts_dn


def _rope(x, tables):
    tc, ts_up, ts_dn = tables
    w = x.shape[1]
    reps = w // LANES
    if reps > 1:
        tc, ts_up, ts_dn = (jnp.concatenate([t] * reps, axis=1) for t in (tc, ts_up, ts_dn))
    return (x * tc + pltpu.roll(x, ROPE_HALF, 1) * ts_up
            + pltpu.roll(x, w - ROPE_HALF, 1) * ts_dn)


def _head_rms(x, ones_bd, gain):
    ss = _dot((x * x).astype(BF16), ones_bd)
    return x * lax.rsqrt(ss * (1.0 / HEAD_DIM) + EPS) * gain


def _dup_halves(x):
    low = lax.broadcasted_iota(jnp.int32, x.shape, 1) < HEAD_DIM
    r = pltpu.roll(x, HEAD_DIM, 1)
    return jnp.where(low, x, r), jnp.where(low, r, x)


def _silu(z):
    return z * jax.nn.sigmoid(z)


def _in_proj_kernel(x_ref, pos_ref, gn_ref, w_ref, bd512_ref, bd256_ref, gq_ref, gks_ref, gkw_ref,
                    gqm_ref, invf_ref,
                    q_ref, kc_ref, vc_ref, ks_ref, vs_ref, kw_ref, vw_ref, sig_ref, zn_ref, vp_ref,
                    zp_ref, qm_ref, zm_ref):
    x = x_ref[...]
    h = x * lax.rsqrt(jnp.mean(x * x, axis=-1, keepdims=True) + EPS) * gn_ref[...]
    hb = h.astype(BF16)

    def seg(name):
        o, w = _SEG[name]
        return _dot(hb, w_ref[:, o:o + w])

    tables = _rope_tables(pos_ref[...].astype(F32), invf_ref[...])

    q = _head_rms(seg('q'), bd512_ref[...], gq_ref[...])
    q_ref[...] = (_rope(q, tables) * SCALE).astype(BF16)

    kv = seg('kv')
    kc_ref[...] = kv[:, 0:128]
    vc_ref[...] = kv[:, 128:256]

    def norm_dup(k, gain):
        k = k * lax.rsqrt(jnp.mean(k * k, axis=-1, keepdims=True) + EPS) * gain
        return _rope(k, tables).astype(BF16)

    ks0, ks1 = _dup_halves(kv[:, 256:384])
    ks_ref[0] = norm_dup(ks0, gks_ref[...])
    ks_ref[1] = norm_dup(ks1, gks_ref[...])
    vs0, vs1 = _dup_halves(kv[:, 384:512])
    vs_ref[0] = vs0.astype(BF16)
    vs_ref[1] = vs1.astype(BF16)
    kw0, kw1 = _dup_halves(kv[:, 512:640])
    kw_ref[0] = norm_dup(kw0, gkw_ref[...])
    kw_ref[1] = norm_dup(kw1, gkw_ref[...])
    vw0, vw1 = _dup_halves(kv[:, 640:768])
    vw_ref[0] = vw0.astype(BF16)
    vw_ref[1] = vw1.astype(BF16)

    sig_ref[...] = jax.nn.sigmoid(seg('gate'))
    zn_ref[...] = _silu(seg('zn'))
    vp_ref[...] = seg('vp')
    zp_ref[...] = _silu(seg('zp'))
    qm = _head_rms(seg('qm'), bd256_ref[...], gqm_ref[...])
    qm_ref[...] = (qm * SCALE).astype(BF16)
    zm_ref[...] = _silu(seg('zm'))


def _in_proj(x2, pos2, gn, w_all, bd512, bd256, gq, gks, gkw, gqm, invf, tm):
    n = x2.shape[0]
    row = lambda w: pl.BlockSpec((tm, w), lambda i: (i, 0))
    full = lambda a: pl.BlockSpec(a.shape, lambda i: (0,) * a.ndim)
    dup = pl.BlockSpec((NSA_KV_HEADS, tm, LANES), lambda i: (0, i, 0))
    sd = jax.ShapeDtypeStruct
    out_shape = [sd((n, 512), BF16), sd((n, 128), F32), sd((n, 128), F32),
                 sd((2, n, 128), BF16), sd((2, n, 128), BF16), sd((2, n, 128), BF16),
                 sd((2, n, 128), BF16), sd((n, 256), F32), sd((n, 512), F32), sd((n, 256), F32),
                 sd((n, 256), F32), sd((n, 256), BF16), sd((n, 256), F32)]
    out_specs = [row(512), row(128), row(128), dup, dup, dup, dup, row(256), row(512), row(256),
                 row(256), row(256), row(256)]
    consts = (gn, w_all, bd512, bd256, gq, gks, gkw, gqm, invf)
    return pl.pallas_call(
        _in_proj_kernel,
        grid=(n // tm,),
        in_specs=[row(D_MODEL), row(1)] + [full(a) for a in consts],
        out_specs=out_specs,
        out_shape=out_shape,
        compiler_params=_vmem_params(("parallel",)),
    )(x2, pos2, *consts)


def _compress_kernel(zk_ref, zv_ref, pend_ref, pak_ref, pbk_ref, w1ak_ref, w1bk_ref, w2k_ref,
                     pav_ref, pbv_ref, w1av_ref, w1bv_ref, w2v_ref, bd128_ref, gk_ref, invf_ref,
                     kc_ref, vc_ref):
    ncp = zk_ref.shape[1]

    def mlp(z, pa, pb, w1a, w1b, w2):
        a = _dot((z + pa).astype(BF16), w1a)
        b = _dot((z + pb).astype(BF16), w1b)
        pre = a + pltpu.roll(b, ncp - 1, 0)
        return _dot(jax.nn.gelu(pre).astype(BF16), w2)

    kc = mlp(zk_ref[0], pak_ref[...], pbk_ref[...], w1ak_ref[...], w1bk_ref[...], w2k_ref[...])
    kc = _head_rms(kc, bd128_ref[...], gk_ref[...])
    kc = _rope(kc, _rope_tables(pend_ref[0].astype(F32), invf_ref[...]))
    k0, k1 = _dup_halves(kc)
    kc_ref[0, 0] = k0.astype(BF16)
    kc_ref[0, 1] = k1.astype(BF16)
    vc = mlp(zv_ref[0], pav_ref[...], pbv_ref[...], w1av_ref[...], w1bv_ref[...], w2v_ref[...])
    v0, v1 = _dup_halves(vc)
    vc_ref[0, 0] = v0.astype(BF16)
    vc_ref[0, 1] = v1.astype(BF16)


def _compress(zk, zv, pend, consts):
    b, ncp, zw = zk.shape
    full = lambda a: pl.BlockSpec(a.shape, lambda i: (0,) * a.ndim)
    zspec = pl.BlockSpec((1, ncp, zw), lambda i: (i, 0, 0))
    ospec = pl.BlockSpec((1, NSA_KV_HEADS, ncp, LANES), lambda i: (i, 0, 0, 0))
    sd = jax.ShapeDtypeStruct((b, NSA_KV_HEADS, ncp, LANES), BF16)
    return pl.pallas_call(
        _compress_kernel,
        grid=(b,),
        in_specs=[zspec, zspec, pl.BlockSpec((1, ncp, 1), lambda i: (i, 0, 0))] + [full(a) for a in consts],
        out_specs=[ospec, ospec],
        out_shape=[sd, sd],
        compiler_params=_vmem_params(("parallel",)),
    )(zk, zv, pend, *consts)


def _head_operand(q_blk, r):
    pair = q_blk[:, (r // 2) * LANES:(r // 2 + 1) * LANES]
    low = lax.broadcasted_iota(jnp.int32, pair.shape, 1) < HEAD_DIM
    keep = low if r % 2 == 0 else jnp.logical_not(low)
    return jnp.where(keep, pair, jnp.zeros_like(pair))


def _pack_heads(outs):
    low = lax.broadcasted_iota(jnp.int32, outs[0].shape, 1) < HEAD_DIM
    return jnp.concatenate([jnp.where(low, outs[0], outs[1]), jnp.where(low, outs[2], outs[3])], axis=1)


def _gate_col(sig_blk, r, branch):
    c = branch * NSA_GROUP + r
    return sig_blk[:, c:c + 1]


def _cmp_kernel(q_ref, kc_ref, vc_ref, sig_ref, selt_ref, o_ref, bias_ref):
    tq = q_ref.shape[1]
    ncp = kc_ref.shape[2]
    t0 = pl.program_id(2) * tq
    q_blk = q_ref[0]
    kc = kc_ref[0, 0]
    vc = vc_ref[0, 0]
    sig = sig_ref[0]
    t = t0 + lax.broadcasted_iota(jnp.int32, (tq, ncp), 0)
    n = lax.broadcasted_iota(jnp.int32, (tq, ncp), 1)
    cmask = n * CMP_STRIDE + (CMP_LEN - 1) <= t
    psum = jnp.zeros((tq, ncp), F32)
    outs = []
    for r in range(NSA_GROUP):
        s = _dot_nt(_head_operand(q_blk, r), kc)
        s = jnp.where(cmask, s, -jnp.inf)
        m = jnp.max(s, axis=-1, keepdims=True)
        m = jnp.where(m == -jnp.inf, 0.0, m)
        e = jnp.where(cmask, jnp.exp(s - m), 0.0)
        p = e / jnp.maximum(jnp.sum(e, axis=-1, keepdims=True), TINY)
        psum = psum + p
        outs.append(_dot(p.astype(BF16), vc) * _gate_col(sig, r, 0))
    o_ref[0] = _pack_heads(outs)

    ps_hi = psum.astype(BF16)
    ps_lo = (psum - ps_hi.astype(F32)).astype(BF16)
    selt = selt_ref[...]
    imp = _dot_nt(selt, ps_hi) + _dot_nt(selt, ps_lo)
    j = lax.broadcasted_iota(jnp.int32, (NS_PAD, tq), 0)
    cur = (t0 + lax.broadcasted_iota(jnp.int32, (NS_PAD, tq), 1)) >> 6
    forced = (j == 0) | (j == cur) | (j == cur - 1)
    valid = j <= cur
    vals = jnp.where(forced, FORCE_SCORE, jnp.where(valid, imp, -1.0))
    chosen = jnp.zeros((NS_PAD, tq), F32)
    for _ in range(SEL_TOPK):
        m = jnp.max(vals, axis=0, keepdims=True)
        idx = jnp.min(jnp.where(vals == m, j, NS_PAD), axis=0, keepdims=True)
        hit = j == idx
        chosen = jnp.where(hit, 1.0, chosen)
        vals = jnp.where(hit, -jnp.inf, vals)
    bias_t = jnp.where((chosen > 0.0) & valid, 0.0, MASK_BIAS)
    bias_ref[0, 0] = bias_t.T.astype(BF16)


def _cmp_attention(q3, kc, vc, sig3, selt, tq):
    b, s, _ = q3.shape
    ncp = kc.shape[2]
    grid = (b, NSA_KV_HEADS, s // tq)
    return pl.pallas_call(
        _cmp_kernel,
        grid=grid,
        in_specs=[pl.BlockSpec((1, tq, 256), lambda bi, g, i: (bi, i, g)),
                  pl.BlockSpec((1, 1, ncp, LANES), lambda bi, g, i: (bi, g, 0, 0)),
                  pl.BlockSpec((1, 1, ncp, LANES), lambda bi, g, i: (bi, g, 0, 0)),
                  pl.BlockSpec((1, tq, LANES), lambda bi, g, i: (bi, i, g)),
                  pl.BlockSpec(selt.shape, lambda bi, g, i: (0, 0))],
        out_specs=[pl.BlockSpec((1, tq, 256), lambda bi, g, i: (bi, i, g)),
                   pl.BlockSpec((1, 1, tq, LANES), lambda bi, g, i: (bi, g, i, 0))],
        out_shape=[jax.ShapeDtypeStruct((b, s, NSA_WIDTH), F32),
                   jax.ShapeDtypeStruct((b, NSA_KV_HEADS, s, LANES), BF16)],
        compiler_params=_vmem_params(("parallel", "parallel", "parallel")),
    )(q3, kc, vc, sig3, selt)


def _sel_kernel(qt_tbl, kt_tbl, q_ref, bias_ref, et_ref, ks_ref, vs_ref, sig_ref, o_ref,
                qaug_ref, m_ref, l_ref, acc_ref):
    tq = q_ref.shape[1]
    tk = ks_ref.shape[2]
    p_id = pl.program_id(2)
    qt = qt_tbl[p_id]
    kt = kt_tbl[p_id]

    @pl.when(kt == 0)
    def _():
        q_blk = q_ref[0]
        bias = bias_ref[0, 0]
        for r in range(NSA_GROUP):
            qaug_ref[r * tq:(r + 1) * tq, 0:LANES] = bias
            qaug_ref[r * tq:(r + 1) * tq, LANES:2 * LANES] = _head_operand(q_blk, r)
        m_ref[...] = jnp.full(m_ref.shape, -jnp.inf, F32)
        l_ref[...] = jnp.zeros(l_ref.shape, F32)
        acc_ref[...] = jnp.zeros(acc_ref.shape, F32)

    kaug = jnp.concatenate([et_ref[...], ks_ref[0, 0]], axis=1)
    s = _dot_nt(qaug_ref[...], kaug)
    rows = NSA_GROUP * tq
    t = qt * tq + (lax.broadcasted_iota(jnp.int32, (rows, tk), 0) & (tq - 1))
    kpos = kt * tk + lax.broadcasted_iota(jnp.int32, (rows, tk), 1)
    s = jnp.where(kpos <= t, s, MASK_BIAS)
    m_prev = m_ref[...]
    m_new = jnp.maximum(m_prev, jnp.max(s, axis=-1, keepdims=True))
    alpha = jnp.exp(m_prev - m_new)
    p = jnp.exp(s - m_new)
    l_ref[...] = alpha * l_ref[...] + jnp.sum(p, axis=-1, keepdims=True)
    acc_ref[...] = alpha * acc_ref[...] + _dot(p.astype(BF16), vs_ref[0, 0])
    m_ref[...] = m_new

    @pl.when(kt == qt)
    def _():
        o = acc_ref[...] / l_ref[...]
        sig = sig_ref[0]
        outs = [o[r * tq:(r + 1) * tq] * _gate_col(sig, r, 1) for r in range(NSA_GROUP)]
        o_ref[0] = _pack_heads(outs)


def _sel_attention(q3, bias, et, ks, vs, sig3, t_s):
    b, s, _ = q3.shape
    nt = s // t_s
    pairs = [(qi, ki) for qi in range(nt) for ki in range(qi + 1)]
    qt_tbl = jnp.asarray(np.array([p[0] for p in pairs], np.int32))
    kt_tbl = jnp.asarray(np.array([p[1] for p in pairs], np.int32))
    rows = NSA_GROUP * t_s
    grid_spec = pltpu.PrefetchScalarGridSpec(
        num_scalar_prefetch=2,
        grid=(b, NSA_KV_HEADS, len(pairs)),
        in_specs=[pl.BlockSpec((1, t_s, 256), lambda bi, g, p, qt, kt: (bi, qt[p], g)),
                  pl.BlockSpec((1, 1, t_s, LANES), lambda bi, g, p, qt, kt: (bi, g, qt[p], 0)),
                  pl.BlockSpec((t_s, LANES), lambda bi, g, p, qt, kt: (kt[p], 0)),
                  pl.BlockSpec((1, 1, t_s, LANES), lambda bi, g, p, qt, kt: (g, bi, kt[p], 0)),
                  pl.BlockSpec((1, 1, t_s, LANES), lambda bi, g, p, qt, kt: (g, bi, kt[p], 0)),
                  pl.BlockSpec((1, t_s, LANES), lambda bi, g, p, qt, kt: (bi, qt[p], g))],
        out_specs=pl.BlockSpec((1, t_s, 256), lambda bi, g, p, qt, kt: (bi, qt[p], g)),
        scratch_shapes=[pltpu.VMEM((rows, 2 * LANES), BF16), pltpu.VMEM((rows, 1), F32),
                        pltpu.VMEM((rows, 1), F32), pltpu.VMEM((rows, LANES), F32)],
    )
    return pl.pallas_call(
        _sel_kernel,
        grid_spec=grid_spec,
        out_shape=jax.ShapeDtypeStruct((b, s, NSA_WIDTH), F32),
        compiler_params=_vmem_params(("parallel", "parallel", "arbitrary")),
    )(qt_tbl, kt_tbl, q3, bias, et, ks, vs, sig3)


def _win_kernel(q_ref, kp_ref, kc_ref, vp_ref, vc_ref, sig_ref, o_ref):
    tq = q_ref.shape[1]
    qt = pl.program_id(2)
    q_blk = q_ref[0]
    sig = sig_ref[0]
    i = lax.broadcasted_iota(jnp.int32, (tq, tq), 0)
    c = lax.broadcasted_iota(jnp.int32, (tq, tq), 1)
    mask_cur = c <= i
    mask_prev = c > i + jnp.where(qt > 0, 0, tq)
    outs = []
    for r in range(NSA_GROUP):
        qh = _head_operand(q_blk, r)
        s_p = jnp.where(mask_prev, _dot_nt(qh, kp_ref[0, 0]), -jnp.inf)
        s_c = jnp.where(mask_cur, _dot_nt(qh, kc_ref[0, 0]), -jnp.inf)
        m = jnp.maximum(jnp.max(s_p, axis=-1, keepdims=True), jnp.max(s_c, axis=-1, keepdims=True))
        e_p = jnp.exp(s_p - m)
        e_c = jnp.exp(s_c - m)
        l = jnp.sum(e_p, axis=-1, keepdims=True) + jnp.sum(e_c, axis=-1, keepdims=True)
        o = _dot(e_p.astype(BF16), vp_ref[0, 0]) + _dot(e_c.astype(BF16), vc_ref[0, 0])
        outs.append(o / l * _gate_col(sig, r, 2))
    o_ref[0] = _pack_heads(outs)


def _win_attention(q3, kw, vw, sig3):
    b, s, _ = q3.shape
    tq = WINDOW
    prev = pl.BlockSpec((1, 1, tq, LANES), lambda bi, g, i: (g, bi, jnp.maximum(i - 1, 0), 0))
    cur = pl.BlockSpec((1, 1, tq, LANES), lambda bi, g, i: (g, bi, i, 0))
    return pl.pallas_call(
        _win_kernel,
        grid=(b, NSA_KV_HEADS, s // tq),
        in_specs=[pl.BlockSpec((1, tq, 256), lambda bi, g, i: (bi, i, g)), prev, cur, prev, cur,
                  pl.BlockSpec((1, tq, LANES), lambda bi, g, i: (bi, i, g))],
        out_specs=pl.BlockSpec((1, tq, 256), lambda bi, g, i: (bi, i, g)),
        out_shape=jax.ShapeDtypeStruct((b, s, NSA_WIDTH), F32),
        compiler_params=_vmem_params(("parallel", "parallel", "parallel")),
    )(q3, kw, kw, vw, vw, sig3)


def _mem_kv_kernel(mem_ref, gm_ref, w_ref, bd256_ref, gk_ref, mk_ref, mv_ref):
    x = mem_ref[0]
    h = x * lax.rsqrt(jnp.mean(x * x, axis=-1, keepdims=True) + EPS) * gm_ref[...]
    mkv = _dot(h.astype(BF16), w_ref[...])
    mk_ref[0] = _head_rms(mkv[:, 0:MEM_WIDTH], bd256_ref[...], gk_ref[...]).astype(BF16)
    mv_ref[0] = mkv[:, MEM_WIDTH:2 * MEM_WIDTH].astype(BF16)


def _mem_kv(mem, gm, w, bd256, gk):
    b, m, _ = mem.shape
    full = lambda a: pl.BlockSpec(a.shape, lambda i: (0,) * a.ndim)
    ospec = pl.BlockSpec((1, m, MEM_WIDTH), lambda i: (i, 0, 0))
    sd = jax.ShapeDtypeStruct((b, m, MEM_WIDTH), BF16)
    return pl.pallas_call(
        _mem_kv_kernel,
        grid=(b,),
        in_specs=[pl.BlockSpec((1, m, D_MODEL), lambda i: (i, 0, 0)), full(gm), full(w), full(bd256), full(gk)],
        out_specs=[ospec, ospec],
        out_shape=[sd, sd],
        compiler_params=_vmem_params(("parallel",)),
    )(mem, gm, w, bd256, gk)


def _mem_attn_kernel(q_ref, mk_ref, mv_ref, o_ref):
    q_blk = q_ref[0]
    mk = mk_ref[0]
    mv = mv_ref[0]
    outs = []
    for h in range(MEM_HEADS):
        hp = h // 2
        s = _dot_nt(_head_operand(q_blk, h), mk[:, hp * LANES:(hp + 1) * LANES])
        m = jnp.max(s, axis=-1, keepdims=True)
        e = jnp.exp(s - m)
        p = e / jnp.sum(e, axis=-1, keepdims=True)
        outs.append(_dot(p.astype(BF16), mv[:, hp * LANES:(hp + 1) * LANES]))
    low = lax.broadcasted_iota(jnp.int32, outs[0].shape, 1) < HEAD_DIM
    o_ref[0] = jnp.concatenate([jnp.where(low, outs[0], outs[1]), jnp.where(low, outs[2], outs[3])], axis=1)


def _mem_attention(qm3, mk, mv, tq):
    b, s, _ = qm3.shape
    m = mk.shape[1]
    kv = pl.BlockSpec((1, m, MEM_WIDTH), lambda bi, i: (bi, 0, 0))
    return pl.pallas_call(
        _mem_attn_kernel,
        grid=(b, s // tq),
        in_specs=[pl.BlockSpec((1, tq, MEM_WIDTH), lambda bi, i: (bi, i, 0)), kv, kv],
        out_specs=pl.BlockSpec((1, tq, MEM_WIDTH), lambda bi, i: (bi, i, 0)),
        out_shape=jax.ShapeDtypeStruct((b, s, MEM_WIDTH), F32),
        compiler_params=_vmem_params(("parallel", "parallel")),
    )(qm3, mk, mv)


def _out_kernel(seq_len, x_ref, oc_ref, os_ref, ow_ref, zn_ref, vp_ref, halo_ref, zp_ref, om_ref,
                zm_ref, wo_ref, wpool_ref, pscale_ref, o_ref):
    tm = x_ref.shape[0]
    t0 = (pl.program_id(0) * tm) % seq_len
    y_nsa = (oc_ref[...] + os_ref[...] + ow_ref[...]) * zn_ref[...]

    v = vp_ref[...]
    halo = halo_ref[...] * (t0 > 0).astype(F32)
    e = jnp.concatenate([halo, v], axis=0)
    w2 = e + pltpu.roll(e, 1, 0)
    w4 = w2 + pltpu.roll(w2, 2, 0)
    w8 = w4 + pltpu.roll(w4, 4, 0)
    w16 = w8 + pltpu.roll(w8, 8, 0)
    shape = (tm, POOL_WIDTH)
    grp = lax.broadcasted_iota(jnp.int32, shape, 1) >> 6
    wsum = jnp.where(grp == 0, w2[POOL_HALO:],
                     jnp.where(grp == 1, w4[POOL_HALO:], jnp.where(grp == 2, w8[POOL_HALO:], w16[POOL_HALO:])))
    width = jnp.left_shift(2, grp)
    cnt = jnp.minimum(t0 + lax.broadcasted_iota(jnp.int32, shape, 0) + 1, width).astype(F32)
    pooled = wsum / cnt - v
    y_pool = _dot(pooled.astype(BF16), wpool_ref[...]) * pscale_ref[...] * zp_ref[...]

    y_mem = om_ref[...] * zm_ref[...]
    out = x_ref[...] + _dot(y_nsa.astype(BF16), wo_ref[0:512, :])
    out = out + _dot(y_pool.astype(BF16), wo_ref[512:768, :])
    out = out + _dot(y_mem.astype(BF16), wo_ref[768:1024, :])
    o_ref[...] = out


def _out_proj(x2, oc, os_, ow, zn, vp, zp, om, zm, wo, wpool, pscale, seq_len, tm):
    n = x2.shape[0]
    row = lambda w: pl.BlockSpec((tm, w), lambda i: (i, 0))
    full = lambda a: pl.BlockSpec(a.shape, lambda i: (0,) * a.ndim)
    hb = tm // POOL_HALO
    halo = pl.BlockSpec((POOL_HALO, POOL_WIDTH), lambda i: (jnp.maximum(i * hb - 1, 0), 0))
    return pl.pallas_call(
        functools.partial(_out_kernel, seq_len),
        grid=(n // tm,),
        in_specs=[row(D_MODEL), row(512), row(512), row(512), row(512), row(256), halo, row(256),
                  row(256), row(256), full(wo), full(wpool), full(pscale)],
        out_specs=row(D_MODEL),
        out_shape=jax.ShapeDtypeStruct((n, D_MODEL), F32),
        compiler_params=_vmem_params(("parallel",)),
    )(x2, oc, os_, ow, zn, vp, vp, zp, om, zm, wo, wpool, pscale)


def _block_diag_ones(width):
    i = np.arange(width) // HEAD_DIM
    return jnp.asarray((i[:, None] == i[None, :]).astype(np.float32), dtype=BF16)


def _pack_w_in(w_in):
    q, kv, gate, zn, vp, zp, qm, zm = jnp.split(
        w_in, np.cumsum([512, 768, 24, 512, 256, 256, 256, 256])[:-1].tolist(), axis=1)
    cols = np.zeros((NSA_KV_HEADS, LANES), np.int64)
    used = np.zeros((NSA_KV_HEADS, LANES), bool)
    for g in range(NSA_KV_HEADS):
        for k in range(N_BRANCH):
            for r in range(NSA_GROUP):
                cols[g, k * NSA_GROUP + r] = (g * NSA_GROUP + r) * N_BRANCH + k
                used[g, k * NSA_GROUP + r] = True
    gate_p = jnp.where(jnp.asarray(used.reshape(-1))[None, :], gate[:, cols.reshape(-1)], 0.0)
    return jnp.concatenate([q, kv, gate_p, zn, vp, zp, qm, zm], axis=1).astype(BF16)


def _pack_cmp(pos, w1, w2):
    half = CMP_LEN // 2

    def w1_half(wh):
        z = jnp.zeros((half, NSA_KV_HEADS, HEAD_DIM, NSA_KV_HEADS, HEAD_DIM), F32)
        for g in range(NSA_KV_HEADS):
            z = z.at[:, g, :, g, :].set(wh)
        return z.reshape(half * KV_WIDTH, KV_WIDTH).astype(BF16)

    def pos_half(ph):
        return jnp.tile(ph[:, None, :], (1, NSA_KV_HEADS, 1)).reshape(1, half * KV_WIDTH)

    w2_bd = jnp.zeros((NSA_KV_HEADS, HEAD_DIM, NSA_KV_HEADS, HEAD_DIM), F32)
    for g in range(NSA_KV_HEADS):
        w2_bd = w2_bd.at[g, :, g, :].set(w2)
    return (pos_half(pos[:half]), pos_half(pos[half:]), w1_half(w1[:half]), w1_half(w1[half:]),
            w2_bd.reshape(KV_WIDTH, KV_WIDTH).astype(BF16))


def _sel_map_t(ncp, s):
    nc = (s - CMP_LEN) // CMP_STRIDE + 1
    ns = s // SEL_BLOCK
    c0 = np.arange(nc) * CMP_STRIDE
    c1 = c0 + CMP_LEN
    s0 = np.arange(ns) * SEL_BLOCK
    s1 = s0 + SEL_BLOCK
    ov = np.clip(np.minimum(c1[:, None], s1[None, :]) - np.maximum(c0[:, None], s0[None, :]), 0, None)
    m = np.zeros((NS_PAD, ncp), np.float32)
    m[:ns, :nc] = (ov / CMP_LEN).T
    return jnp.asarray(m, dtype=BF16)


def kernel(x, mem, positions, g_norm, w_in, g_q_nsa, g_k_cmp, g_k_slc, g_k_win, cmp_pos_k, w_cmp_k1,
           w_cmp_k2, cmp_pos_v, w_cmp_v1, w_cmp_v2, w_pool, pool_scale, g_mem, w_mem_kv, g_q_mem,
           g_k_mem, w_out):
    b, s, d = x.shape
    depth = g_norm.shape[0]
    assert d == D_MODEL and s % WINDOW == 0 and s // SEL_BLOCK >= SEL_TOPK and s // SEL_BLOCK <= NS_PAD
    n = b * s
    ncp = s // CMP_STRIDE
    tm = 256
    t_cmp = 256
    t_sel = 256
    t_mem = 512

    inv_freq = ROPE_THETA ** (-jnp.arange(ROPE_HALF, dtype=F32) / ROPE_HALF)
    invf = inv_freq[(np.arange(LANES) % HEAD_DIM) % ROPE_HALF][None, :]
    bd512, bd256, bd128 = _block_diag_ones(512), _block_diag_ones(256), _block_diag_ones(128)
    pos2 = positions.reshape(n, 1)
    end_idx = np.minimum(np.arange(ncp) * CMP_STRIDE + CMP_LEN - 1, s - 1)
    pend = positions[:, end_idx][:, :, None]
    selt = _sel_map_t(ncp, s)
    et = jnp.asarray((np.arange(s)[:, None] // SEL_BLOCK == np.arange(NS_PAD)[None, :]).astype(np.float32),
                     dtype=BF16)
    tile = lambda g, reps: jnp.tile(g[None, :], (1, reps))

    for l in range(depth):
        x2 = x.reshape(n, d)
        w_all = _pack_w_in(w_in[l])
        (q, kc_raw, vc_raw, ks, vs, kw, vw, sig, zn, vp, zp, qm, zm) = _in_proj(
            x2, pos2, g_norm[l][None, :], w_all, bd512, bd256, tile(g_q_nsa[l], 8), tile(g_k_slc[l], 2),
            tile(g_k_win[l], 2), tile(g_q_mem[l], 4), invf, tm)

        cmp_consts = (_pack_cmp(cmp_pos_k[l], w_cmp_k1[l], w_cmp_k2[l])
                      + _pack_cmp(cmp_pos_v[l], w_cmp_v1[l], w_cmp_v2[l])
                      + (bd128, tile(g_k_cmp[l], 2), invf))
        kc, vc = _compress(kc_raw.reshape(b, ncp, CMP_STRIDE * KV_WIDTH),
                           vc_raw.reshape(b, ncp, CMP_STRIDE * KV_WIDTH), pend, cmp_consts)

        q3 = q.reshape(b, s, NSA_WIDTH)
        sig3 = sig.reshape(b, s, 256)
        o_c, bias = _cmp_attention(q3, kc, vc, sig3, selt, t_cmp)
        dup4 = lambda a: a.reshape(NSA_KV_HEADS, b, s, LANES)
        o_s = _sel_attention(q3, bias, et, dup4(ks), dup4(vs), sig3, t_sel)
        o_w = _win_attention(q3, dup4(kw), dup4(vw), sig3)

        mk, mv = _mem_kv(mem, g_mem[l][None, :], w_mem_kv[l].astype(BF16), bd256, tile(g_k_mem[l], 4))
        o_m = _mem_attention(qm.reshape(b, s, MEM_WIDTH), mk, mv, t_mem)

        wpool = jnp.zeros((4, HEAD_DIM, 4, HEAD_DIM), F32)
        for g in range(4):
            wpool = wpool.at[g, :, g, :].set(w_pool[l, g])
        out = _out_proj(x2, o_c.reshape(n, 512), o_s.reshape(n, 512), o_w.reshape(n, 512), zn, vp, zp,
                        o_m.reshape(n, MEM_WIDTH), zm, w_out[l].astype(BF16),
                        wpool.reshape(POOL_WIDTH, POOL_WIDTH).astype(BF16), pool_scale[l][None, :], s, tm)
        x = out.reshape(b, s, d)
    return x
```

```python
import functools

import numpy as np
import jax
import jax.numpy as jnp
from jax import lax
from jax.experimental import pallas as pl
from jax.experimental.pallas import tpu as pltpu

F32 = jnp.float32
BF16 = jnp.bfloat16

D_MODEL = 1024
HEAD_DIM = 64
NSA_HEADS = 8
NSA_KV_HEADS = 2
NSA_GROUP = NSA_HEADS // NSA_KV_HEADS
NSA_WIDTH = NSA_HEADS * HEAD_DIM
KV_WIDTH = NSA_KV_HEADS * HEAD_DIM
N_BRANCH = 3
CMP_LEN = 32
CMP_STRIDE = 16
SEL_BLOCK = 64
SEL_TOPK = 16
WINDOW = 512
FORCE_SCORE = 1e4
POOL_WINDOWS = (2, 4, 8, 16)
POOL_WIDTH = 256
MEM_HEADS = 4
MEM_WIDTH = MEM_HEADS * HEAD_DIM
ROPE_THETA = 500000.0
ROPE_DIM = HEAD_DIM // 4
ROPE_HALF = ROPE_DIM // 2
EPS = 1e-6
SCALE = HEAD_DIM ** -0.5

LANES = 128
NS_PAD = LANES
MASK_BIAS = -1e9
POOL_HALO = 16
TINY = float(np.finfo(np.float32).tiny)

_SEG = dict(q=(0, 512), kv=(512, 768), gate=(1280, 256), zn=(1536, 512), vp=(2048, 256),
            zp=(2304, 256), qm=(2560, 256), zm=(2816, 256))
W_ALL_WIDTH = 3072

_NT = (((1,), (1,)), ((), ()))


def _dot(a, b):
    return jnp.dot(a, b, preferred_element_type=F32)


def _dot_nt(a, b):
    return lax.dot_general(a, b, _NT, preferred_element_type=F32)


def _vmem_params(sem):
    return pltpu.CompilerParams(dimension_semantics=sem, vmem_limit_bytes=56 * 1024 * 1024)


def _rope_tables(pos_f32, invf):
    ang = pos_f32 * invf
    c, s = jnp.cos(ang), jnp.sin(ang)
    d = lax.broadcasted_iota(jnp.int32, ang.shape, 1) % HEAD_DIM
    tc = jnp.where(d < ROPE_DIM, c, 1.0)
    ts_up = jnp.where((d >= ROPE_HALF) & (d < ROPE_DIM), s, 0.0)
    ts_dn = jnp.where(d < ROPE_HALF, -s, 0.0)
    return tc, ts_up, ts_dn


def _rope(x, tables):
    tc, ts_up, ts_dn = tables
    w = x.shape[1]
    reps = w // LANES
    if reps > 1:
        tc, ts_up, ts_dn = (jnp.concatenate([t] * reps, axis=1) for t in (tc, ts_up, ts_dn))
    return (x * tc + pltpu.roll(x, ROPE_HALF, 1) * ts_up
            + pltpu.roll(x, w - ROPE_HALF, 1) * ts_dn)


def _head_rms(x, ones_bd, gain):
    ss = _dot((x * x).astype(BF16), ones_bd)
    return x * lax.rsqrt(ss * (1.0 / HEAD_DIM) + EPS) * gain


def _dup_halves(x):
    low = lax.broadcasted_iota(jnp.int32, x.shape, 1) < HEAD_DIM
    r = pltpu.roll(x, HEAD_DIM, 1)
    return jnp.where(low, x, r), jnp.where(low, r, x)


def _silu(z):
    return z * jax.nn.sigmoid(z)


def _in_proj_kernel(x_ref, pos_ref, gn_ref, w_ref, bd512_ref, bd256_ref, gq_ref, gks_ref, gkw_ref,
                    gqm_ref, invf_ref,
                    q_ref, kc_ref, vc_ref, ks_ref, vs_ref, kw_ref, vw_ref, sig_ref, zn_ref, vp_ref,
                    zp_ref, qm_ref, zm_ref):
    x = x_ref[...]
    h = x * lax.rsqrt(jnp.mean(x * x, axis=-1, keepdims=True) + EPS) * gn_ref[...]
    hb = h.astype(BF16)

    def seg(name):
        o, w = _SEG[name]
        return _dot(hb, w_ref[:, o:o + w])

    tables = _rope_tables(pos_ref[...].astype(F32), invf_ref[...])

    q = _head_rms(seg('q'), bd512_ref[...], gq_ref[...])
    q_ref[...] = (_rope(q, tables) * SCALE).astype(BF16)

    kv = seg('kv')
    kc_ref[...] = kv[:, 0:128]
    vc_ref[...] = kv[:, 128:256]

    def norm_dup(k, gain):
        k = k * lax.rsqrt(jnp.mean(k * k, axis=-1, keepdims=True) + EPS) * gain
        return _rope(k, tables).astype(BF16)

    ks0, ks1 = _dup_halves(kv[:, 256:384])
    ks_ref[0] = norm_dup(ks0, gks_ref[...])
    ks_ref[1] = norm_dup(ks1, gks_ref[...])
    vs_t = kv[:, 384:512].T.astype(BF16)
    ones = jnp.ones((HEAD_DIM, vs_t.shape[1]), BF16)
    vs_ref[0, 0] = jnp.concatenate([vs_t[0:HEAD_DIM], ones], axis=0)
    vs_ref[1, 0] = jnp.concatenate([vs_t[HEAD_DIM:], ones], axis=0)
    kw0, kw1 = _dup_halves(kv[:, 512:640])
    kw_ref[0] = norm_dup(kw0, gkw_ref[...])
    kw_ref[1] = norm_dup(kw1, gkw_ref[...])
    vw0, vw1 = _dup_halves(kv[:, 640:768])
    vw_ref[0] = vw0.astype(BF16)
    vw_ref[1] = vw1.astype(BF16)

    sig_ref[...] = jax.nn.sigmoid(seg('gate'))
    zn_ref[...] = _silu(seg('zn'))
    vp_ref[...] = seg('vp')
    zp_ref[...] = _silu(seg('zp'))
    qm = _head_rms(seg('qm'), bd256_ref[...], gqm_ref[...])
    qm_ref[...] = (qm * SCALE).astype(BF16)
    zm_ref[...] = _silu(seg('zm'))


def _in_proj(x2, pos2, gn, w_all, bd512, bd256, gq, gks, gkw, gqm, invf, tm):
    n = x2.shape[0]
    row = lambda w: pl.BlockSpec((tm, w), lambda i: (i, 0))
    full = lambda a: pl.BlockSpec(a.shape, lambda i: (0,) * a.ndim)
    dup = pl.BlockSpec((NSA_KV_HEADS, tm, LANES), lambda i: (0, i, 0))
    sd = jax.ShapeDtypeStruct
    vst = pl.BlockSpec((NSA_KV_HEADS, 1, LANES, tm), lambda i: (0, i, 0, 0))
    out_shape = [sd((n, 512), BF16), sd((n, 128), F32), sd((n, 128), F32),
                 sd((2, n, 128), BF16), sd((2, n // tm, LANES, tm), BF16), sd((2, n, 128), BF16),
                 sd((2, n, 128), BF16), sd((n, 256), F32), sd((n, 512), F32), sd((n, 256), F32),
                 sd((n, 256), F32), sd((n, 256), BF16), sd((n, 256), F32)]
    out_specs = [row(512), row(128), row(128), dup, vst, dup, dup, row(256), row(512), row(256),
                 row(256), row(256), row(256)]
    consts = (gn, w_all, bd512, bd256, gq, gks, gkw, gqm, invf)
    return pl.pallas_call(
        _in_proj_kernel,
        grid=(n // tm,),
        in_specs=[row(D_MODEL), row(1)] + [full(a) for a in consts],
        out_specs=out_specs,
        out_shape=out_shape,
        compiler_params=_vmem_params(("parallel",)),
    )(x2, pos2, *consts)


def _compress_kernel(zk_ref, zv_ref, pend_ref, pak_ref, pbk_ref, w1ak_ref, w1bk_ref, w2k_ref,
                     pav_ref, pbv_ref, w1av_ref, w1bv_ref, w2v_ref, bd128_ref, gk_ref, invf_ref,
                     kc_ref, vc_ref):
    ncp = zk_ref.shape[1]

    def mlp(z, pa, pb, w1a, w1b, w2):
        a = _dot((z + pa).astype(BF16), w1a)
        b = _dot((z + pb).astype(BF16), w1b)
        pre = a + pltpu.roll(b, ncp - 1, 0)
        return _dot(jax.nn.gelu(pre).astype(BF16), w2)

    kc = mlp(zk_ref[0], pak_ref[...], pbk_ref[...], w1ak_ref[...], w1bk_ref[...], w2k_ref[...])
    kc = _head_rms(kc, bd128_ref[...], gk_ref[...])
    kc = _rope(kc, _rope_tables(pend_ref[0].astype(F32), invf_ref[...]))
    k0, k1 = _dup_halves(kc)
    kc_ref[0, 0] = k0.astype(BF16)
    kc_ref[0, 1] = k1.astype(BF16)
    vc = mlp(zv_ref[0], pav_ref[...], pbv_ref[...], w1av_ref[...], w1bv_ref[...], w2v_ref[...])
    v0, v1 = _dup_halves(vc)
    vc_ref[0, 0] = v0.astype(BF16)
    vc_ref[0, 1] = v1.astype(BF16)


def _compress(zk, zv, pend, consts):
    b, ncp, zw = zk.shape
    full = lambda a: pl.BlockSpec(a.shape, lambda i: (0,) * a.ndim)
    zspec = pl.BlockSpec((1, ncp, zw), lambda i: (i, 0, 0))
    ospec = pl.BlockSpec((1, NSA_KV_HEADS, ncp, LANES), lambda i: (i, 0, 0, 0))
    sd = jax.ShapeDtypeStruct((b, NSA_KV_HEADS, ncp, LANES), BF16)
    return pl.pallas_call(
        _compress_kernel,
        grid=(b,),
        in_specs=[zspec, zspec, pl.BlockSpec((1, ncp, 1), lambda i: (i, 0, 0))] + [full(a) for a in consts],
        out_specs=[ospec, ospec],
        out_shape=[sd, sd],
        compiler_params=_vmem_params(("parallel",)),
    )(zk, zv, pend, *consts)


def _head_operand(q_blk, r):
    pair = q_blk[:, (r // 2) * LANES:(r // 2 + 1) * LANES]
    low = lax.broadcasted_iota(jnp.int32, pair.shape, 1) < HEAD_DIM
    keep = low if r % 2 == 0 else jnp.logical_not(low)
    return jnp.where(keep, pair, jnp.zeros_like(pair))


def _pack_heads(outs):
    low = lax.broadcasted_iota(jnp.int32, outs[0].shape, 1) < HEAD_DIM
    return jnp.concatenate([jnp.where(low, outs[0], outs[1]), jnp.where(low, outs[2], outs[3])], axis=1)


def _gate_col(sig_blk, r, branch):
    c = branch * NSA_GROUP + r
    return sig_blk[:, c:c + 1]


def _cmp_kernel(q_ref, kc_ref, vc_ref, sig_ref, selt_ref, o_ref, bias_ref):
    tq = q_ref.shape[1]
    ncp = kc_ref.shape[2]
    t0 = pl.program_id(2) * tq
    q_blk = q_ref[0]
    kc = kc_ref[0, 0]
    vc = vc_ref[0, 0]
    sig = sig_ref[0]
    t = t0 + lax.broadcasted_iota(jnp.int32, (tq, ncp), 0)
    n = lax.broadcasted_iota(jnp.int32, (tq, ncp), 1)
    cmask = n * CMP_STRIDE + (CMP_LEN - 1) <= t
    psum = jnp.zeros((tq, ncp), F32)
    outs = []
    for r in range(NSA_GROUP):
        s = _dot_nt(_head_operand(q_blk, r), kc)
        s = jnp.where(cmask, s, -jnp.inf)
        m = jnp.max(s, axis=-1, keepdims=True)
        m = jnp.where(m == -jnp.inf, 0.0, m)
        e = jnp.where(cmask, jnp.exp(s - m), 0.0)
        p = e / jnp.maximum(jnp.sum(e, axis=-1, keepdims=True), TINY)
        psum = psum + p
        outs.append(_dot(p.astype(BF16), vc) * _gate_col(sig, r, 0))
    o_ref[0] = _pack_heads(outs)

    ps_hi = psum.astype(BF16)
    ps_lo = (psum - ps_hi.astype(F32)).astype(BF16)
    selt = selt_ref[...]
    imp = _dot_nt(selt, ps_hi) + _dot_nt(selt, ps_lo)
    j = lax.broadcasted_iota(jnp.int32, (NS_PAD, tq), 0)
    cur = (t0 + lax.broadcasted_iota(jnp.int32, (NS_PAD, tq), 1)) >> 6
    forced = (j == 0) | (j == cur) | (j == cur - 1)
    valid = j <= cur
    vals = jnp.where(forced, FORCE_SCORE, jnp.where(valid, imp, -1.0))
    chosen = jnp.zeros((NS_PAD, tq), F32)
    for _ in range(SEL_TOPK):
        m = jnp.max(vals, axis=0, keepdims=True)
        idx = jnp.min(jnp.where(vals == m, j, NS_PAD), axis=0, keepdims=True)
        hit = j == idx
        chosen = jnp.where(hit, 1.0, chosen)
        vals = jnp.where(hit, -jnp.inf, vals)
    bias_t = jnp.where((chosen > 0.0) & valid, 0.0, MASK_BIAS)
    bias_ref[0, 0] = bias_t.astype(BF16)


def _cmp_attention(q3, kc, vc, sig3, selt, tq):
    b, s, _ = q3.shape
    ncp = kc.shape[2]
    grid = (b, NSA_KV_HEADS, s // tq)
    return pl.pallas_call(
        _cmp_kernel,
        grid=grid,
        in_specs=[pl.BlockSpec((1, tq, 256), lambda bi, g, i: (bi, i, g)),
                  pl.BlockSpec((1, 1, ncp, LANES), lambda bi, g, i: (bi, g, 0, 0)),
                  pl.BlockSpec((1, 1, ncp, LANES), lambda bi, g, i: (bi, g, 0, 0)),
                  pl.BlockSpec((1, tq, LANES), lambda bi, g, i: (bi, i, g)),
                  pl.BlockSpec(selt.shape, lambda bi, g, i: (0, 0))],
        out_specs=[pl.BlockSpec((1, tq, 256), lambda bi, g, i: (bi, i, g)),
                   pl.BlockSpec((1, 1, NS_PAD, tq), lambda bi, g, i: (bi, g, 0, i))],
        out_shape=[jax.ShapeDtypeStruct((b, s, NSA_WIDTH), F32),
                   jax.ShapeDtypeStruct((b, NSA_KV_HEADS, NS_PAD, s), BF16)],
        compiler_params=_vmem_params(("parallel", "parallel", "parallel")),
    )(q3, kc, vc, sig3, selt)


def _sel_kernel(q_ref, bias_ref, et_ref, ks_ref, vs_ref, sig_ref, o_ref, qaug_ref, m_ref, acc_ref):
    tq = q_ref.shape[1]
    tk = vs_ref.shape[4]
    cols = NSA_GROUP * tq
    qt = pl.program_id(2)

    q_blk = q_ref[0].astype(F32)
    bias = bias_ref[0, 0]
    zeros = jnp.zeros((HEAD_DIM, tq), BF16)
    for rp in range(NSA_GROUP // 2):
        pair_t = q_blk[:, rp * LANES:(rp + 1) * LANES].T.astype(BF16)
        for par in range(2):
            c0 = (2 * rp + par) * tq
            qaug_ref[0:LANES, c0:c0 + tq] = bias
            qaug_ref[LANES:LANES + HEAD_DIM, c0:c0 + tq] = pair_t[par * HEAD_DIM:(par + 1) * HEAD_DIM]
            qaug_ref[LANES + HEAD_DIM:2 * LANES, c0:c0 + tq] = zeros
    m_ref[...] = jnp.full(m_ref.shape, -jnp.inf, F32)
    acc_ref[...] = jnp.zeros(acc_ref.shape, F32)

    def step(kt, causal):
        k0 = pl.multiple_of(kt * tk, tk)
        kaug = jnp.concatenate([et_ref[pl.ds(k0, tk), :], ks_ref[0, 0, pl.ds(k0, tk), :]], axis=1)
        s = _dot(kaug, qaug_ref[...])
        if causal:
            kpos = k0 + lax.broadcasted_iota(jnp.int32, (tk, cols), 0)
            t = qt * tq + (lax.broadcasted_iota(jnp.int32, (tk, cols), 1) & (tq - 1))
            s = jnp.where(kpos <= t, s, MASK_BIAS)
        m_prev = m_ref[...]
        m_new = jnp.maximum(m_prev, jnp.max(s, axis=0, keepdims=True))
        alpha = jnp.exp(m_prev - m_new)
        p = jnp.exp(s - m_new).astype(BF16)
        acc_ref[...] = alpha * acc_ref[...] + _dot(vs_ref[0, 0, kt], p)
        m_ref[...] = m_new

    last = (qt * tq) // tk
    lax.fori_loop(0, last, lambda kt, c: (step(kt, False), c)[1], 0)
    step(last, True)

    sig = sig_ref[0]
    acc = acc_ref[...]
    low = lax.broadcasted_iota(jnp.int32, (tq, LANES), 1) < HEAD_DIM
    outs = []
    for r in range(NSA_GROUP):
        a = acc[:, r * tq:(r + 1) * tq].T
        outs.append(a / pltpu.roll(a, HEAD_DIM, 1) * _gate_col(sig, r, 1))
    o_ref[0] = jnp.concatenate(
        [jnp.where(low, outs[0], pltpu.roll(outs[1], HEAD_DIM, 1)),
         jnp.where(low, outs[2], pltpu.roll(outs[3], HEAD_DIM, 1))], axis=1)


def _sel_attention(q3, bias, et, ks, vst, sig3, tq):
    b, s, _ = q3.shape
    tk = vst.shape[4]
    cols = NSA_GROUP * tq
    return pl.pallas_call(
        _sel_kernel,
        grid=(b, NSA_KV_HEADS, s // tq),
        in_specs=[pl.BlockSpec((1, tq, 256), lambda bi, g, i: (bi, i, g)),
                  pl.BlockSpec((1, 1, NS_PAD, tq), lambda bi, g, i: (bi, g, 0, i)),
                  pl.BlockSpec((s, LANES), lambda bi, g, i: (0, 0)),
                  pl.BlockSpec((1, 1, s, LANES), lambda bi, g, i: (g, bi, 0, 0)),
                  pl.BlockSpec((1, 1, s // tk, LANES, tk), lambda bi, g, i: (g, bi, 0, 0, 0)),
                  pl.BlockSpec((1, tq, LANES), lambda bi, g, i: (bi, i, g))],
        out_specs=pl.BlockSpec((1, tq, 256), lambda bi, g, i: (bi, i, g)),
        out_shape=jax.ShapeDtypeStruct((b, s, NSA_WIDTH), F32),
        scratch_shapes=[pltpu.VMEM((2 * LANES, cols), BF16), pltpu.VMEM((1, cols), F32),
                        pltpu.VMEM((LANES, cols), F32)],
        compiler_params=_vmem_params(("parallel", "parallel", "parallel")),
    )(q3, bias, et, ks, vst, sig3)


def _win_kernel(q_ref, kp_ref, kc_ref, vp_ref, vc_ref, sig_ref, o_ref):
    tq = q_ref.shape[1]
    qt = pl.program_id(2)
    q_blk = q_ref[0]
    sig = sig_ref[0]
    i = lax.broadcasted_iota(jnp.int32, (tq, tq), 0)
    c = lax.broadcasted_iota(jnp.int32, (tq, tq), 1)
    mask_cur = c <= i
    mask_prev = c > i + jnp.where(qt > 0, 0, tq)
    outs = []
    for r in range(NSA_GROUP):
        qh = _head_operand(q_blk, r)
        s_p = jnp.where(mask_prev, _dot_nt(qh, kp_ref[0, 0]), -jnp.inf)
        s_c = jnp.where(mask_cur, _dot_nt(qh, kc_ref[0, 0]), -jnp.inf)
        m = jnp.maximum(jnp.max(s_p, axis=-1, keepdims=True), jnp.max(s_c, axis=-1, keepdims=True))
        e_p = jnp.exp(s_p - m)
        e_c = jnp.exp(s_c - m)
        l = jnp.sum(e_p, axis=-1, keepdims=True) + jnp.sum(e_c, axis=-1, keepdims=True)
        o = _dot(e_p.astype(BF16), vp_ref[0, 0]) + _dot(e_c.astype(BF16), vc_ref[0, 0])
        outs.append(o / l * _gate_col(sig, r, 2))
    o_ref[0] = _pack_heads(outs)


def _win_attention(q3, kw, vw, sig3):
    b, s, _ = q3.shape
    tq = WINDOW
    prev = pl.BlockSpec((1, 1, tq, LANES), lambda bi, g, i: (g, bi, jnp.maximum(i - 1, 0), 0))
    cur = pl.BlockSpec((1, 1, tq, LANES), lambda bi, g, i: (g, bi, i, 0))
    return pl.pallas_call(
        _win_kernel,
        grid=(b, NSA_KV_HEADS, s // tq),
        in_specs=[pl.BlockSpec((1, tq, 256), lambda bi, g, i: (bi, i, g)), prev, cur, prev, cur,
                  pl.BlockSpec((1, tq, LANES), lambda bi, g, i: (bi, i, g))],
        out_specs=pl.BlockSpec((1, tq, 256), lambda bi, g, i: (bi, i, g)),
        out_shape=jax.ShapeDtypeStruct((b, s, NSA_WIDTH), F32),
        compiler_params=_vmem_params(("parallel", "parallel", "parallel")),
    )(q3, kw, kw, vw, vw, sig3)


def _mem_kv_kernel(mem_ref, gm_ref, w_ref, bd256_ref, gk_ref, mk_ref, mv_ref):
    x = mem_ref[0]
    h = x * lax.rsqrt(jnp.mean(x * x, axis=-1, keepdims=True) + EPS) * gm_ref[...]
    mkv = _dot(h.astype(BF16), w_ref[...])
    mk_ref[0] = _head_rms(mkv[:, 0:MEM_WIDTH], bd256_ref[...], gk_ref[...]).astype(BF16)
    mv_ref[0] = mkv[:, MEM_WIDTH:2 * MEM_WIDTH].astype(BF16)


def _mem_kv(mem, gm, w, bd256, gk):
    b, m, _ = mem.shape
    full = lambda a: pl.BlockSpec(a.shape, lambda i: (0,) * a.ndim)
    ospec = pl.BlockSpec((1, m, MEM_WIDTH), lambda i: (i, 0, 0))
    sd = jax.ShapeDtypeStruct((b, m, MEM_WIDTH), BF16)
    return pl.pallas_call(
        _mem_kv_kernel,
        grid=(b,),
        in_specs=[pl.BlockSpec((1, m, D_MODEL), lambda i: (i, 0, 0)), full(gm), full(w), full(bd256), full(gk)],
        out_specs=[ospec, ospec],
        out_shape=[sd, sd],
        compiler_params=_vmem_params(("parallel",)),
    )(mem, gm, w, bd256, gk)


def _mem_attn_kernel(q_ref, mk_ref, mv_ref, o_ref):
    q_blk = q_ref[0]
    mk = mk_ref[0]
    mv = mv_ref[0]
    outs = []
    for h in range(MEM_HEADS):
        hp = h // 2
        s = _dot_nt(_head_operand(q_blk, h), mk[:, hp * LANES:(hp + 1) * LANES])
        m = jnp.max(s, axis=-1, keepdims=True)
        e = jnp.exp(s - m)
        p = e / jnp.sum(e, axis=-1, keepdims=True)
        outs.append(_dot(p.astype(BF16), mv[:, hp * LANES:(hp + 1) * LANES]))
    low = lax.broadcasted_iota(jnp.int32, outs[0].shape, 1) < HEAD_DIM
    o_ref[0] = jnp.concatenate([jnp.where(low, outs[0], outs[1]), jnp.where(low, outs[2], outs[3])], axis=1)


def _mem_attention(qm3, mk, mv, tq):
    b, s, _ = qm3.shape
    m = mk.shape[1]
    kv = pl.BlockSpec((1, m, MEM_WIDTH), lambda bi, i: (bi, 0, 0))
    return pl.pallas_call(
        _mem_attn_kernel,
        grid=(b, s // tq),
        in_specs=[pl.BlockSpec((1, tq, MEM_WIDTH), lambda bi, i: (bi, i, 0)), kv, kv],
        out_specs=pl.BlockSpec((1, tq, MEM_WIDTH), lambda bi, i: (bi, i, 0)),
        out_shape=jax.ShapeDtypeStruct((b, s, MEM_WIDTH), F32),
        compiler_params=_vmem_params(("parallel", "parallel")),
    )(qm3, mk, mv)


def _out_kernel(seq_len, x_ref, oc_ref, os_ref, ow_ref, zn_ref, vp_ref, halo_ref, zp_ref, om_ref,
                zm_ref, wo_ref, wpool_ref, pscale_ref, o_ref):
    tm = x_ref.shape[0]
    t0 = (pl.program_id(0) * tm) % seq_len
    y_nsa = (oc_ref[...] + os_ref[...] + ow_ref[...]) * zn_ref[...]

    v = vp_ref[...]
    halo = halo_ref[...] * (t0 > 0).astype(F32)
    e = jnp.concatenate([halo, v], axis=0)
    w2 = e + pltpu.roll(e, 1, 0)
    w4 = w2 + pltpu.roll(w2, 2, 0)
    w8 = w4 + pltpu.roll(w4, 4, 0)
    w16 = w8 + pltpu.roll(w8, 8, 0)
    shape = (tm, POOL_WIDTH)
    grp = lax.broadcasted_iota(jnp.int32, shape, 1) >> 6
    wsum = jnp.where(grp == 0, w2[POOL_HALO:],
                     jnp.where(grp == 1, w4[POOL_HALO:], jnp.where(grp == 2, w8[POOL_HALO:], w16[POOL_HALO:])))
    width = jnp.left_shift(2, grp)
    cnt = jnp.minimum(t0 + lax.broadcasted_iota(jnp.int32, shape, 0) + 1, width).astype(F32)
    pooled = wsum / cnt - v
    y_pool = _dot(pooled.astype(BF16), wpool_ref[...]) * pscale_ref[...] * zp_ref[...]

    y_mem = om_ref[...] * zm_ref[...]
    out = x_ref[...] + _dot(y_nsa.astype(BF16), wo_ref[0:512, :])
    out = out + _dot(y_pool.astype(BF16), wo_ref[512:768, :])
    out = out + _dot(y_mem.astype(BF16), wo_ref[768:1024, :])
    o_ref[...] = out


def _out_proj(x2, oc, os_, ow, zn, vp, zp, om, zm, wo, wpool, pscale, seq_len, tm):
    n = x2.shape[0]
    row = lambda w: pl.BlockSpec((tm, w), lambda i: (i, 0))
    full = lambda a: pl.BlockSpec(a.shape, lambda i: (0,) * a.ndim)
    hb = tm // POOL_HALO
    halo = pl.BlockSpec((POOL_HALO, POOL_WIDTH), lambda i: (jnp.maximum(i * hb - 1, 0), 0))
    return pl.pallas_call(
        functools.partial(_out_kernel, seq_len),
        grid=(n // tm,),
        in_specs=[row(D_MODEL), row(512), row(512), row(512), row(512), row(256), halo, row(256),
                  row(256), row(256), full(wo), full(wpool), full(pscale)],
        out_specs=row(D_MODEL),
        out_shape=jax.ShapeDtypeStruct((n, D_MODEL), F32),
        compiler_params=_vmem_params(("parallel",)),
    )(x2, oc, os_, ow, zn, vp, vp, zp, om, zm, wo, wpool, pscale)


def _block_diag_ones(width):
    i = np.arange(width) // HEAD_DIM
    return jnp.asarray((i[:, None] == i[None, :]).astype(np.float32), dtype=BF16)


def _pack_w_in(w_in):
    q, kv, gate, zn, vp, zp, qm, zm = jnp.split(
        w_in, np.cumsum([512, 768, 24, 512, 256, 256, 256, 256])[:-1].tolist(), axis=1)
    cols = np.zeros((NSA_KV_HEADS, LANES), np.int64)
    used = np.zeros((NSA_KV_HEADS, LANES), bool)
    for g in range(NSA_KV_HEADS):
        for k in range(N_BRANCH):
            for r in range(NSA_GROUP):
                cols[g, k * NSA_GROUP + r] = (g * NSA_GROUP + r) * N_BRANCH + k
                used[g, k * NSA_GROUP + r] = True
    gate_p = jnp.where(jnp.asarray(used.reshape(-1))[None, :], gate[:, cols.reshape(-1)], 0.0)
    return jnp.concatenate([q, kv, gate_p, zn, vp, zp, qm, zm], axis=1).astype(BF16)


def _pack_cmp(pos, w1, w2):
    half = CMP_LEN // 2

    def w1_half(wh):
        z = jnp.zeros((half, NSA_KV_HEADS, HEAD_DIM, NSA_KV_HEADS, HEAD_DIM), F32)
        for g in range(NSA_KV_HEADS):
            z = z.at[:, g, :, g, :].set(wh)
        return z.reshape(half * KV_WIDTH, KV_WIDTH).astype(BF16)

    def pos_half(ph):
        return jnp.tile(ph[:, None, :], (1, NSA_KV_HEADS, 1)).reshape(1, half * KV_WIDTH)

    w2_bd = jnp.zeros((NSA_KV_HEADS, HEAD_DIM, NSA_KV_HEADS, HEAD_DIM), F32)
    for g in range(NSA_KV_HEADS):
        w2_bd = w2_bd.at[g, :, g, :].set(w2)
    return (pos_half(pos[:half]), pos_half(pos[half:]), w1_half(w1[:half]), w1_half(w1[half:]),
            w2_bd.reshape(KV_WIDTH, KV_WIDTH).astype(BF16))


def _sel_map_t(ncp, s):
    nc = (s - CMP_LEN) // CMP_STRIDE + 1
    ns = s // SEL_BLOCK
    c0 = np.arange(nc) * CMP_STRIDE
    c1 = c0 + CMP_LEN
    s0 = np.arange(ns) * SEL_BLOCK
    s1 = s0 + SEL_BLOCK
    ov = np.clip(np.minimum(c1[:, None], s1[None, :]) - np.maximum(c0[:, None], s0[None, :]), 0, None)
    m = np.zeros((NS_PAD, ncp), np.float32)
    m[:ns, :nc] = (ov / CMP_LEN).T
    return jnp.asarray(m, dtype=BF16)


def kernel(x, mem, positions, g_norm, w_in, g_q_nsa, g_k_cmp, g_k_slc, g_k_win, cmp_pos_k, w_cmp_k1,
           w_cmp_k2, cmp_pos_v, w_cmp_v1, w_cmp_v2, w_pool, pool_scale, g_mem, w_mem_kv, g_q_mem,
           g_k_mem, w_out):
    b, s, d = x.shape
    depth = g_norm.shape[0]
    assert d == D_MODEL and s % WINDOW == 0 and s // SEL_BLOCK >= SEL_TOPK and s // SEL_BLOCK <= NS_PAD
    n = b * s
    ncp = s // CMP_STRIDE
    tm = 512
    t_cmp = 256
    t_sel = 256
    t_mem = 512
    assert s % tm == 0 and tm % t_sel == 0

    inv_freq = ROPE_THETA ** (-jnp.arange(ROPE_HALF, dtype=F32) / ROPE_HALF)
    invf = inv_freq[(np.arange(LANES) % HEAD_DIM) % ROPE_HALF][None, :]
    bd512, bd256, bd128 = _block_diag_ones(512), _block_diag_ones(256), _block_diag_ones(128)
    pos2 = positions.reshape(n, 1)
    end_idx = np.minimum(np.arange(ncp) * CMP_STRIDE + CMP_LEN - 1, s - 1)
    pend = positions[:, end_idx][:, :, None]
    selt = _sel_map_t(ncp, s)
    et = jnp.asarray((np.arange(s)[:, None] // SEL_BLOCK == np.arange(NS_PAD)[None, :]).astype(np.float32),
                     dtype=BF16)
    tile = lambda g, reps: jnp.tile(g[None, :], (1, reps))

    for l in range(depth):
        x2 = x.reshape(n, d)
        w_all = _pack_w_in(w_in[l])
        (q, kc_raw, vc_raw, ks, vs, kw, vw, sig, zn, vp, zp, qm, zm) = _in_proj(
            x2, pos2, g_norm[l][None, :], w_all, bd512, bd256, tile(g_q_nsa[l], 8), tile(g_k_slc[l], 2),
            tile(g_k_win[l], 2), tile(g_q_mem[l], 4), invf, tm)

        cmp_consts = (_pack_cmp(cmp_pos_k[l], w_cmp_k1[l], w_cmp_k2[l])
                      + _pack_cmp(cmp_pos_v[l], w_cmp_v1[l], w_cmp_v2[l])
                      + (bd128, tile(g_k_cmp[l], 2), invf))
        kc, vc = _compress(kc_raw.reshape(b, ncp, CMP_STRIDE * KV_WIDTH),
                           vc_raw.reshape(b, ncp, CMP_STRIDE * KV_WIDTH), pend, cmp_consts)

        q3 = q.reshape(b, s, NSA_WIDTH)
        sig3 = sig.reshape(b, s, 256)
        o_c, bias = _cmp_attention(q3, kc, vc, sig3, selt, t_cmp)
        dup4 = lambda a: a.reshape(NSA_KV_HEADS, b, s, LANES)
        o_s = _sel_attention(q3, bias, et, dup4(ks), vs.reshape(NSA_KV_HEADS, b, s // tm, LANES, tm),
                             sig3, t_sel)
        o_w = _win_attention(q3, dup4(kw), dup4(vw), sig3)

        mk, mv = _mem_kv(mem, g_mem[l][None, :], w_mem_kv[l].astype(BF16), bd256, tile(g_k_mem[l], 4))
        o_m = _mem_attention(qm.reshape(b, s, MEM_WIDTH), mk, mv, t_mem)

        wpool = jnp.zeros((4, HEAD_DIM, 4, HEAD_DIM), F32)
        for g in range(4):
            wpool = wpool.at[g, :, g, :].set(w_pool[l, g])
        out = _out_proj(x2, o_c.reshape(n, 512), o_s.reshape(n, 512), o_w.reshape(n, 512), zn, vp, zp,
                        o_m.reshape(n, MEM_WIDTH), zm, w_out[l].astype(BF16),
                        wpool.reshape(POOL_WIDTH, POOL_WIDTH).astype(BF16), pool_scale[l][None, :], s, tm)
        x = out.reshape(b, s, d)
    return x
```

```python
import functools

import numpy as np
import jax
import jax.numpy as jnp
from jax import lax
from jax.experimental import pallas as pl
from jax.experimental.pallas import tpu as pltpu

F32 = jnp.float32
BF16 = jnp.bfloat16

D_MODEL = 1024
HEAD_DIM = 64
NSA_HEADS = 8
NSA_KV_HEADS = 2
NSA_GROUP = NSA_HEADS // NSA_KV_HEADS
NSA_WIDTH = NSA_HEADS * HEAD_DIM
KV_WIDTH = NSA_KV_HEADS * HEAD_DIM
N_BRANCH = 3
CMP_LEN = 32
CMP_STRIDE = 16
SEL_BLOCK = 64
SEL_TOPK = 16
WINDOW = 512
FORCE_SCORE = 1e4
N_FORCED = 3
POOL_WINDOWS = (2, 4, 8, 16)
POOL_WIDTH = 256
MEM_HEADS = 4
MEM_WIDTH = MEM_HEADS * HEAD_DIM
ROPE_THETA = 500000.0
ROPE_DIM = HEAD_DIM // 4
ROPE_HALF = ROPE_DIM // 2
EPS = 1e-6
SCALE = HEAD_DIM ** -0.5
Q_SCALE = SCALE * float(np.log2(np.e))

LANES = 128
NS_PAD = LANES
MASK_BIAS = -1e9
POOL_HALO = 16
TINY = float(np.finfo(np.float32).tiny)

_SEG = dict(q=(0, 512), kv=(512, 768), gate=(1280, 256), zn=(1536, 512), vp=(2048, 256),
            zp=(2304, 256), qm=(2560, 256), zm=(2816, 256))
W_ALL_WIDTH = 3072

_NT = (((1,), (1,)), ((), ()))


def _dot(a, b):
    return jnp.dot(a, b, preferred_element_type=F32)


def _dot_nt(a, b):
    return lax.dot_general(a, b, _NT, preferred_element_type=F32)


def _vmem_params(sem):
    return pltpu.CompilerParams(dimension_semantics=sem, vmem_limit_bytes=56 * 1024 * 1024)


def _rope_tables(pos_f32, invf):
    ang = pos_f32 * invf
    c, s = jnp.cos(ang), jnp.sin(ang)
    d = lax.broadcasted_iota(jnp.int32, ang.shape, 1) % HEAD_DIM
    tc = jnp.where(d < ROPE_DIM, c, 1.0)
    ts_up = jnp.where((d >= ROPE_HALF) & (d < ROPE_DIM), s, 0.0)
    ts_dn = jnp.where(d < ROPE_HALF, -s, 0.0)
    return tc, ts_up, ts_dn


def _rope(x, tables):
    tc, ts_up, ts_dn = tables
    w = x.shape[1]
    reps = w // LANES
    if reps > 1:
        tc, ts_up, ts_dn = (jnp.concatenate([t] * reps, axis=1) for t in (tc, ts_up, ts_dn))
    return (x * tc + pltpu.roll(x, ROPE_HALF, 1) * ts_up
            + pltpu.roll(x, w - ROPE_HALF, 1) * ts_dn)


def _head_rms(x, ones_bd, gain):
    ss = _dot((x * x).astype(BF16), ones_bd)
    return x * lax.rsqrt(ss * (1.0 / HEAD_DIM) + EPS) * gain


def _dup_halves(x):
    low = lax.broadcasted_iota(jnp.int32, x.shape, 1) < HEAD_DIM
    r = pltpu.roll(x, HEAD_DIM, 1)
    return jnp.where(low, x, r), jnp.where(low, r, x)


def _values_t(v):
    return _values_rows(v.T)


def _values_rows(vt):
    vt = vt.astype(BF16)
    ones = jnp.ones((HEAD_DIM, vt.shape[1]), BF16)
    return (jnp.concatenate([vt[0:HEAD_DIM], ones], axis=0),
            jnp.concatenate([vt[HEAD_DIM:], ones], axis=0))


def _silu(z):
    return z * jax.nn.sigmoid(z)


def _in_proj_kernel(x_ref, pos_ref, gn_ref, w_ref, bd512_ref, bd256_ref, gq_ref, gks_ref, gkw_ref,
                    gqm_ref, invf_ref,
                    q_ref, kc_ref, vc_ref, ks_ref, vs_ref, kw_ref, vw_ref, sig_ref, zn_ref, vp_ref,
                    zp_ref, qm_ref, zm_ref):
    x = x_ref[...]
    h = x * lax.rsqrt(jnp.mean(x * x, axis=-1, keepdims=True) + EPS) * gn_ref[...]
    hb = h.astype(BF16)

    def seg(name):
        o, w = _SEG[name]
        return _dot(hb, w_ref[:, o:o + w])

    tables = _rope_tables(pos_ref[...].astype(F32), invf_ref[...])

    q = _head_rms(seg('q'), bd512_ref[...], gq_ref[...])
    q_ref[...] = (_rope(q, tables) * Q_SCALE).astype(BF16)

    kv = seg('kv')
    kc_ref[...] = kv[:, 0:128]
    vc_ref[...] = kv[:, 128:256]

    def norm_dup(k, gain):
        k = k * lax.rsqrt(jnp.mean(k * k, axis=-1, keepdims=True) + EPS) * gain
        return _rope(k, tables).astype(BF16)

    ks0, ks1 = _dup_halves(kv[:, 256:384])
    ks_ref[0] = norm_dup(ks0, gks_ref[...])
    ks_ref[1] = norm_dup(ks1, gks_ref[...])
    vs_ref[0, 0], vs_ref[1, 0] = _values_t(kv[:, 384:512])
    kw0, kw1 = _dup_halves(kv[:, 512:640])
    kw_ref[0] = norm_dup(kw0, gkw_ref[...])
    kw_ref[1] = norm_dup(kw1, gkw_ref[...])
    vw = _values_t(kv[:, 640:768])
    wt = vw_ref.shape[3]
    for g in range(NSA_KV_HEADS):
        for c in range(vw_ref.shape[1]):
            vw_ref[g, c] = vw[g][:, c * wt:(c + 1) * wt]

    sig_ref[...] = jax.nn.sigmoid(seg('gate'))
    zn_ref[...] = _silu(seg('zn'))
    vp_ref[...] = seg('vp')
    zp_ref[...] = _silu(seg('zp'))
    qm = _head_rms(seg('qm'), bd256_ref[...], gqm_ref[...])
    qm_ref[...] = (qm * Q_SCALE).astype(BF16)
    zm_ref[...] = _silu(seg('zm'))


def _in_proj(x2, pos2, gn, w_all, bd512, bd256, gq, gks, gkw, gqm, invf, tm, t_win):
    n = x2.shape[0]
    row = lambda w: pl.BlockSpec((tm, w), lambda i: (i, 0))
    full = lambda a: pl.BlockSpec(a.shape, lambda i: (0,) * a.ndim)
    dup = pl.BlockSpec((NSA_KV_HEADS, tm, LANES), lambda i: (0, i, 0))
    sd = jax.ShapeDtypeStruct
    vst = pl.BlockSpec((NSA_KV_HEADS, 1, LANES, tm), lambda i: (0, i, 0, 0))
    vwt = pl.BlockSpec((NSA_KV_HEADS, tm // t_win, LANES, t_win), lambda i: (0, i, 0, 0))
    out_shape = [sd((n, 512), BF16), sd((n, 128), F32), sd((n, 128), F32),
                 sd((2, n, 128), BF16), sd((2, n // tm, LANES, tm), BF16), sd((2, n, 128), BF16),
                 sd((2, n // t_win, LANES, t_win), BF16), sd((n, 256), F32), sd((n, 512), F32), sd((n, 256), F32),
                 sd((n, 256), F32), sd((n, 256), BF16), sd((n, 256), F32)]
    out_specs = [row(512), row(128), row(128), dup, vst, dup, vwt, row(256), row(512), row(256),
                 row(256), row(256), row(256)]
    consts = (gn, w_all, bd512, bd256, gq, gks, gkw, gqm, invf)
    return pl.pallas_call(
        _in_proj_kernel,
        grid=(n // tm,),
        in_specs=[row(D_MODEL), row(1)] + [full(a) for a in consts],
        out_specs=out_specs,
        out_shape=out_shape,
        compiler_params=_vmem_params(("parallel",)),
    )(x2, pos2, *consts)


def _compress_kernel(zk_ref, zv_ref, pend_ref, pak_ref, pbk_ref, w1ak_ref, w1bk_ref, w2k_ref,
                     pav_ref, pbv_ref, w1av_ref, w1bv_ref, w2v_ref, bd128_ref, gk_ref, invf_ref,
                     kc_ref, vc_ref):
    ncp = zk_ref.shape[1]

    def mlp(z, pa, pb, w1a, w1b, w2, transposed=False):
        a = _dot((z + pa).astype(BF16), w1a)
        b = _dot((z + pb).astype(BF16), w1b)
        pre = a + pltpu.roll(b, ncp - 1, 0)
        hid = jax.nn.gelu(pre).astype(BF16)
        return _dot_nt(w2, hid) if transposed else _dot(hid, w2)

    kc = mlp(zk_ref[0], pak_ref[...], pbk_ref[...], w1ak_ref[...], w1bk_ref[...], w2k_ref[...])
    kc = _head_rms(kc, bd128_ref[...], gk_ref[...])
    kc = _rope(kc, _rope_tables(pend_ref[0].astype(F32), invf_ref[...]))
    k0, k1 = _dup_halves(kc)
    kc_ref[0, 0] = k0.astype(BF16)
    kc_ref[0, 1] = k1.astype(BF16)
    vc_t = mlp(zv_ref[0], pav_ref[...], pbv_ref[...], w1av_ref[...], w1bv_ref[...], w2v_ref[...], True)
    vc_ref[0, 0], vc_ref[0, 1] = _values_rows(vc_t)


def _compress(zk, zv, pend, consts):
    b, ncp, zw = zk.shape
    full = lambda a: pl.BlockSpec(a.shape, lambda i: (0,) * a.ndim)
    zspec = pl.BlockSpec((1, ncp, zw), lambda i: (i, 0, 0))
    ospec = pl.BlockSpec((1, NSA_KV_HEADS, ncp, LANES), lambda i: (i, 0, 0, 0))
    sd = jax.ShapeDtypeStruct((b, NSA_KV_HEADS, ncp, LANES), BF16)
    return pl.pallas_call(
        _compress_kernel,
        grid=(b,),
        in_specs=[zspec, zspec, pl.BlockSpec((1, ncp, 1), lambda i: (i, 0, 0))] + [full(a) for a in consts],
        out_specs=[ospec, pl.BlockSpec((1, NSA_KV_HEADS, LANES, ncp), lambda i: (i, 0, 0, 0))],
        out_shape=[sd, jax.ShapeDtypeStruct((b, NSA_KV_HEADS, LANES, ncp), BF16)],
        compiler_params=_vmem_params(("parallel",)),
    )(zk, zv, pend, *consts)


def _head_operand(q_blk, r):
    pair = q_blk[:, (r // 2) * LANES:(r // 2 + 1) * LANES]
    low = lax.broadcasted_iota(jnp.int32, pair.shape, 1) < HEAD_DIM
    keep = low if r % 2 == 0 else jnp.logical_not(low)
    return jnp.where(keep, pair, jnp.zeros_like(pair))


def _gate_col(sig_blk, r, branch):
    c = branch * NSA_GROUP + r
    return sig_blk[:, c:c + 1]


def _heads_t(q_blk):
    out = []
    for rp in range(NSA_GROUP // 2):
        pair_t = q_blk[:, rp * LANES:(rp + 1) * LANES].astype(F32).T
        low = lax.broadcasted_iota(jnp.int32, pair_t.shape, 0) < HEAD_DIM
        out.append(jnp.where(low, pair_t, 0.0).astype(BF16))
        out.append(jnp.where(low, 0.0, pair_t).astype(BF16))
    return out


def _finish_heads(accs, sig, branch):
    outs = []
    for r, acc in enumerate(accs):
        a = acc.T
        den = jnp.maximum(pltpu.roll(a, HEAD_DIM, 1), TINY)
        outs.append(a / den * _gate_col(sig, r, branch))
    low = lax.broadcasted_iota(jnp.int32, outs[0].shape, 1) < HEAD_DIM
    return jnp.concatenate(
        [jnp.where(low, outs[0], pltpu.roll(outs[1], HEAD_DIM, 1)),
         jnp.where(low, outs[2], pltpu.roll(outs[3], HEAD_DIM, 1))], axis=1)


def _cmp_kernel(q_ref, kc_ref, vc_ref, sig_ref, selt_ref, o_ref, bias_ref):
    tq = q_ref.shape[1]
    ncp = kc_ref.shape[2]
    t0 = pl.program_id(2) * tq
    kc = kc_ref[0, 0]
    vc_t = vc_ref[0, 0]
    t = t0 + lax.broadcasted_iota(jnp.int32, (ncp, tq), 1)
    n = lax.broadcasted_iota(jnp.int32, (ncp, tq), 0)
    cmask = n * CMP_STRIDE + (CMP_LEN - 1) <= t
    psum = jnp.zeros((ncp, tq), F32)
    accs = []
    for q_t in _heads_t(q_ref[0]):
        s = jnp.where(cmask, _dot(kc, q_t), -jnp.inf)
        m = jnp.max(s, axis=0, keepdims=True)
        m = jnp.where(m == -jnp.inf, 0.0, m)
        e = jnp.exp2(s - m)
        psum = psum + e * (1.0 / jnp.maximum(jnp.sum(e, axis=0, keepdims=True), TINY))
        accs.append(_dot(vc_t, e.astype(BF16)))
    o_ref[0] = _finish_heads(accs, sig_ref[0], 0)

    ps_hi = psum.astype(BF16)
    ps_lo = (psum - ps_hi.astype(F32)).astype(BF16)
    selt = selt_ref[...]
    imp = _dot(selt, ps_hi) + _dot(selt, ps_lo)
    j = lax.broadcasted_iota(jnp.int32, (NS_PAD, tq), 0)
    cur = (t0 + lax.broadcasted_iota(jnp.int32, (NS_PAD, tq), 1)) >> 6
    forced = (j == 0) | (j == cur) | (j == cur - 1)
    valid = j <= cur
    vals = jnp.where(forced, -jnp.inf, jnp.where(valid, imp, -1.0))
    neg_j = -j.astype(F32)
    for _ in range(SEL_TOPK - N_FORCED):
        m = jnp.max(vals, axis=0, keepdims=True)
        first = jnp.max(jnp.where(vals == m, neg_j, -jnp.inf), axis=0, keepdims=True)
        vals = jnp.where(neg_j == first, -jnp.inf, vals)
    bias_t = jnp.where((vals == -jnp.inf) & valid, 0.0, MASK_BIAS)
    bias_ref[0, 0] = bias_t.astype(BF16)


def _cmp_attention(q3, kc, vc, sig3, selt, tq):
    b, s, _ = q3.shape
    ncp = kc.shape[2]
    grid = (b, NSA_KV_HEADS, s // tq)
    return pl.pallas_call(
        _cmp_kernel,
        grid=grid,
        in_specs=[pl.BlockSpec((1, tq, 256), lambda bi, g, i: (bi, i, g)),
                  pl.BlockSpec((1, 1, ncp, LANES), lambda bi, g, i: (bi, g, 0, 0)),
                  pl.BlockSpec((1, 1, LANES, ncp), lambda bi, g, i: (bi, g, 0, 0)),
                  pl.BlockSpec((1, tq, LANES), lambda bi, g, i: (bi, i, g)),
                  pl.BlockSpec(selt.shape, lambda bi, g, i: (0, 0))],
        out_specs=[pl.BlockSpec((1, tq, 256), lambda bi, g, i: (bi, i, g)),
                   pl.BlockSpec((1, 1, NS_PAD, tq), lambda bi, g, i: (bi, g, 0, i))],
        out_shape=[jax.ShapeDtypeStruct((b, s, NSA_WIDTH), F32),
                   jax.ShapeDtypeStruct((b, NSA_KV_HEADS, NS_PAD, s), BF16)],
        compiler_params=_vmem_params(("parallel", "parallel", "parallel")),
    )(q3, kc, vc, sig3, selt)


def _sel_kernel(q_ref, bias_ref, et_ref, ks_ref, vs_ref, sig_ref, o_ref, qaug_ref, s_ref, m_ref,
                acc_ref):
    tq = q_ref.shape[1]
    tk = vs_ref.shape[4]
    cols = NSA_GROUP * tq
    qt = pl.program_id(2)

    bias = bias_ref[0, 0]
    for r, q_t in enumerate(_heads_t(q_ref[0])):
        qaug_ref[0:LANES, r * tq:(r + 1) * tq] = bias
        qaug_ref[LANES:2 * LANES, r * tq:(r + 1) * tq] = q_t
    m_ref[...] = jnp.full(m_ref.shape, -jnp.inf, F32)
    acc_ref[...] = jnp.zeros(acc_ref.shape, F32)
    heads = [slice(r * tq, (r + 1) * tq) for r in range(NSA_GROUP)]

    def keys(kt):
        k0 = pl.multiple_of(kt * tk, tk)
        return jnp.concatenate([et_ref[pl.ds(k0, tk), :], ks_ref[0, 0, pl.ds(k0, tk), :]], axis=1)

    def softmax_pv(kt, h, s):
        m_prev = m_ref[:, h]
        m_new = jnp.maximum(m_prev, jnp.max(s, axis=0, keepdims=True))
        p = jnp.exp2(s - m_new).astype(BF16)
        acc_ref[:, h] = jnp.exp2(m_prev - m_new) * acc_ref[:, h] + _dot(vs_ref[0, 0, kt], p)
        m_ref[:, h] = m_new

    k_first = keys(0)
    for h in heads:
        s_ref[:, h] = _dot(k_first, qaug_ref[:, h])

    def body(kt, carry):
        k_next = keys(kt + 1)
        for h in heads:
            s = s_ref[:, h]
            s_next = _dot(k_next, qaug_ref[:, h])
            softmax_pv(kt, h, s)
            s_ref[:, h] = s_next
        return carry

    last = (qt * tq) // tk
    lax.fori_loop(0, last, body, 0)
    kpos = last * tk + lax.broadcasted_iota(jnp.int32, (tk, tq), 0)
    causal = kpos <= qt * tq + lax.broadcasted_iota(jnp.int32, (tk, tq), 1)
    for h in heads:
        softmax_pv(last, h, jnp.where(causal, s_ref[:, h], MASK_BIAS))

    acc = acc_ref[...]
    o_ref[0] = _finish_heads([acc[:, h] for h in heads], sig_ref[0], 1)


def _sel_attention(q3, bias, et, ks, vst, sig3, tq):
    b, s, _ = q3.shape
    tk = vst.shape[4]
    cols = NSA_GROUP * tq
    return pl.pallas_call(
        _sel_kernel,
        grid=(b, NSA_KV_HEADS, s // tq),
        in_specs=[pl.BlockSpec((1, tq, 256), lambda bi, g, i: (bi, i, g)),
                  pl.BlockSpec((1, 1, NS_PAD, tq), lambda bi, g, i: (bi, g, 0, i)),
                  pl.BlockSpec((s, LANES), lambda bi, g, i: (0, 0)),
                  pl.BlockSpec((1, 1, s, LANES), lambda bi, g, i: (g, bi, 0, 0)),
                  pl.BlockSpec((1, 1, s // tk, LANES, tk), lambda bi, g, i: (g, bi, 0, 0, 0)),
                  pl.BlockSpec((1, tq, LANES), lambda bi, g, i: (bi, i, g))],
        out_specs=pl.BlockSpec((1, tq, 256), lambda bi, g, i: (bi, i, g)),
        out_shape=jax.ShapeDtypeStruct((b, s, NSA_WIDTH), F32),
        scratch_shapes=[pltpu.VMEM((2 * LANES, cols), BF16), pltpu.VMEM((tk, cols), F32),
                        pltpu.VMEM((1, cols), F32), pltpu.VMEM((LANES, cols), F32)],
        compiler_params=_vmem_params(("parallel", "parallel", "parallel")),
    )(q3, bias, et, ks, vst, sig3)


def _win_kernel(q_ref, kp_ref, kc_ref, vp_ref, vc_ref, sig_ref, o_ref):
    tq = q_ref.shape[1]
    qt = pl.program_id(2)
    c = lax.broadcasted_iota(jnp.int32, (tq, tq), 0)
    i = lax.broadcasted_iota(jnp.int32, (tq, tq), 1)
    mask_cur = c <= i
    mask_prev = c > i + jnp.where(qt > 0, 0, tq)
    accs = []
    for q_t in _heads_t(q_ref[0]):
        s_p = jnp.where(mask_prev, _dot(kp_ref[0, 0], q_t), -jnp.inf)
        s_c = jnp.where(mask_cur, _dot(kc_ref[0, 0], q_t), -jnp.inf)
        m = jnp.maximum(jnp.max(s_p, axis=0, keepdims=True), jnp.max(s_c, axis=0, keepdims=True))
        e_p = jnp.exp2(s_p - m).astype(BF16)
        e_c = jnp.exp2(s_c - m).astype(BF16)
        accs.append(_dot(vp_ref[0, 0], e_p) + _dot(vc_ref[0, 0], e_c))
    o_ref[0] = _finish_heads(accs, sig_ref[0], 2)


def _win_attention(q3, kw, vw, sig3, tq):
    b, s, _ = q3.shape
    assert tq == WINDOW
    nt = s // tq
    kspec = lambda back: pl.BlockSpec((1, 1, tq, LANES),
                                      lambda bi, g, i: (g, bi, jnp.maximum(i - back, 0), 0))
    vspec = lambda back: pl.BlockSpec((1, 1, LANES, tq),
                                      lambda bi, g, i: (g, bi * nt + jnp.maximum(i - back, 0), 0, 0))
    return pl.pallas_call(
        _win_kernel,
        grid=(b, NSA_KV_HEADS, nt),
        in_specs=[pl.BlockSpec((1, tq, 256), lambda bi, g, i: (bi, i, g)), kspec(1), kspec(0),
                  vspec(1), vspec(0), pl.BlockSpec((1, tq, LANES), lambda bi, g, i: (bi, i, g))],
        out_specs=pl.BlockSpec((1, tq, 256), lambda bi, g, i: (bi, i, g)),
        out_shape=jax.ShapeDtypeStruct((b, s, NSA_WIDTH), F32),
        compiler_params=_vmem_params(("parallel", "parallel", "parallel")),
    )(q3, kw, kw, vw, vw, sig3)


def _mem_kv_kernel(mem_ref, gm_ref, w_ref, bd256_ref, gk_ref, mk_ref, mv_ref):
    x = mem_ref[0]
    h = x * lax.rsqrt(jnp.mean(x * x, axis=-1, keepdims=True) + EPS) * gm_ref[...]
    mkv = _dot(h.astype(BF16), w_ref[...])
    mk_ref[0] = _head_rms(mkv[:, 0:MEM_WIDTH], bd256_ref[...], gk_ref[...]).astype(BF16)
    mv_ref[0] = mkv[:, MEM_WIDTH:2 * MEM_WIDTH].astype(BF16)


def _mem_kv(mem, gm, w, bd256, gk):
    b, m, _ = mem.shape
    full = lambda a: pl.BlockSpec(a.shape, lambda i: (0,) * a.ndim)
    ospec = pl.BlockSpec((1, m, MEM_WIDTH), lambda i: (i, 0, 0))
    sd = jax.ShapeDtypeStruct((b, m, MEM_WIDTH), BF16)
    return pl.pallas_call(
        _mem_kv_kernel,
        grid=(b,),
        in_specs=[pl.BlockSpec((1, m, D_MODEL), lambda i: (i, 0, 0)), full(gm), full(w), full(bd256), full(gk)],
        out_specs=[ospec, ospec],
        out_shape=[sd, sd],
        compiler_params=_vmem_params(("parallel",)),
    )(mem, gm, w, bd256, gk)


def _mem_attn_kernel(q_ref, mk_ref, mv_ref, o_ref):
    q_blk = q_ref[0]
    mk = mk_ref[0]
    mv = mv_ref[0]
    outs = []
    for h in range(MEM_HEADS):
        hp = h // 2
        s = _dot_nt(_head_operand(q_blk, h), mk[:, hp * LANES:(hp + 1) * LANES])
        m = jnp.max(s, axis=-1, keepdims=True)
        e = jnp.exp2(s - m)
        p = e / jnp.sum(e, axis=-1, keepdims=True)
        outs.append(_dot(p.astype(BF16), mv[:, hp * LANES:(hp + 1) * LANES]))
    low = lax.broadcasted_iota(jnp.int32, outs[0].shape, 1) < HEAD_DIM
    o_ref[0] = jnp.concatenate([jnp.where(low, outs[0], outs[1]), jnp.where(low, outs[2], outs[3])], axis=1)


def _mem_attention(qm3, mk, mv, tq):
    b, s, _ = qm3.shape
    m = mk.shape[1]
    kv = pl.BlockSpec((1, m, MEM_WIDTH), lambda bi, i: (bi, 0, 0))
    return pl.pallas_call(
        _mem_attn_kernel,
        grid=(b, s // tq),
        in_specs=[pl.BlockSpec((1, tq, MEM_WIDTH), lambda bi, i: (bi, i, 0)), kv, kv],
        out_specs=pl.BlockSpec((1, tq, MEM_WIDTH), lambda bi, i: (bi, i, 0)),
        out_shape=jax.ShapeDtypeStruct((b, s, MEM_WIDTH), F32),
        compiler_params=_vmem_params(("parallel", "parallel")),
    )(qm3, mk, mv)


def _out_kernel(seq_len, x_ref, oc_ref, os_ref, ow_ref, zn_ref, vp_ref, halo_ref, zp_ref, om_ref,
                zm_ref, wo_ref, wpool_ref, pscale_ref, o_ref):
    tm = x_ref.shape[0]
    t0 = (pl.program_id(0) * tm) % seq_len
    y_nsa = (oc_ref[...] + os_ref[...] + ow_ref[...]) * zn_ref[...]

    v = vp_ref[...]
    halo = halo_ref[...] * (t0 > 0).astype(F32)
    e = jnp.concatenate([halo, v], axis=0)
    w2 = e + pltpu.roll(e, 1, 0)
    w4 = w2 + pltpu.roll(w2, 2, 0)
    w8 = w4 + pltpu.roll(w4, 4, 0)
    w16 = w8 + pltpu.roll(w8, 8, 0)
    shape = (tm, POOL_WIDTH)
    grp = lax.broadcasted_iota(jnp.int32, shape, 1) >> 6
    wsum = jnp.where(grp == 0, w2[POOL_HALO:],
                     jnp.where(grp == 1, w4[POOL_HALO:], jnp.where(grp == 2, w8[POOL_HALO:], w16[POOL_HALO:])))
    width = jnp.left_shift(2, grp)
    cnt = jnp.minimum(t0 + lax.broadcasted_iota(jnp.int32, shape, 0) + 1, width).astype(F32)
    pooled = wsum / cnt - v
    y_pool = _dot(pooled.astype(BF16), wpool_ref[...]) * pscale_ref[...] * zp_ref[...]

    y_mem = om_ref[...] * zm_ref[...]
    out = x_ref[...] + _dot(y_nsa.astype(BF16), wo_ref[0:512, :])
    out = out + _dot(y_pool.astype(BF16), wo_ref[512:768, :])
    out = out + _dot(y_mem.astype(BF16), wo_ref[768:1024, :])
    o_ref[...] = out


def _out_proj(x2, oc, os_, ow, zn, vp, zp, om, zm, wo, wpool, pscale, seq_len, tm):
    n = x2.shape[0]
    row = lambda w: pl.BlockSpec((tm, w), lambda i: (i, 0))
    full = lambda a: pl.BlockSpec(a.shape, lambda i: (0,) * a.ndim)
    hb = tm // POOL_HALO
    halo = pl.BlockSpec((POOL_HALO, POOL_WIDTH), lambda i: (jnp.maximum(i * hb - 1, 0), 0))
    return pl.pallas_call(
        functools.partial(_out_kernel, seq_len),
        grid=(n // tm,),
        in_specs=[row(D_MODEL), row(512), row(512), row(512), row(512), row(256), halo, row(256),
                  row(256), row(256), full(wo), full(wpool), full(pscale)],
        out_specs=row(D_MODEL),
        out_shape=jax.ShapeDtypeStruct((n, D_MODEL), F32),
        compiler_params=_vmem_params(("parallel",)),
    )(x2, oc, os_, ow, zn, vp, vp, zp, om, zm, wo, wpool, pscale)


def _block_diag_ones(width):
    i = np.arange(width) // HEAD_DIM
    return jnp.asarray((i[:, None] == i[None, :]).astype(np.float32), dtype=BF16)


def _pack_w_in(w_in):
    q, kv, gate, zn, vp, zp, qm, zm = jnp.split(
        w_in, np.cumsum([512, 768, 24, 512, 256, 256, 256, 256])[:-1].tolist(), axis=1)
    cols = np.zeros((NSA_KV_HEADS, LANES), np.int64)
    used = np.zeros((NSA_KV_HEADS, LANES), bool)
    for g in range(NSA_KV_HEADS):
        for k in range(N_BRANCH):
            for r in range(NSA_GROUP):
                cols[g, k * NSA_GROUP + r] = (g * NSA_GROUP + r) * N_BRANCH + k
                used[g, k * NSA_GROUP + r] = True
    gate_p = jnp.where(jnp.asarray(used.reshape(-1))[None, :], gate[:, cols.reshape(-1)], 0.0)
    return jnp.concatenate([q, kv, gate_p, zn, vp, zp, qm, zm], axis=1).astype(BF16)


def _pack_cmp(pos, w1, w2):
    half = CMP_LEN // 2

    def w1_half(wh):
        z = jnp.zeros((half, NSA_KV_HEADS, HEAD_DIM, NSA_KV_HEADS, HEAD_DIM), F32)
        for g in range(NSA_KV_HEADS):
            z = z.at[:, g, :, g, :].set(wh)
        return z.reshape(half * KV_WIDTH, KV_WIDTH).astype(BF16)

    def pos_half(ph):
        return jnp.tile(ph[:, None, :], (1, NSA_KV_HEADS, 1)).reshape(1, half * KV_WIDTH)

    w2_bd = jnp.zeros((NSA_KV_HEADS, HEAD_DIM, NSA_KV_HEADS, HEAD_DIM), F32)
    for g in range(NSA_KV_HEADS):
        w2_bd = w2_bd.at[g, :, g, :].set(w2)
    return (pos_half(pos[:half]), pos_half(pos[half:]), w1_half(w1[:half]), w1_half(w1[half:]),
            w2_bd.reshape(KV_WIDTH, KV_WIDTH).astype(BF16))


def _sel_map_t(ncp, s):
    nc = (s - CMP_LEN) // CMP_STRIDE + 1
    ns = s // SEL_BLOCK
    c0 = np.arange(nc) * CMP_STRIDE
    c1 = c0 + CMP_LEN
    s0 = np.arange(ns) * SEL_BLOCK
    s1 = s0 + SEL_BLOCK
    ov = np.clip(np.minimum(c1[:, None], s1[None, :]) - np.maximum(c0[:, None], s0[None, :]), 0, None)
    m = np.zeros((NS_PAD, ncp), np.float32)
    m[:ns, :nc] = (ov / CMP_LEN).T
    return jnp.asarray(m, dtype=BF16)


def kernel(x, mem, positions, g_norm, w_in, g_q_nsa, g_k_cmp, g_k_slc, g_k_win, cmp_pos_k, w_cmp_k1,
           w_cmp_k2, cmp_pos_v, w_cmp_v1, w_cmp_v2, w_pool, pool_scale, g_mem, w_mem_kv, g_q_mem,
           g_k_mem, w_out):
    b, s, d = x.shape
    depth = g_norm.shape[0]
    assert d == D_MODEL and s % WINDOW == 0 and s // SEL_BLOCK >= SEL_TOPK and s // SEL_BLOCK <= NS_PAD
    n = b * s
    ncp = s // CMP_STRIDE
    tm = 512
    t_cmp = 256
    t_sel = 256
    t_mem = 512
    t_win = WINDOW
    assert s % tm == 0 and tm % t_sel == 0 and tm == WINDOW

    inv_freq = ROPE_THETA ** (-jnp.arange(ROPE_HALF, dtype=F32) / ROPE_HALF)
    invf = inv_freq[(np.arange(LANES) % HEAD_DIM) % ROPE_HALF][None, :]
    bd512, bd256, bd128 = _block_diag_ones(512), _block_diag_ones(256), _block_diag_ones(128)
    pos2 = positions.reshape(n, 1)
    end_idx = np.minimum(np.arange(ncp) * CMP_STRIDE + CMP_LEN - 1, s - 1)
    pend = positions[:, end_idx][:, :, None]
    selt = _sel_map_t(ncp, s)
    et = jnp.asarray((np.arange(s)[:, None] // SEL_BLOCK == np.arange(NS_PAD)[None, :]).astype(np.float32),
                     dtype=BF16)
    tile = lambda g, reps: jnp.tile(g[None, :], (1, reps))

    for l in range(depth):
        x2 = x.reshape(n, d)
        w_all = _pack_w_in(w_in[l])
        (q, kc_raw, vc_raw, ks, vs, kw, vw, sig, zn, vp, zp, qm, zm) = _in_proj(
            x2, pos2, g_norm[l][None, :], w_all, bd512, bd256, tile(g_q_nsa[l], 8), tile(g_k_slc[l], 2),
            tile(g_k_win[l], 2), tile(g_q_mem[l], 4), invf, tm, t_win)

        cmp_consts = (_pack_cmp(cmp_pos_k[l], w_cmp_k1[l], w_cmp_k2[l])
                      + _pack_cmp(cmp_pos_v[l], w_cmp_v1[l], w_cmp_v2[l].T)
                      + (bd128, tile(g_k_cmp[l], 2), invf))
        kc, vc = _compress(kc_raw.reshape(b, ncp, CMP_STRIDE * KV_WIDTH),
                           vc_raw.reshape(b, ncp, CMP_STRIDE * KV_WIDTH), pend, cmp_consts)

        q3 = q.reshape(b, s, NSA_WIDTH)
        sig3 = sig.reshape(b, s, 256)
        o_c, bias = _cmp_attention(q3, kc, vc, sig3, selt, t_cmp)
        dup4 = lambda a: a.reshape(NSA_KV_HEADS, b, s, LANES)
        vt5 = lambda a: a.reshape(NSA_KV_HEADS, b, s // tm, LANES, tm)
        o_s = _sel_attention(q3, bias, et, dup4(ks), vt5(vs), sig3, t_sel)
        o_w = _win_attention(q3, dup4(kw), vw, sig3, t_win)

        mk, mv = _mem_kv(mem, g_mem[l][None, :], w_mem_kv[l].astype(BF16), bd256, tile(g_k_mem[l], 4))
        o_m = _mem_attention(qm.reshape(b, s, MEM_WIDTH), mk, mv, t_mem)

        wpool = jnp.zeros((4, HEAD_DIM, 4, HEAD_DIM), F32)
        for g in range(4):
            wpool = wpool.at[g, :, g, :].set(w_pool[l, g])
        out = _out_proj(x2, o_c.reshape(n, 512), o_s.reshape(n, 512), o_w.reshape(n, 512), zn, vp, zp,
                        o_m.reshape(n, MEM_WIDTH), zm, w_out[l].astype(BF16),
                        wpool.reshape(POOL_WIDTH, POOL_WIDTH).astype(BF16), pool_scale[l][None, :], s, tm)
        x = out.reshape(b, s, d)
    return x
```

```python
import functools

import numpy as np
import jax
import jax.numpy as jnp
from jax import lax
from jax.experimental import pallas as pl
from jax.experimental.pallas import tpu as pltpu

F32 = jnp.float32
BF16 = jnp.bfloat16

D_MODEL = 1024
HEAD_DIM = 64
NSA_HEADS = 8
NSA_KV_HEADS = 2
NSA_GROUP = NSA_HEADS // NSA_KV_HEADS
NSA_WIDTH = NSA_HEADS * HEAD_DIM
KV_WIDTH = NSA_KV_HEADS * HEAD_DIM
N_BRANCH = 3
CMP_LEN = 32
CMP_STRIDE = 16
SEL_BLOCK = 64
SEL_TOPK = 16
WINDOW = 512
FORCE_SCORE = 1e4
N_FORCED = 3
POOL_WINDOWS = (2, 4, 8, 16)
POOL_WIDTH = 256
MEM_HEADS = 4
MEM_WIDTH = MEM_HEADS * HEAD_DIM
ROPE_THETA = 500000.0
ROPE_DIM = HEAD_DIM // 4
ROPE_HALF = ROPE_DIM // 2
EPS = 1e-6
SCALE = HEAD_DIM ** -0.5
Q_SCALE = SCALE * float(np.log2(np.e))

LANES = 128
NS_PAD = LANES
MASK_BIAS = -1e9
POOL_HALO = 16
CMP_CHUNK = LANES
TINY = float(np.finfo(np.float32).tiny)

_SEG = dict(q=(0, 512), kv=(512, 768), gate=(1280, 256), zn=(1536, 512), vp=(2048, 256),
            zp=(2304, 256), qm=(2560, 256), zm=(2816, 256))
W_ALL_WIDTH = 3072

_NT = (((1,), (1,)), ((), ()))


def _dot(a, b):
    return jnp.dot(a, b, preferred_element_type=F32)


def _dot_nt(a, b):
    return lax.dot_general(a, b, _NT, preferred_element_type=F32)


def _vmem_params(sem):
    return pltpu.CompilerParams(dimension_semantics=sem, vmem_limit_bytes=56 * 1024 * 1024)


def _rope_tables(pos_f32, invf):
    ang = pos_f32 * invf
    c, s = jnp.cos(ang), jnp.sin(ang)
    d = lax.broadcasted_iota(jnp.int32, ang.shape, 1) % HEAD_DIM
    tc = jnp.where(d < ROPE_DIM, c, 1.0)
    ts_up = jnp.where((d >= ROPE_HALF) & (d < ROPE_DIM), s, 0.0)
    ts_dn = jnp.where(d < ROPE_HALF, -s, 0.0)
    return tc, ts_up, ts_dn


def _rope(x, tables):
    tc, ts_up, ts_dn = tables
    w = x.shape[1]
    reps = w // LANES
    if reps > 1:
        tc, ts_up, ts_dn = (jnp.concatenate([t] * reps, axis=1) for t in (tc, ts_up, ts_dn))
    return (x * tc + pltpu.roll(x, ROPE_HALF, 1) * ts_up
            + pltpu.roll(x, w - ROPE_HALF, 1) * ts_dn)


def _head_rms(x, ones_bd, gain):
    ss = _dot((x * x).astype(BF16), ones_bd)
    return x * lax.rsqrt(ss * (1.0 / HEAD_DIM) + EPS) * gain


def _dup_halves(x):
    low = lax.broadcasted_iota(jnp.int32, x.shape, 1) < HEAD_DIM
    r = pltpu.roll(x, HEAD_DIM, 1)
    return jnp.where(low, x, r), jnp.where(low, r, x)


def _values_t(v):
    return _values_rows(v.T)


def _values_rows(vt):
    vt = vt.astype(BF16)
    ones = jnp.ones((HEAD_DIM, vt.shape[1]), BF16)
    return (jnp.concatenate([vt[0:HEAD_DIM], ones], axis=0),
            jnp.concatenate([vt[HEAD_DIM:], ones], axis=0))


def _silu(z):
    return z * jax.nn.sigmoid(z)


def _in_proj_kernel(x_ref, pos_ref, gn_ref, w_ref, bd512_ref, bd256_ref, gq_ref, gks_ref, gkw_ref,
                    gqm_ref, invf_ref,
                    q_ref, kc_ref, vc_ref, ks_ref, vs_ref, kw_ref, vw_ref, sig_ref, zn_ref, vp_ref,
                    zp_ref, qm_ref, zm_ref):
    x = x_ref[...]
    h = x * lax.rsqrt(jnp.mean(x * x, axis=-1, keepdims=True) + EPS) * gn_ref[...]
    hb = h.astype(BF16)

    def seg(name):
        o, w = _SEG[name]
        return _dot(hb, w_ref[:, o:o + w])

    tables = _rope_tables(pos_ref[...].astype(F32), invf_ref[...])

    q = _head_rms(seg('q'), bd512_ref[...], gq_ref[...])
    q_ref[...] = (_rope(q, tables) * Q_SCALE).astype(BF16)

    kv = seg('kv')
    kc_ref[...] = kv[:, 0:128]
    vc_ref[...] = kv[:, 128:256]

    def norm_dup(k, gain):
        k = k * lax.rsqrt(jnp.mean(k * k, axis=-1, keepdims=True) + EPS) * gain
        return _rope(k, tables).astype(BF16)

    ks0, ks1 = _dup_halves(kv[:, 256:384])
    ks_ref[0] = norm_dup(ks0, gks_ref[...])
    ks_ref[1] = norm_dup(ks1, gks_ref[...])
    vs_ref[0, 0], vs_ref[1, 0] = _values_t(kv[:, 384:512])
    kw0, kw1 = _dup_halves(kv[:, 512:640])
    kw_ref[0] = norm_dup(kw0, gkw_ref[...])
    kw_ref[1] = norm_dup(kw1, gkw_ref[...])
    vw = _values_t(kv[:, 640:768])
    wt = vw_ref.shape[3]
    for g in range(NSA_KV_HEADS):
        for c in range(vw_ref.shape[1]):
            vw_ref[g, c] = vw[g][:, c * wt:(c + 1) * wt]

    sig_ref[...] = jax.nn.sigmoid(seg('gate'))
    zn_ref[...] = _silu(seg('zn')).astype(BF16)
    vp_ref[...] = seg('vp')
    zp_ref[...] = _silu(seg('zp')).astype(BF16)
    qm = _head_rms(seg('qm'), bd256_ref[...], gqm_ref[...])
    qm_ref[...] = (qm * Q_SCALE).astype(BF16)
    zm_ref[...] = _silu(seg('zm')).astype(BF16)


def _in_proj(x2, pos2, gn, w_all, bd512, bd256, gq, gks, gkw, gqm, invf, tm, t_win):
    n = x2.shape[0]
    row = lambda w: pl.BlockSpec((tm, w), lambda i: (i, 0))
    full = lambda a: pl.BlockSpec(a.shape, lambda i: (0,) * a.ndim)
    dup = pl.BlockSpec((NSA_KV_HEADS, tm, LANES), lambda i: (0, i, 0))
    sd = jax.ShapeDtypeStruct
    vst = pl.BlockSpec((NSA_KV_HEADS, 1, LANES, tm), lambda i: (0, i, 0, 0))
    vwt = pl.BlockSpec((NSA_KV_HEADS, tm // t_win, LANES, t_win), lambda i: (0, i, 0, 0))
    out_shape = [sd((n, 512), BF16), sd((n, 128), F32), sd((n, 128), F32),
                 sd((2, n, 128), BF16), sd((2, n // tm, LANES, tm), BF16), sd((2, n, 128), BF16),
                 sd((2, n // t_win, LANES, t_win), BF16), sd((n, 256), F32), sd((n, 512), BF16), sd((n, 256), F32),
                 sd((n, 256), BF16), sd((n, 256), BF16), sd((n, 256), BF16)]
    out_specs = [row(512), row(128), row(128), dup, vst, dup, vwt, row(256), row(512), row(256),
                 row(256), row(256), row(256)]
    consts = (gn, w_all, bd512, bd256, gq, gks, gkw, gqm, invf)
    return pl.pallas_call(
        _in_proj_kernel,
        grid=(n // tm,),
        in_specs=[row(D_MODEL), row(1)] + [full(a) for a in consts],
        out_specs=out_specs,
        out_shape=out_shape,
        compiler_params=_vmem_params(("parallel",)),
    )(x2, pos2, *consts)


def _compress_kernel(zk_ref, zv_ref, pend_ref, pak_ref, pbk_ref, w1ak_ref, w1bk_ref, w2k_ref,
                     pav_ref, pbv_ref, w1av_ref, w1bv_ref, w2v_ref, bd128_ref, gk_ref, invf_ref,
                     kc_ref, vc_ref):
    ncp = zk_ref.shape[1]

    def mlp(z, pa, pb, w1a, w1b, w2, transposed=False):
        a = _dot((z + pa).astype(BF16), w1a)
        b = _dot((z + pb).astype(BF16), w1b)
        pre = a + pltpu.roll(b, ncp - 1, 0)
        hid = jax.nn.gelu(pre).astype(BF16)
        return _dot_nt(w2, hid) if transposed else _dot(hid, w2)

    kc = mlp(zk_ref[0], pak_ref[...], pbk_ref[...], w1ak_ref[...], w1bk_ref[...], w2k_ref[...])
    kc = _head_rms(kc, bd128_ref[...], gk_ref[...])
    kc = _rope(kc, _rope_tables(pend_ref[0].astype(F32), invf_ref[...]))
    k0, k1 = _dup_halves(kc)
    kc_ref[0, 0] = k0.astype(BF16)
    kc_ref[0, 1] = k1.astype(BF16)
    vc_t = mlp(zv_ref[0], pav_ref[...], pbv_ref[...], w1av_ref[...], w1bv_ref[...], w2v_ref[...], True)
    vc_ref[0, 0], vc_ref[0, 1] = _values_rows(vc_t)


def _compress(zk, zv, pend, consts):
    b, ncp, zw = zk.shape
    full = lambda a: pl.BlockSpec(a.shape, lambda i: (0,) * a.ndim)
    zspec = pl.BlockSpec((1, ncp, zw), lambda i: (i, 0, 0))
    ospec = pl.BlockSpec((1, NSA_KV_HEADS, ncp, LANES), lambda i: (i, 0, 0, 0))
    sd = jax.ShapeDtypeStruct((b, NSA_KV_HEADS, ncp, LANES), BF16)
    return pl.pallas_call(
        _compress_kernel,
        grid=(b,),
        in_specs=[zspec, zspec, pl.BlockSpec((1, ncp, 1), lambda i: (i, 0, 0))] + [full(a) for a in consts],
        out_specs=[ospec, pl.BlockSpec((1, NSA_KV_HEADS, LANES, ncp), lambda i: (i, 0, 0, 0))],
        out_shape=[sd, jax.ShapeDtypeStruct((b, NSA_KV_HEADS, LANES, ncp), BF16)],
        compiler_params=_vmem_params(("parallel",)),
    )(zk, zv, pend, *consts)


def _head_operand(q_blk, r):
    pair = q_blk[:, (r // 2) * LANES:(r // 2 + 1) * LANES]
    low = lax.broadcasted_iota(jnp.int32, pair.shape, 1) < HEAD_DIM
    keep = low if r % 2 == 0 else jnp.logical_not(low)
    return jnp.where(keep, pair, jnp.zeros_like(pair))


def _gate_col(sig_blk, r, branch):
    c = branch * NSA_GROUP + r
    return sig_blk[:, c:c + 1]


def _heads_t(q_blk):
    out = []
    for rp in range(NSA_GROUP // 2):
        pair_t = q_blk[:, rp * LANES:(rp + 1) * LANES].astype(F32).T
        low = lax.broadcasted_iota(jnp.int32, pair_t.shape, 0) < HEAD_DIM
        out.append(jnp.where(low, pair_t, 0.0).astype(BF16))
        out.append(jnp.where(low, 0.0, pair_t).astype(BF16))
    return out


def _finish_heads(accs, sig, branch):
    outs = []
    for r, acc in enumerate(accs):
        a = acc.T
        den = jnp.maximum(pltpu.roll(a, HEAD_DIM, 1), TINY)
        outs.append(a / den * _gate_col(sig, r, branch))
    low = lax.broadcasted_iota(jnp.int32, outs[0].shape, 1) < HEAD_DIM
    return jnp.concatenate(
        [jnp.where(low, outs[0], pltpu.roll(outs[1], HEAD_DIM, 1)),
         jnp.where(low, outs[2], pltpu.roll(outs[3], HEAD_DIM, 1))], axis=1)


def _cmp_kernel(q_ref, kc_ref, vc_ref, sig_ref, selt_ref, o_ref, bias_ref, psum_ref, acc_ref):
    tq = q_ref.shape[1]
    ncp = kc_ref.shape[2]
    t0 = pl.program_id(2) * tq
    q_heads = _heads_t(q_ref[0])

    def attend(nk):
        kc = kc_ref[0, 0, 0:nk, :]
        vc_t = vc_ref[0, 0, :, 0:nk]
        t = t0 + lax.broadcasted_iota(jnp.int32, (nk, tq), 1)
        n = lax.broadcasted_iota(jnp.int32, (nk, tq), 0)
        cmask = n * CMP_STRIDE + (CMP_LEN - 1) <= t
        psum = jnp.zeros((nk, tq), F32)
        for r, q_t in enumerate(q_heads):
            s = jnp.where(cmask, _dot(kc, q_t), -jnp.inf)
            m = jnp.max(s, axis=0, keepdims=True)
            m = jnp.where(m == -jnp.inf, 0.0, m)
            e = jnp.exp2(s - m)
            psum = psum + e * (1.0 / jnp.maximum(jnp.sum(e, axis=0, keepdims=True), TINY))
            acc_ref[r] = _dot(vc_t, e.astype(BF16))
        psum_ref[0:nk, :] = psum
        if nk < ncp:
            psum_ref[nk:ncp, :] = jnp.zeros((ncp - nk, tq), F32)

    last_block = (t0 + tq - CMP_LEN) // CMP_STRIDE
    n_chunks = jnp.minimum(last_block // CMP_CHUNK + 1, ncp // CMP_CHUNK)
    for v in range(1, ncp // CMP_CHUNK + 1):
        pl.when(n_chunks == v)(functools.partial(attend, v * CMP_CHUNK))
    o_ref[0] = _finish_heads([acc_ref[r] for r in range(NSA_GROUP)], sig_ref[0], 0).astype(BF16)

    psum = psum_ref[...]
    ps_hi = psum.astype(BF16)
    ps_lo = (psum - ps_hi.astype(F32)).astype(BF16)
    selt = selt_ref[...]
    imp = _dot(selt, ps_hi) + _dot(selt, ps_lo)
    j = lax.broadcasted_iota(jnp.int32, (NS_PAD, tq), 0)
    cur = (t0 + lax.broadcasted_iota(jnp.int32, (NS_PAD, tq), 1)) >> 6
    forced = (j == 0) | (j == cur) | (j == cur - 1)
    valid = j <= cur
    vals = jnp.where(forced, -jnp.inf, jnp.where(valid, imp, -1.0))
    neg_j = -j.astype(F32)
    for _ in range(SEL_TOPK - N_FORCED):
        m = jnp.max(vals, axis=0, keepdims=True)
        first = jnp.max(jnp.where(vals == m, neg_j, -jnp.inf), axis=0, keepdims=True)
        vals = jnp.where(neg_j == first, -jnp.inf, vals)
    bias_t = jnp.where((vals == -jnp.inf) & valid, 0.0, MASK_BIAS)
    bias_ref[0, 0] = bias_t.astype(BF16)


def _cmp_attention(q3, kc, vc, sig3, selt, tq):
    b, s, _ = q3.shape
    ncp = kc.shape[2]
    assert ncp % CMP_CHUNK == 0 and tq >= CMP_LEN
    grid = (b, NSA_KV_HEADS, s // tq)
    return pl.pallas_call(
        _cmp_kernel,
        grid=grid,
        in_specs=[pl.BlockSpec((1, tq, 256), lambda bi, g, i: (bi, i, g)),
                  pl.BlockSpec((1, 1, ncp, LANES), lambda bi, g, i: (bi, g, 0, 0)),
                  pl.BlockSpec((1, 1, LANES, ncp), lambda bi, g, i: (bi, g, 0, 0)),
                  pl.BlockSpec((1, tq, LANES), lambda bi, g, i: (bi, i, g)),
                  pl.BlockSpec(selt.shape, lambda bi, g, i: (0, 0))],
        out_specs=[pl.BlockSpec((1, tq, 256), lambda bi, g, i: (bi, i, g)),
                   pl.BlockSpec((1, 1, NS_PAD, tq), lambda bi, g, i: (bi, g, 0, i))],
        out_shape=[jax.ShapeDtypeStruct((b, s, NSA_WIDTH), BF16),
                   jax.ShapeDtypeStruct((b, NSA_KV_HEADS, NS_PAD, s), BF16)],
        scratch_shapes=[pltpu.VMEM((ncp, tq), F32), pltpu.VMEM((NSA_GROUP, LANES, tq), F32)],
        compiler_params=_vmem_params(("parallel", "parallel", "parallel")),
    )(q3, kc, vc, sig3, selt)


def _sel_kernel(q_ref, bias_ref, et_ref, ks_ref, vs_ref, sig_ref, oc_ref, ow_ref, zn_ref, o_ref,
                qaug_ref, s_ref, m_ref, acc_ref):
    tq = q_ref.shape[1]
    tk = vs_ref.shape[4]
    cols = NSA_GROUP * tq
    qt = pl.program_id(2)

    bias = bias_ref[0, 0]
    for r, q_t in enumerate(_heads_t(q_ref[0])):
        qaug_ref[0:LANES, r * tq:(r + 1) * tq] = bias
        qaug_ref[LANES:2 * LANES, r * tq:(r + 1) * tq] = q_t
    m_ref[...] = jnp.full(m_ref.shape, -jnp.inf, F32)
    acc_ref[...] = jnp.zeros(acc_ref.shape, F32)
    heads = [slice(r * tq, (r + 1) * tq) for r in range(NSA_GROUP)]

    def keys(kt):
        k0 = pl.multiple_of(kt * tk, tk)
        return jnp.concatenate([et_ref[pl.ds(k0, tk), :], ks_ref[0, 0, pl.ds(k0, tk), :]], axis=1)

    def softmax_pv(kt, h, s):
        m_prev = m_ref[:, h]
        m_new = jnp.maximum(m_prev, jnp.max(s, axis=0, keepdims=True))
        p = jnp.exp2(s - m_new).astype(BF16)
        acc_ref[:, h] = jnp.exp2(m_prev - m_new) * acc_ref[:, h] + _dot(vs_ref[0, 0, kt], p)
        m_ref[:, h] = m_new

    k_first = keys(0)
    for h in heads:
        s_ref[:, h] = _dot(k_first, qaug_ref[:, h])

    def step(kt):
        k_next = keys(kt + 1)
        for h in heads:
            s = s_ref[:, h]
            s_next = _dot(k_next, qaug_ref[:, h])
            softmax_pv(kt, h, s)
            s_ref[:, h] = s_next

    def two_steps(j, first):
        step(first + 2 * j)
        step(first + 2 * j + 1)
        return first

    last = (qt * tq) // tk
    odd = last % 2
    pl.when(odd == 1)(lambda: step(0))
    lax.fori_loop(0, last // 2, two_steps, odd)
    kpos = last * tk + lax.broadcasted_iota(jnp.int32, (tk, tq), 0)
    causal = kpos <= qt * tq + lax.broadcasted_iota(jnp.int32, (tk, tq), 1)
    for h in heads:
        softmax_pv(last, h, jnp.where(causal, s_ref[:, h], MASK_BIAS))

    acc = acc_ref[...]
    o_sel = _finish_heads([acc[:, h] for h in heads], sig_ref[0], 1)
    o_nsa = oc_ref[0].astype(F32) + o_sel + ow_ref[0].astype(F32)
    o_ref[0] = (o_nsa * zn_ref[0].astype(F32)).astype(BF16)


def _sel_attention(q3, bias, et, ks, vst, sig3, o_c, o_w, zn3, tq):
    b, s, _ = q3.shape
    tk = vst.shape[4]
    cols = NSA_GROUP * tq
    group_blk = pl.BlockSpec((1, tq, 256), lambda bi, g, i: (bi, i, g))
    return pl.pallas_call(
        _sel_kernel,
        grid=(b, NSA_KV_HEADS, s // tq),
        in_specs=[group_blk,
                  pl.BlockSpec((1, 1, NS_PAD, tq), lambda bi, g, i: (bi, g, 0, i)),
                  pl.BlockSpec((s, LANES), lambda bi, g, i: (0, 0)),
                  pl.BlockSpec((1, 1, s, LANES), lambda bi, g, i: (g, bi, 0, 0)),
                  pl.BlockSpec((1, 1, s // tk, LANES, tk), lambda bi, g, i: (g, bi, 0, 0, 0)),
                  pl.BlockSpec((1, tq, LANES), lambda bi, g, i: (bi, i, g)),
                  group_blk, group_blk, group_blk],
        out_specs=group_blk,
        out_shape=jax.ShapeDtypeStruct((b, s, NSA_WIDTH), BF16),
        scratch_shapes=[pltpu.VMEM((2 * LANES, cols), BF16), pltpu.VMEM((tk, cols), F32),
                        pltpu.VMEM((1, cols), F32), pltpu.VMEM((LANES, cols), F32)],
        compiler_params=_vmem_params(("parallel", "parallel", "parallel")),
    )(q3, bias, et, ks, vst, sig3, o_c, o_w, zn3)


def _win_kernel(q_ref, kp_ref, kc_ref, vp_ref, vc_ref, sig_ref, o_ref):
    tq = q_ref.shape[1]
    qt = pl.program_id(2)
    c = lax.broadcasted_iota(jnp.int32, (tq, tq), 0)
    i = lax.broadcasted_iota(jnp.int32, (tq, tq), 1)
    mask_cur = c <= i
    mask_prev = c > i + jnp.where(qt > 0, 0, tq)
    accs = []
    for q_t in _heads_t(q_ref[0]):
        s_p = jnp.where(mask_prev, _dot(kp_ref[0, 0], q_t), -jnp.inf)
        s_c = jnp.where(mask_cur, _dot(kc_ref[0, 0], q_t), -jnp.inf)
        m = jnp.maximum(jnp.max(s_p, axis=0, keepdims=True), jnp.max(s_c, axis=0, keepdims=True))
        e_p = jnp.exp2(s_p - m).astype(BF16)
        e_c = jnp.exp2(s_c - m).astype(BF16)
        accs.append(_dot(vp_ref[0, 0], e_p) + _dot(vc_ref[0, 0], e_c))
    o_ref[0] = _finish_heads(accs, sig_ref[0], 2).astype(BF16)


def _win_attention(q3, kw, vw, sig3, tq):
    b, s, _ = q3.shape
    assert tq == WINDOW
    nt = s // tq
    kspec = lambda back: pl.BlockSpec((1, 1, tq, LANES),
                                      lambda bi, g, i: (g, bi, jnp.maximum(i - back, 0), 0))
    vspec = lambda back: pl.BlockSpec((1, 1, LANES, tq),
                                      lambda bi, g, i: (g, bi * nt + jnp.maximum(i - back, 0), 0, 0))
    return pl.pallas_call(
        _win_kernel,
        grid=(b, NSA_KV_HEADS, nt),
        in_specs=[pl.BlockSpec((1, tq, 256), lambda bi, g, i: (bi, i, g)), kspec(1), kspec(0),
                  vspec(1), vspec(0), pl.BlockSpec((1, tq, LANES), lambda bi, g, i: (bi, i, g))],
        out_specs=pl.BlockSpec((1, tq, 256), lambda bi, g, i: (bi, i, g)),
        out_shape=jax.ShapeDtypeStruct((b, s, NSA_WIDTH), BF16),
        compiler_params=_vmem_params(("parallel", "parallel", "parallel")),
    )(q3, kw, kw, vw, vw, sig3)


def _mem_kv_kernel(mem_ref, gm_ref, w_ref, bd256_ref, gk_ref, mk_ref, mv_ref):
    x = mem_ref[0]
    h = x * lax.rsqrt(jnp.mean(x * x, axis=-1, keepdims=True) + EPS) * gm_ref[...]
    mkv = _dot(h.astype(BF16), w_ref[...])
    mk_ref[0] = _head_rms(mkv[:, 0:MEM_WIDTH], bd256_ref[...], gk_ref[...]).astype(BF16)
    mv_ref[0] = mkv[:, MEM_WIDTH:2 * MEM_WIDTH].astype(BF16)


def _mem_kv(mem, gm, w, bd256, gk):
    b, m, _ = mem.shape
    full = lambda a: pl.BlockSpec(a.shape, lambda i: (0,) * a.ndim)
    ospec = pl.BlockSpec((1, m, MEM_WIDTH), lambda i: (i, 0, 0))
    sd = jax.ShapeDtypeStruct((b, m, MEM_WIDTH), BF16)
    return pl.pallas_call(
        _mem_kv_kernel,
        grid=(b,),
        in_specs=[pl.BlockSpec((1, m, D_MODEL), lambda i: (i, 0, 0)), full(gm), full(w), full(bd256), full(gk)],
        out_specs=[ospec, ospec],
        out_shape=[sd, sd],
        compiler_params=_vmem_params(("parallel",)),
    )(mem, gm, w, bd256, gk)


def _mem_attn_kernel(q_ref, mk_ref, mv_ref, o_ref):
    q_blk = q_ref[0]
    mk = mk_ref[0]
    mv = mv_ref[0]
    outs = []
    for h in range(MEM_HEADS):
        hp = h // 2
        s = _dot_nt(_head_operand(q_blk, h), mk[:, hp * LANES:(hp + 1) * LANES])
        m = jnp.max(s, axis=-1, keepdims=True)
        e = jnp.exp2(s - m)
        p = e / jnp.sum(e, axis=-1, keepdims=True)
        outs.append(_dot(p.astype(BF16), mv[:, hp * LANES:(hp + 1) * LANES]))
    low = lax.broadcasted_iota(jnp.int32, outs[0].shape, 1) < HEAD_DIM
    o_ref[0] = jnp.concatenate([jnp.where(low, outs[0], outs[1]), jnp.where(low, outs[2], outs[3])],
                               axis=1).astype(BF16)


def _mem_attention(qm3, mk, mv, tq):
    b, s, _ = qm3.shape
    m = mk.shape[1]
    kv = pl.BlockSpec((1, m, MEM_WIDTH), lambda bi, i: (bi, 0, 0))
    return pl.pallas_call(
        _mem_attn_kernel,
        grid=(b, s // tq),
        in_specs=[pl.BlockSpec((1, tq, MEM_WIDTH), lambda bi, i: (bi, i, 0)), kv, kv],
        out_specs=pl.BlockSpec((1, tq, MEM_WIDTH), lambda bi, i: (bi, i, 0)),
        out_shape=jax.ShapeDtypeStruct((b, s, MEM_WIDTH), BF16),
        compiler_params=_vmem_params(("parallel", "parallel")),
    )(qm3, mk, mv)


def _out_kernel(seq_len, x_ref, ynsa_ref, vp_ref, halo_ref, zp_ref, om_ref, zm_ref, wo_ref, wpool_ref,
                pscale_ref, o_ref):
    tm = x_ref.shape[0]
    t0 = (pl.program_id(0) * tm) % seq_len

    v = vp_ref[...]
    halo = halo_ref[...] * (t0 > 0).astype(F32)
    e = jnp.concatenate([halo, v], axis=0)
    w2 = e + pltpu.roll(e, 1, 0)
    w4 = w2 + pltpu.roll(w2, 2, 0)
    w8 = w4 + pltpu.roll(w4, 4, 0)
    w16 = w8 + pltpu.roll(w8, 8, 0)
    shape = (tm, POOL_WIDTH)
    grp = lax.broadcasted_iota(jnp.int32, shape, 1) >> 6
    wsum = jnp.where(grp == 0, w2[POOL_HALO:],
                     jnp.where(grp == 1, w4[POOL_HALO:], jnp.where(grp == 2, w8[POOL_HALO:], w16[POOL_HALO:])))
    width = jnp.left_shift(2, grp)
    cnt = jnp.minimum(t0 + lax.broadcasted_iota(jnp.int32, shape, 0) + 1, width).astype(F32)
    pooled = wsum / cnt - v
    y_pool = _dot(pooled.astype(BF16), wpool_ref[...]) * pscale_ref[...] * zp_ref[...].astype(F32)

    y_mem = om_ref[...].astype(F32) * zm_ref[...].astype(F32)
    out = x_ref[...] + _dot(ynsa_ref[...], wo_ref[0:512, :])
    out = out + _dot(y_pool.astype(BF16), wo_ref[512:768, :])
    out = out + _dot(y_mem.astype(BF16), wo_ref[768:1024, :])
    o_ref[...] = out


def _out_proj(x2, ynsa, vp, zp, om, zm, wo, wpool, pscale, seq_len, tm):
    n = x2.shape[0]
    row = lambda w: pl.BlockSpec((tm, w), lambda i: (i, 0))
    full = lambda a: pl.BlockSpec(a.shape, lambda i: (0,) * a.ndim)
    hb = tm // POOL_HALO
    halo = pl.BlockSpec((POOL_HALO, POOL_WIDTH), lambda i: (jnp.maximum(i * hb - 1, 0), 0))
    return pl.pallas_call(
        functools.partial(_out_kernel, seq_len),
        grid=(n // tm,),
        in_specs=[row(D_MODEL), row(512), row(256), halo, row(256),
                  row(256), row(256), full(wo), full(wpool), full(pscale)],
        out_specs=row(D_MODEL),
        out_shape=jax.ShapeDtypeStruct((n, D_MODEL), F32),
        compiler_params=_vmem_params(("parallel",)),
    )(x2, ynsa, vp, vp, zp, om, zm, wo, wpool, pscale)


def _block_diag_ones(width):
    i = np.arange(width) // HEAD_DIM
    return jnp.asarray((i[:, None] == i[None, :]).astype(np.float32), dtype=BF16)


def _pack_w_in(w_in):
    q, kv, gate, zn, vp, zp, qm, zm = jnp.split(
        w_in, np.cumsum([512, 768, 24, 512, 256, 256, 256, 256])[:-1].tolist(), axis=1)
    cols = np.zeros((NSA_KV_HEADS, LANES), np.int64)
    used = np.zeros((NSA_KV_HEADS, LANES), bool)
    for g in range(NSA_KV_HEADS):
        for k in range(N_BRANCH):
            for r in range(NSA_GROUP):
                cols[g, k * NSA_GROUP + r] = (g * NSA_GROUP + r) * N_BRANCH + k
                used[g, k * NSA_GROUP + r] = True
    gate_p = jnp.where(jnp.asarray(used.reshape(-1))[None, :], gate[:, cols.reshape(-1)], 0.0)
    return jnp.concatenate([q, kv, gate_p, zn, vp, zp, qm, zm], axis=1).astype(BF16)


def _pack_cmp(pos, w1, w2):
    half = CMP_LEN // 2

    def w1_half(wh):
        z = jnp.zeros((half, NSA_KV_HEADS, HEAD_DIM, NSA_KV_HEADS, HEAD_DIM), F32)
        for g in range(NSA_KV_HEADS):
            z = z.at[:, g, :, g, :].set(wh)
        return z.reshape(half * KV_WIDTH, KV_WIDTH).astype(BF16)

    def pos_half(ph):
        return jnp.tile(ph[:, None, :], (1, NSA_KV_HEADS, 1)).reshape(1, half * KV_WIDTH)

    w2_bd = jnp.zeros((NSA_KV_HEADS, HEAD_DIM, NSA_KV_HEADS, HEAD_DIM), F32)
    for g in range(NSA_KV_HEADS):
        w2_bd = w2_bd.at[g, :, g, :].set(w2)
    return (pos_half(pos[:half]), pos_half(pos[half:]), w1_half(w1[:half]), w1_half(w1[half:]),
            w2_bd.reshape(KV_WIDTH, KV_WIDTH).astype(BF16))


def _sel_map_t(ncp, s):
    nc = (s - CMP_LEN) // CMP_STRIDE + 1
    ns = s // SEL_BLOCK
    c0 = np.arange(nc) * CMP_STRIDE
    c1 = c0 + CMP_LEN
    s0 = np.arange(ns) * SEL_BLOCK
    s1 = s0 + SEL_BLOCK
    ov = np.clip(np.minimum(c1[:, None], s1[None, :]) - np.maximum(c0[:, None], s0[None, :]), 0, None)
    m = np.zeros((NS_PAD, ncp), np.float32)
    m[:ns, :nc] = (ov / CMP_LEN).T
    return jnp.asarray(m, dtype=BF16)


def kernel(x, mem, positions, g_norm, w_in, g_q_nsa, g_k_cmp, g_k_slc, g_k_win, cmp_pos_k, w_cmp_k1,
           w_cmp_k2, cmp_pos_v, w_cmp_v1, w_cmp_v2, w_pool, pool_scale, g_mem, w_mem_kv, g_q_mem,
           g_k_mem, w_out):
    b, s, d = x.shape
    depth = g_norm.shape[0]
    assert d == D_MODEL and s % WINDOW == 0 and s // SEL_BLOCK >= SEL_TOPK and s // SEL_BLOCK <= NS_PAD
    n = b * s
    ncp = s // CMP_STRIDE
    tm = 512
    t_cmp = 256
    t_sel = 256
    t_mem = 512
    t_win = WINDOW
    assert s % tm == 0 and tm % t_sel == 0 and tm == WINDOW

    inv_freq = ROPE_THETA ** (-jnp.arange(ROPE_HALF, dtype=F32) / ROPE_HALF)
    invf = inv_freq[(np.arange(LANES) % HEAD_DIM) % ROPE_HALF][None, :]
    bd512, bd256, bd128 = _block_diag_ones(512), _block_diag_ones(256), _block_diag_ones(128)
    pos2 = positions.reshape(n, 1)
    end_idx = np.minimum(np.arange(ncp) * CMP_STRIDE + CMP_LEN - 1, s - 1)
    pend = positions[:, end_idx][:, :, None]
    selt = _sel_map_t(ncp, s)
    et = jnp.asarray((np.arange(s)[:, None] // SEL_BLOCK == np.arange(NS_PAD)[None, :]).astype(np.float32),
                     dtype=BF16)
    tile = lambda g, reps: jnp.tile(g[None, :], (1, reps))

    for l in range(depth):
        x2 = x.reshape(n, d)
        w_all = _pack_w_in(w_in[l])
        (q, kc_raw, vc_raw, ks, vs, kw, vw, sig, zn, vp, zp, qm, zm) = _in_proj(
            x2, pos2, g_norm[l][None, :], w_all, bd512, bd256, tile(g_q_nsa[l], 8), tile(g_k_slc[l], 2),
            tile(g_k_win[l], 2), tile(g_q_mem[l], 4), invf, tm, t_win)

        cmp_consts = (_pack_cmp(cmp_pos_k[l], w_cmp_k1[l], w_cmp_k2[l])
                      + _pack_cmp(cmp_pos_v[l], w_cmp_v1[l], w_cmp_v2[l].T)
                      + (bd128, tile(g_k_cmp[l], 2), invf))
        kc, vc = _compress(kc_raw.reshape(b, ncp, CMP_STRIDE * KV_WIDTH),
                           vc_raw.reshape(b, ncp, CMP_STRIDE * KV_WIDTH), pend, cmp_consts)

        q3 = q.reshape(b, s, NSA_WIDTH)
        sig3 = sig.reshape(b, s, 256)
        o_c, bias = _cmp_attention(q3, kc, vc, sig3, selt, t_cmp)
        dup4 = lambda a: a.reshape(NSA_KV_HEADS, b, s, LANES)
        vt5 = lambda a: a.reshape(NSA_KV_HEADS, b, s // tm, LANES, tm)
        o_w = _win_attention(q3, dup4(kw), vw, sig3, t_win)
        y_nsa = _sel_attention(q3, bias, et, dup4(ks), vt5(vs), sig3, o_c, o_w,
                               zn.reshape(b, s, NSA_WIDTH), t_sel)

        mk, mv = _mem_kv(mem, g_mem[l][None, :], w_mem_kv[l].astype(BF16), bd256, tile(g_k_mem[l], 4))
        o_m = _mem_attention(qm.reshape(b, s, MEM_WIDTH), mk, mv, t_mem)

        wpool = jnp.zeros((4, HEAD_DIM, 4, HEAD_DIM), F32)
        for g in range(4):
            wpool = wpool.at[g, :, g, :].set(w_pool[l, g])
        out = _out_proj(x2, y_nsa.reshape(n, NSA_WIDTH), vp, zp,
                        o_m.reshape(n, MEM_WIDTH), zm, w_out[l].astype(BF16),
                        wpool.reshape(POOL_WIDTH, POOL_WIDTH).astype(BF16), pool_scale[l][None, :], s, tm)
        x = out.reshape(b, s, d)
    return x
```

```python
import functools

import numpy as np
import jax
import jax.numpy as jnp
from jax import lax
from jax.experimental import pallas as pl
from jax.experimental.pallas import tpu as pltpu

F32 = jnp.float32
BF16 = jnp.bfloat16

D_MODEL = 1024
HEAD_DIM = 64
NSA_HEADS = 8
NSA_KV_HEADS = 2
NSA_GROUP = NSA_HEADS // NSA_KV_HEADS
NSA_WIDTH = NSA_HEADS * HEAD_DIM
KV_WIDTH = NSA_KV_HEADS * HEAD_DIM
N_BRANCH = 3
CMP_LEN = 32
CMP_STRIDE = 16
SEL_BLOCK = 64
SEL_TOPK = 16
WINDOW = 512
FORCE_SCORE = 1e4
N_FORCED = 3
POOL_WINDOWS = (2, 4, 8, 16)
POOL_WIDTH = 256
MEM_HEADS = 4
MEM_WIDTH = MEM_HEADS * HEAD_DIM
ROPE_THETA = 500000.0
ROPE_DIM = HEAD_DIM // 4
ROPE_HALF = ROPE_DIM // 2
EPS = 1e-6
SCALE = HEAD_DIM ** -0.5
Q_SCALE = SCALE * float(np.log2(np.e))

LANES = 128
NS_PAD = LANES
MASK_BIAS = -1e9
POOL_HALO = 16
CMP_CHUNK = LANES
SEL_UNROLL = 4
ROPE_ROWS = 32
TINY = float(np.finfo(np.float32).tiny)

_SEG = dict(q=(0, 512), kv=(512, 768), gate=(1280, 256), zn=(1536, 512), vp=(2048, 256),
            zp=(2304, 256), qm=(2560, 256), zm=(2816, 256))
W_ALL_WIDTH = 3072

_NT = (((1,), (1,)), ((), ()))


def _dot(a, b):
    return jnp.dot(a, b, preferred_element_type=F32)


def _dot_nt(a, b):
    return lax.dot_general(a, b, _NT, preferred_element_type=F32)


def _vmem_params(sem):
    return pltpu.CompilerParams(dimension_semantics=sem, vmem_limit_bytes=56 * 1024 * 1024)


def _rope_tables(pos_f32, invf):
    ang = pos_f32 * invf
    c, s = jnp.cos(ang), jnp.sin(ang)
    d = lax.broadcasted_iota(jnp.int32, ang.shape, 1) % HEAD_DIM
    tc = jnp.where(d < ROPE_DIM, c, 1.0)
    ts_up = jnp.where((d >= ROPE_HALF) & (d < ROPE_DIM), s, 0.0)
    ts_dn = jnp.where(d < ROPE_HALF, -s, 0.0)
    return tc, ts_up, ts_dn


def _rope(x, tables):
    tc, ts_up, ts_dn = tables
    w = x.shape[1]
    reps = w // LANES
    if reps > 1:
        tc, ts_up, ts_dn = (jnp.concatenate([t] * reps, axis=1) for t in (tc, ts_up, ts_dn))
    return (x * tc + pltpu.roll(x, ROPE_HALF, 1) * ts_up
            + pltpu.roll(x, w - ROPE_HALF, 1) * ts_dn)


def _head_rms(x, ones_bd, gain):
    ss = _dot((x * x).astype(BF16), ones_bd)
    return x * lax.rsqrt(ss * (1.0 / HEAD_DIM) + EPS) * gain


def _dup_halves(x):
    low = lax.broadcasted_iota(jnp.int32, x.shape, 1) < HEAD_DIM
    r = pltpu.roll(x, HEAD_DIM, 1)
    return jnp.where(low, x, r), jnp.where(low, r, x)


def _values_t(v):
    return _values_rows(v.T)


def _values_rows(vt):
    vt = vt.astype(BF16)
    ones = jnp.ones((HEAD_DIM, vt.shape[1]), BF16)
    return (jnp.concatenate([vt[0:HEAD_DIM], ones], axis=0),
            jnp.concatenate([vt[HEAD_DIM:], ones], axis=0))


def _silu(z):
    return z * jax.nn.sigmoid(z)


def _in_proj_kernel(x_ref, cs_ref, gn_ref, w_ref, bd512_ref, bd256_ref, gq_ref, gks_ref, gkw_ref,
                    gqm_ref, rope_e_ref,
                    q_ref, kc_ref, vc_ref, ks_ref, vs_ref, kw_ref, vw_ref, sig_ref, zn_ref, vp_ref,
                    zp_ref, qm_ref, zm_ref):
    x = x_ref[...]
    h = x * lax.rsqrt(jnp.mean(x * x, axis=-1, keepdims=True) + EPS) * gn_ref[...]
    hb = h.astype(BF16)

    def seg(name):
        o, w = _SEG[name]
        return _dot(hb, w_ref[:, o:o + w])

    cs = cs_ref[...]
    cs_hi = cs.astype(BF16)
    cs_lo = (cs - cs_hi.astype(F32)).astype(BF16)
    tables = tuple((_dot(rope_e_ref[k], cs_hi) + _dot(rope_e_ref[k], cs_lo)).T for k in range(3))

    q = _head_rms(seg('q'), bd512_ref[...], gq_ref[...])
    q_ref[...] = (_rope(q, tables) * Q_SCALE).astype(BF16)

    kv = seg('kv')
    kc_ref[...] = kv[:, 0:128]
    vc_ref[...] = kv[:, 128:256]

    def norm_dup(k, gain):
        k = k * lax.rsqrt(jnp.mean(k * k, axis=-1, keepdims=True) + EPS) * gain
        return _rope(k, tables).astype(BF16)

    ks0, ks1 = _dup_halves(kv[:, 256:384])
    ks_ref[0] = norm_dup(ks0, gks_ref[...])
    ks_ref[1] = norm_dup(ks1, gks_ref[...])
    vs_ref[0, 0], vs_ref[1, 0] = _values_t(kv[:, 384:512])
    kw0, kw1 = _dup_halves(kv[:, 512:640])
    kw_ref[0] = norm_dup(kw0, gkw_ref[...])
    kw_ref[1] = norm_dup(kw1, gkw_ref[...])
    vw = _values_t(kv[:, 640:768])
    wt = vw_ref.shape[3]
    for g in range(NSA_KV_HEADS):
        for c in range(vw_ref.shape[1]):
            vw_ref[g, c] = vw[g][:, c * wt:(c + 1) * wt]

    sig_ref[...] = jax.nn.sigmoid(seg('gate'))
    zn_ref[...] = _silu(seg('zn')).astype(BF16)
    vp_ref[...] = seg('vp')
    zp_ref[...] = _silu(seg('zp')).astype(BF16)
    qm = _head_rms(seg('qm'), bd256_ref[...], gqm_ref[...])
    qm_ref[...] = (qm * Q_SCALE).astype(BF16)
    zm_ref[...] = _silu(seg('zm')).astype(BF16)


def _rope_cs_kernel(pos_ref, invf_ref, o_ref):
    ang = invf_ref[...] * pos_ref[...].astype(F32)
    tail = lax.broadcasted_iota(jnp.int32, (ROPE_ROWS - 2 * ROPE_HALF, ang.shape[1]), 0) == 0
    o_ref[...] = jnp.concatenate([jnp.cos(ang), jnp.sin(ang), tail.astype(F32)], axis=0)


def _rope_cs(pos_row, invf_col, blk):
    n = pos_row.shape[1]
    return pl.pallas_call(
        _rope_cs_kernel,
        grid=(n // blk,),
        in_specs=[pl.BlockSpec((1, blk), lambda i: (0, i)), pl.BlockSpec(invf_col.shape, lambda i: (0, 0))],
        out_specs=pl.BlockSpec((ROPE_ROWS, blk), lambda i: (0, i)),
        out_shape=jax.ShapeDtypeStruct((ROPE_ROWS, n), F32),
        compiler_params=_vmem_params(("parallel",)),
    )(pos_row, invf_col)


def _rope_expanders():
    e = np.zeros((3, LANES, ROPE_ROWS), np.float32)
    for lane in range(LANES):
        d = lane % HEAD_DIM
        if d < ROPE_DIM:
            e[0, lane, d % ROPE_HALF] = 1.0
        else:
            e[0, lane, 2 * ROPE_HALF] = 1.0
        if ROPE_HALF <= d < ROPE_DIM:
            e[1, lane, ROPE_HALF + d % ROPE_HALF] = 1.0
        if d < ROPE_HALF:
            e[2, lane, ROPE_HALF + d] = -1.0
    return jnp.asarray(e, dtype=BF16)


def _in_proj(x2, cs, gn, w_all, bd512, bd256, gq, gks, gkw, gqm, rope_e, tm, t_win):
    n = x2.shape[0]
    row = lambda w: pl.BlockSpec((tm, w), lambda i: (i, 0))
    full = lambda a: pl.BlockSpec(a.shape, lambda i: (0,) * a.ndim)
    dup = pl.BlockSpec((NSA_KV_HEADS, tm, LANES), lambda i: (0, i, 0))
    sd = jax.ShapeDtypeStruct
    vst = pl.BlockSpec((NSA_KV_HEADS, 1, LANES, tm), lambda i: (0, i, 0, 0))
    vwt = pl.BlockSpec((NSA_KV_HEADS, tm // t_win, LANES, t_win), lambda i: (0, i, 0, 0))
    out_shape = [sd((n, 512), BF16), sd((n, 128), F32), sd((n, 128), F32),
                 sd((2, n, 128), BF16), sd((2, n // tm, LANES, tm), BF16), sd((2, n, 128), BF16),
                 sd((2, n // t_win, LANES, t_win), BF16), sd((n, 256), F32), sd((n, 512), BF16), sd((n, 256), F32),
                 sd((n, 256), BF16), sd((n, 256), BF16), sd((n, 256), BF16)]
    out_specs = [row(512), row(128), row(128), dup, vst, dup, vwt, row(256), row(512), row(256),
                 row(256), row(256), row(256)]
    consts = (gn, w_all, bd512, bd256, gq, gks, gkw, gqm, rope_e)
    return pl.pallas_call(
        _in_proj_kernel,
        grid=(n // tm,),
        in_specs=[row(D_MODEL), pl.BlockSpec((ROPE_ROWS, tm), lambda i: (0, i))] + [full(a) for a in consts],
        out_specs=out_specs,
        out_shape=out_shape,
        compiler_params=_vmem_params(("parallel",)),
    )(x2, cs, *consts)


def _compress_kernel(zk_ref, zv_ref, pend_ref, pak_ref, pbk_ref, w1ak_ref, w1bk_ref, w2k_ref,
                     pav_ref, pbv_ref, w1av_ref, w1bv_ref, w2v_ref, bd128_ref, gk_ref, invf_ref,
                     kc_ref, vc_ref):
    ncp = zk_ref.shape[1]

    def mlp(z, pa, pb, w1a, w1b, w2, transposed=False):
        a = _dot((z + pa).astype(BF16), w1a)
        b = _dot((z + pb).astype(BF16), w1b)
        pre = a + pltpu.roll(b, ncp - 1, 0)
        hid = jax.nn.gelu(pre).astype(BF16)
        return _dot_nt(w2, hid) if transposed else _dot(hid, w2)

    kc = mlp(zk_ref[0], pak_ref[...], pbk_ref[...], w1ak_ref[...], w1bk_ref[...], w2k_ref[...])
    kc = _head_rms(kc, bd128_ref[...], gk_ref[...])
    kc = _rope(kc, _rope_tables(pend_ref[0].astype(F32), invf_ref[...]))
    k0, k1 = _dup_halves(kc)
    kc_ref[0, 0] = k0.astype(BF16)
    kc_ref[0, 1] = k1.astype(BF16)
    vc_t = mlp(zv_ref[0], pav_ref[...], pbv_ref[...], w1av_ref[...], w1bv_ref[...], w2v_ref[...], True)
    vc_ref[0, 0], vc_ref[0, 1] = _values_rows(vc_t)


def _compress(zk, zv, pend, consts):
    b, ncp, zw = zk.shape
    full = lambda a: pl.BlockSpec(a.shape, lambda i: (0,) * a.ndim)
    zspec = pl.BlockSpec((1, ncp, zw), lambda i: (i, 0, 0))
    ospec = pl.BlockSpec((1, NSA_KV_HEADS, ncp, LANES), lambda i: (i, 0, 0, 0))
    sd = jax.ShapeDtypeStruct((b, NSA_KV_HEADS, ncp, LANES), BF16)
    return pl.pallas_call(
        _compress_kernel,
        grid=(b,),
        in_specs=[zspec, zspec, pl.BlockSpec((1, ncp, 1), lambda i: (i, 0, 0))] + [full(a) for a in consts],
        out_specs=[ospec, pl.BlockSpec((1, NSA_KV_HEADS, LANES, ncp), lambda i: (i, 0, 0, 0))],
        out_shape=[sd, jax.ShapeDtypeStruct((b, NSA_KV_HEADS, LANES, ncp), BF16)],
        compiler_params=_vmem_params(("parallel",)),
    )(zk, zv, pend, *consts)


def _gate_col(sig_blk, r, branch):
    c = branch * NSA_GROUP + r
    return sig_blk[:, c:c + 1]


def _heads_t(q_blk):
    out = []
    for rp in range(NSA_GROUP // 2):
        pair_t = q_blk[:, rp * LANES:(rp + 1) * LANES].astype(F32).T
        low = lax.broadcasted_iota(jnp.int32, pair_t.shape, 0) < HEAD_DIM
        out.append(jnp.where(low, pair_t, 0.0).astype(BF16))
        out.append(jnp.where(low, 0.0, pair_t).astype(BF16))
    return out


def _finish_heads(accs, sig, branch):
    outs = []
    for r, acc in enumerate(accs):
        a = acc.T
        o = a / jnp.maximum(pltpu.roll(a, HEAD_DIM, 1), TINY)
        outs.append(o if sig is None else o * _gate_col(sig, r, branch))
    low = lax.broadcasted_iota(jnp.int32, outs[0].shape, 1) < HEAD_DIM
    return jnp.concatenate(
        [jnp.where(low, outs[0], pltpu.roll(outs[1], HEAD_DIM, 1)),
         jnp.where(low, outs[2], pltpu.roll(outs[3], HEAD_DIM, 1))], axis=1)


def _cmp_kernel(q_ref, kc_ref, vc_ref, sig_ref, selt_ref, o_ref, bias_ref, psum_ref, acc_ref):
    tq = q_ref.shape[1]
    ncp = kc_ref.shape[2]
    t0 = pl.program_id(2) * tq
    q_heads = _heads_t(q_ref[0])

    def attend(nk):
        kc = kc_ref[0, 0, 0:nk, :]
        vc_t = vc_ref[0, 0, :, 0:nk]
        t = t0 + lax.broadcasted_iota(jnp.int32, (nk, tq), 1)
        n = lax.broadcasted_iota(jnp.int32, (nk, tq), 0)
        cmask = n * CMP_STRIDE + (CMP_LEN - 1) <= t
        for r, q_t in enumerate(q_heads):
            s = jnp.where(cmask, _dot(kc, q_t), -jnp.inf)
            m = jnp.max(s, axis=0, keepdims=True)
            m = jnp.where(m == -jnp.inf, 0.0, m)
            e = jnp.exp2(s - m)
            p = e * (1.0 / jnp.maximum(jnp.sum(e, axis=0, keepdims=True), TINY))
            psum_ref[0:nk, :] = p if r == 0 else psum_ref[0:nk, :] + p
            acc_ref[r] = _dot(vc_t, e.astype(BF16))
        if nk < ncp:
            psum_ref[nk:ncp, :] = jnp.zeros((ncp - nk, tq), F32)

    last_block = (t0 + tq - CMP_LEN) // CMP_STRIDE
    n_chunks = jnp.minimum(last_block // CMP_CHUNK + 1, ncp // CMP_CHUNK)
    for v in range(1, ncp // CMP_CHUNK + 1):
        pl.when(n_chunks == v)(functools.partial(attend, v * CMP_CHUNK))
    o_ref[0] = _finish_heads([acc_ref[r] for r in range(NSA_GROUP)], sig_ref[0], 0).astype(BF16)

    psum = psum_ref[...]
    ps_hi = psum.astype(BF16)
    ps_lo = (psum - ps_hi.astype(F32)).astype(BF16)
    selt = selt_ref[...]
    imp = _dot(selt, ps_hi) + _dot(selt, ps_lo)
    j = lax.broadcasted_iota(jnp.int32, (NS_PAD, tq), 0)
    cur = (t0 + lax.broadcasted_iota(jnp.int32, (NS_PAD, tq), 1)) >> 6
    forced = (j == 0) | (j == cur) | (j == cur - 1)
    valid = j <= cur
    vals = jnp.where(forced, -jnp.inf, jnp.where(valid, imp, -1.0))
    neg_j = -j.astype(F32)
    for _ in range(SEL_TOPK - N_FORCED):
        m = jnp.max(vals, axis=0, keepdims=True)
        first = jnp.max(jnp.where(vals == m, neg_j, -jnp.inf), axis=0, keepdims=True)
        vals = jnp.where(neg_j == first, -jnp.inf, vals)
    bias_t = jnp.where((vals == -jnp.inf) & valid, 0.0, MASK_BIAS)
    bias_ref[0, 0] = bias_t.astype(BF16)


def _cmp_attention(q3, kc, vc, sig3, selt, tq):
    b, s, _ = q3.shape
    ncp = kc.shape[2]
    assert ncp % CMP_CHUNK == 0 and tq >= CMP_LEN
    grid = (b, NSA_KV_HEADS, s // tq)
    return pl.pallas_call(
        _cmp_kernel,
        grid=grid,
        in_specs=[pl.BlockSpec((1, tq, 256), lambda bi, g, i: (bi, i, g)),
                  pl.BlockSpec((1, 1, ncp, LANES), lambda bi, g, i: (bi, g, 0, 0)),
                  pl.BlockSpec((1, 1, LANES, ncp), lambda bi, g, i: (bi, g, 0, 0)),
                  pl.BlockSpec((1, tq, LANES), lambda bi, g, i: (bi, i, g)),
                  pl.BlockSpec(selt.shape, lambda bi, g, i: (0, 0))],
        out_specs=[pl.BlockSpec((1, tq, 256), lambda bi, g, i: (bi, i, g)),
                   pl.BlockSpec((1, 1, NS_PAD, tq), lambda bi, g, i: (bi, g, 0, i))],
        out_shape=[jax.ShapeDtypeStruct((b, s, NSA_WIDTH), BF16),
                   jax.ShapeDtypeStruct((b, NSA_KV_HEADS, NS_PAD, s), BF16)],
        scratch_shapes=[pltpu.VMEM((ncp, tq), F32), pltpu.VMEM((NSA_GROUP, LANES, tq), F32)],
        compiler_params=_vmem_params(("parallel", "parallel", "parallel")),
    )(q3, kc, vc, sig3, selt)


def _sel_kernel(q_ref, bias_ref, et_ref, ks_ref, vs_ref, sig_ref, oc_ref, ow_ref, zn_ref, o_ref,
                qaug_ref, s_ref, m_ref, acc_ref):
    tq = q_ref.shape[1]
    tk = vs_ref.shape[4]
    cols = NSA_GROUP * tq
    qt = pl.program_id(2)

    bias = bias_ref[0, 0]
    for r, q_t in enumerate(_heads_t(q_ref[0])):
        qaug_ref[0:LANES, r * tq:(r + 1) * tq] = bias
        qaug_ref[LANES:2 * LANES, r * tq:(r + 1) * tq] = q_t
    m_ref[...] = jnp.full(m_ref.shape, -jnp.inf, F32)
    acc_ref[...] = jnp.zeros(acc_ref.shape, F32)
    heads = [slice(r * tq, (r + 1) * tq) for r in range(NSA_GROUP)]

    def keys(kt):
        k0 = pl.multiple_of(kt * tk, tk)
        return jnp.concatenate([et_ref[pl.ds(k0, tk), :], ks_ref[0, 0, pl.ds(k0, tk), :]], axis=1)

    def softmax_pv(kt, h, s):
        m_prev = m_ref[:, h]
        m_new = jnp.maximum(m_prev, jnp.max(s, axis=0, keepdims=True))
        p = jnp.exp2(s - m_new).astype(BF16)
        acc_ref[:, h] = jnp.exp2(m_prev - m_new) * acc_ref[:, h] + _dot(vs_ref[0, 0, kt], p)
        m_ref[:, h] = m_new

    k_first = keys(0)
    for h in heads:
        s_ref[:, h] = _dot(k_first, qaug_ref[:, h])

    def step(kt):
        k_next = keys(kt + 1)
        for h in heads:
            s = s_ref[:, h]
            s_next = _dot(k_next, qaug_ref[:, h])
            softmax_pv(kt, h, s)
            s_ref[:, h] = s_next

    def one_step(kt, carry):
        step(kt)
        return carry

    def unrolled_steps(j, first):
        for u in range(SEL_UNROLL):
            step(first + SEL_UNROLL * j + u)
        return first

    last = (qt * tq) // tk
    rest = last % SEL_UNROLL
    lax.fori_loop(0, rest, one_step, 0)
    lax.fori_loop(0, last // SEL_UNROLL, unrolled_steps, rest)
    kpos = last * tk + lax.broadcasted_iota(jnp.int32, (tk, tq), 0)
    causal = kpos <= qt * tq + lax.broadcasted_iota(jnp.int32, (tk, tq), 1)
    for h in heads:
        softmax_pv(last, h, jnp.where(causal, s_ref[:, h], MASK_BIAS))

    acc = acc_ref[...]
    o_sel = _finish_heads([acc[:, h] for h in heads], sig_ref[0], 1)
    o_nsa = oc_ref[0].astype(F32) + o_sel + ow_ref[0].astype(F32)
    o_ref[0] = (o_nsa * zn_ref[0].astype(F32)).astype(BF16)


def _sel_attention(q3, bias, et, ks, vst, sig3, o_c, o_w, zn3, tq):
    b, s, _ = q3.shape
    tk = vst.shape[4]
    cols = NSA_GROUP * tq
    group_blk = pl.BlockSpec((1, tq, 256), lambda bi, g, i: (bi, i, g))
    return pl.pallas_call(
        _sel_kernel,
        grid=(b, NSA_KV_HEADS, s // tq),
        in_specs=[group_blk,
                  pl.BlockSpec((1, 1, NS_PAD, tq), lambda bi, g, i: (bi, g, 0, i)),
                  pl.BlockSpec((s, LANES), lambda bi, g, i: (0, 0)),
                  pl.BlockSpec((1, 1, s, LANES), lambda bi, g, i: (g, bi, 0, 0)),
                  pl.BlockSpec((1, 1, s // tk, LANES, tk), lambda bi, g, i: (g, bi, 0, 0, 0)),
                  pl.BlockSpec((1, tq, LANES), lambda bi, g, i: (bi, i, g)),
                  group_blk, group_blk, group_blk],
        out_specs=group_blk,
        out_shape=jax.ShapeDtypeStruct((b, s, NSA_WIDTH), BF16),
        scratch_shapes=[pltpu.VMEM((2 * LANES, cols), BF16), pltpu.VMEM((tk, cols), F32),
                        pltpu.VMEM((1, cols), F32), pltpu.VMEM((LANES, cols), F32)],
        compiler_params=_vmem_params(("parallel", "parallel", "parallel")),
    )(q3, bias, et, ks, vst, sig3, o_c, o_w, zn3)


def _win_kernel(q_ref, kp_ref, kc_ref, vp_ref, vc_ref, sig_ref, o_ref):
    tq = q_ref.shape[1]
    qt = pl.program_id(2)
    c = lax.broadcasted_iota(jnp.int32, (tq, tq), 0)
    i = lax.broadcasted_iota(jnp.int32, (tq, tq), 1)
    mask_cur = c <= i
    mask_prev = c > i + jnp.where(qt > 0, 0, tq)
    accs = []
    for q_t in _heads_t(q_ref[0]):
        s_p = jnp.where(mask_prev, _dot(kp_ref[0, 0], q_t), -jnp.inf)
        s_c = jnp.where(mask_cur, _dot(kc_ref[0, 0], q_t), -jnp.inf)
        m = jnp.maximum(jnp.max(s_p, axis=0, keepdims=True), jnp.max(s_c, axis=0, keepdims=True))
        e_p = jnp.exp2(s_p - m).astype(BF16)
        e_c = jnp.exp2(s_c - m).astype(BF16)
        accs.append(_dot(vp_ref[0, 0], e_p) + _dot(vc_ref[0, 0], e_c))
    o_ref[0] = _finish_heads(accs, sig_ref[0], 2).astype(BF16)


def _win_attention(q3, kw, vw, sig3, tq):
    b, s, _ = q3.shape
    assert tq == WINDOW
    nt = s // tq
    kspec = lambda back: pl.BlockSpec((1, 1, tq, LANES),
                                      lambda bi, g, i: (g, bi, jnp.maximum(i - back, 0), 0))
    vspec = lambda back: pl.BlockSpec((1, 1, LANES, tq),
                                      lambda bi, g, i: (g, bi * nt + jnp.maximum(i - back, 0), 0, 0))
    return pl.pallas_call(
        _win_kernel,
        grid=(b, NSA_KV_HEADS, nt),
        in_specs=[pl.BlockSpec((1, tq, 256), lambda bi, g, i: (bi, i, g)), kspec(1), kspec(0),
                  vspec(1), vspec(0), pl.BlockSpec((1, tq, LANES), lambda bi, g, i: (bi, i, g))],
        out_specs=pl.BlockSpec((1, tq, 256), lambda bi, g, i: (bi, i, g)),
        out_shape=jax.ShapeDtypeStruct((b, s, NSA_WIDTH), BF16),
        compiler_params=_vmem_params(("parallel", "parallel", "parallel")),
    )(q3, kw, kw, vw, vw, sig3)


def _mem_kv_kernel(mem_ref, gm_ref, w_ref, bd256_ref, gk_ref, mk_ref, mv_ref):
    x = mem_ref[0]
    h = x * lax.rsqrt(jnp.mean(x * x, axis=-1, keepdims=True) + EPS) * gm_ref[...]
    mkv = _dot(h.astype(BF16), w_ref[...])
    mk_ref[0] = _head_rms(mkv[:, 0:MEM_WIDTH], bd256_ref[...], gk_ref[...]).astype(BF16)
    for hp in range(MEM_HEADS // 2):
        lo = MEM_WIDTH + hp * LANES
        mv_ref[0, 2 * hp], mv_ref[0, 2 * hp + 1] = _values_t(mkv[:, lo:lo + LANES])


def _mem_kv(mem, gm, w, bd256, gk):
    b, m, _ = mem.shape
    full = lambda a: pl.BlockSpec(a.shape, lambda i: (0,) * a.ndim)
    return pl.pallas_call(
        _mem_kv_kernel,
        grid=(b,),
        in_specs=[pl.BlockSpec((1, m, D_MODEL), lambda i: (i, 0, 0)), full(gm), full(w), full(bd256), full(gk)],
        out_specs=[pl.BlockSpec((1, m, MEM_WIDTH), lambda i: (i, 0, 0)),
                   pl.BlockSpec((1, MEM_HEADS, LANES, m), lambda i: (i, 0, 0, 0))],
        out_shape=[jax.ShapeDtypeStruct((b, m, MEM_WIDTH), BF16),
                   jax.ShapeDtypeStruct((b, MEM_HEADS, LANES, m), BF16)],
        compiler_params=_vmem_params(("parallel",)),
    )(mem, gm, w, bd256, gk)


def _mem_attention_block(q_blk, mk, mv_t):
    accs = []
    for h, q_t in enumerate(_heads_t(q_blk)):
        hp = h // 2
        s = _dot(mk[:, hp * LANES:(hp + 1) * LANES], q_t)
        e = jnp.exp2(s - jnp.max(s, axis=0, keepdims=True))
        accs.append(_dot(mv_t[h], e.astype(BF16)))
    return _finish_heads(accs, None, 0)


def _out_kernel(seq_len, x_ref, ynsa_ref, vp_ref, halo_ref, zp_ref, qm_ref, mk_ref, mv_ref, zm_ref, wo_ref,
                wpool_ref, pscale_ref, o_ref):
    tm = x_ref.shape[0]
    t0 = (pl.program_id(0) * tm) % seq_len

    v = vp_ref[...]
    halo = halo_ref[...] * (t0 > 0).astype(F32)
    e = jnp.concatenate([halo, v], axis=0)
    w2 = e + pltpu.roll(e, 1, 0)
    w4 = w2 + pltpu.roll(w2, 2, 0)
    w8 = w4 + pltpu.roll(w4, 4, 0)
    w16 = w8 + pltpu.roll(w8, 8, 0)
    shape = (tm, POOL_WIDTH)
    grp = lax.broadcasted_iota(jnp.int32, shape, 1) >> 6
    wsum = jnp.where(grp == 0, w2[POOL_HALO:],
                     jnp.where(grp == 1, w4[POOL_HALO:], jnp.where(grp == 2, w8[POOL_HALO:], w16[POOL_HALO:])))
    width = jnp.left_shift(2, grp)
    cnt = jnp.minimum(t0 + lax.broadcasted_iota(jnp.int32, shape, 0) + 1, width).astype(F32)
    pooled = wsum / cnt - v
    y_pool = _dot(pooled.astype(BF16), wpool_ref[...]) * pscale_ref[...] * zp_ref[...].astype(F32)

    y_mem = _mem_attention_block(qm_ref[...], mk_ref[0], mv_ref[0]) * zm_ref[...].astype(F32)
    out = x_ref[...] + _dot(ynsa_ref[...], wo_ref[0:512, :])
    out = out + _dot(y_pool.astype(BF16), wo_ref[512:768, :])
    out = out + _dot(y_mem.astype(BF16), wo_ref[768:1024, :])
    o_ref[...] = out


def _out_proj(x2, ynsa, vp, zp, qm, mk, mv_t, zm, wo, wpool, pscale, seq_len, tm):
    n = x2.shape[0]
    row = lambda w: pl.BlockSpec((tm, w), lambda i: (i, 0))
    full = lambda a: pl.BlockSpec(a.shape, lambda i: (0,) * a.ndim)
    per_batch = lambda a: pl.BlockSpec((1,) + a.shape[1:],
                                       lambda i: ((i * tm) // seq_len,) + (0,) * (a.ndim - 1))
    hb = tm // POOL_HALO
    halo = pl.BlockSpec((POOL_HALO, POOL_WIDTH), lambda i: (jnp.maximum(i * hb - 1, 0), 0))
    return pl.pallas_call(
        functools.partial(_out_kernel, seq_len),
        grid=(n // tm,),
        in_specs=[row(D_MODEL), row(512), row(256), halo, row(256),
                  row(256), per_batch(mk), per_batch(mv_t), row(256), full(wo), full(wpool), full(pscale)],
        out_specs=row(D_MODEL),
        out_shape=jax.ShapeDtypeStruct((n, D_MODEL), F32),
        compiler_params=_vmem_params(("parallel",)),
    )(x2, ynsa, vp, vp, zp, qm, mk, mv_t, zm, wo, wpool, pscale)


def _block_diag_ones(width):
    i = np.arange(width) // HEAD_DIM
    return jnp.asarray((i[:, None] == i[None, :]).astype(np.float32), dtype=BF16)


def _pack_w_in(w_in):
    q, kv, gate, zn, vp, zp, qm, zm = jnp.split(
        w_in, np.cumsum([512, 768, 24, 512, 256, 256, 256, 256])[:-1].tolist(), axis=1)
    cols = np.zeros((NSA_KV_HEADS, LANES), np.int64)
    used = np.zeros((NSA_KV_HEADS, LANES), bool)
    for g in range(NSA_KV_HEADS):
        for k in range(N_BRANCH):
            for r in range(NSA_GROUP):
                cols[g, k * NSA_GROUP + r] = (g * NSA_GROUP + r) * N_BRANCH + k
                used[g, k * NSA_GROUP + r] = True
    gate_p = jnp.where(jnp.asarray(used.reshape(-1))[None, :], gate[:, cols.reshape(-1)], 0.0)
    return jnp.concatenate([q, kv, gate_p, zn, vp, zp, qm, zm], axis=1).astype(BF16)


def _pack_cmp(pos, w1, w2):
    half = CMP_LEN // 2

    def w1_half(wh):
        z = jnp.zeros((half, NSA_KV_HEADS, HEAD_DIM, NSA_KV_HEADS, HEAD_DIM), F32)
        for g in range(NSA_KV_HEADS):
            z = z.at[:, g, :, g, :].set(wh)
        return z.reshape(half * KV_WIDTH, KV_WIDTH).astype(BF16)

    def pos_half(ph):
        return jnp.tile(ph[:, None, :], (1, NSA_KV_HEADS, 1)).reshape(1, half * KV_WIDTH)

    w2_bd = jnp.zeros((NSA_KV_HEADS, HEAD_DIM, NSA_KV_HEADS, HEAD_DIM), F32)
    for g in range(NSA_KV_HEADS):
        w2_bd = w2_bd.at[g, :, g, :].set(w2)
    return (pos_half(pos[:half]), pos_half(pos[half:]), w1_half(w1[:half]), w1_half(w1[half:]),
            w2_bd.reshape(KV_WIDTH, KV_WIDTH).astype(BF16))


def _sel_map_t(ncp, s):
    nc = (s - CMP_LEN) // CMP_STRIDE + 1
    ns = s // SEL_BLOCK
    c0 = np.arange(nc) * CMP_STRIDE
    c1 = c0 + CMP_LEN
    s0 = np.arange(ns) * SEL_BLOCK
    s1 = s0 + SEL_BLOCK
    ov = np.clip(np.minimum(c1[:, None], s1[None, :]) - np.maximum(c0[:, None], s0[None, :]), 0, None)
    m = np.zeros((NS_PAD, ncp), np.float32)
    m[:ns, :nc] = (ov / CMP_LEN).T
    return jnp.asarray(m, dtype=BF16)


def kernel(x, mem, positions, g_norm, w_in, g_q_nsa, g_k_cmp, g_k_slc, g_k_win, cmp_pos_k, w_cmp_k1,
           w_cmp_k2, cmp_pos_v, w_cmp_v1, w_cmp_v2, w_pool, pool_scale, g_mem, w_mem_kv, g_q_mem,
           g_k_mem, w_out):
    b, s, d = x.shape
    depth = g_norm.shape[0]
    assert d == D_MODEL and s % WINDOW == 0 and s // SEL_BLOCK >= SEL_TOPK and s // SEL_BLOCK <= NS_PAD
    n = b * s
    ncp = s // CMP_STRIDE
    tm = 512
    t_cmp = 256
    t_sel = 256
    t_win = WINDOW
    assert s % tm == 0 and tm % t_sel == 0 and tm == WINDOW

    inv_freq = ROPE_THETA ** (-jnp.arange(ROPE_HALF, dtype=F32) / ROPE_HALF)
    invf = inv_freq[(np.arange(LANES) % HEAD_DIM) % ROPE_HALF][None, :]
    bd512, bd256, bd128 = _block_diag_ones(512), _block_diag_ones(256), _block_diag_ones(128)
    cs = _rope_cs(positions.reshape(1, n), inv_freq[:, None], min(n, 8192))
    rope_e = _rope_expanders()
    end_idx = np.minimum(np.arange(ncp) * CMP_STRIDE + CMP_LEN - 1, s - 1)
    pend = positions[:, end_idx][:, :, None]
    selt = _sel_map_t(ncp, s)
    et = jnp.asarray((np.arange(s)[:, None] // SEL_BLOCK == np.arange(NS_PAD)[None, :]).astype(np.float32),
                     dtype=BF16)
    tile = lambda g, reps: jnp.tile(g[None, :], (1, reps))

    for l in range(depth):
        x2 = x.reshape(n, d)
        w_all = _pack_w_in(w_in[l])
        (q, kc_raw, vc_raw, ks, vs, kw, vw, sig, zn, vp, zp, qm, zm) = _in_proj(
            x2, cs, g_norm[l][None, :], w_all, bd512, bd256, tile(g_q_nsa[l], 8), tile(g_k_slc[l], 2),
            tile(g_k_win[l], 2), tile(g_q_mem[l], 4), rope_e, tm, t_win)

        cmp_consts = (_pack_cmp(cmp_pos_k[l], w_cmp_k1[l], w_cmp_k2[l])
                      + _pack_cmp(cmp_pos_v[l], w_cmp_v1[l], w_cmp_v2[l].T)
                      + (bd128, tile(g_k_cmp[l], 2), invf))
        kc, vc = _compress(kc_raw.reshape(b, ncp, CMP_STRIDE * KV_WIDTH),
                           vc_raw.reshape(b, ncp, CMP_STRIDE * KV_WIDTH), pend, cmp_consts)

        q3 = q.reshape(b, s, NSA_WIDTH)
        sig3 = sig.reshape(b, s, 256)
        o_c, bias = _cmp_attention(q3, kc, vc, sig3, selt, t_cmp)
        dup4 = lambda a: a.reshape(NSA_KV_HEADS, b, s, LANES)
        vt5 = lambda a: a.reshape(NSA_KV_HEADS, b, s // tm, LANES, tm)
        o_w = _win_attention(q3, dup4(kw), vw, sig3, t_win)
        y_nsa = _sel_attention(q3, bias, et, dup4(ks), vt5(vs), sig3, o_c, o_w,
                               zn.reshape(b, s, NSA_WIDTH), t_sel)

        mk, mv_t = _mem_kv(mem, g_mem[l][None, :], w_mem_kv[l].astype(BF16), bd256, tile(g_k_mem[l], 4))

        wpool = jnp.zeros((4, HEAD_DIM, 4, HEAD_DIM), F32)
        for g in range(4):
            wpool = wpool.at[g, :, g, :].set(w_pool[l, g])
        out = _out_proj(x2, y_nsa.reshape(n, NSA_WIDTH), vp, zp, qm, mk, mv_t, zm, w_out[l].astype(BF16),
                        wpool.reshape(POOL_WIDTH, POOL_WIDTH).astype(BF16), pool_scale[l][None, :], s, tm)
        x = out.reshape(b, s, d)
    return x
```

```python
import functools

import numpy as np
import jax
import jax.numpy as jnp
from jax import lax
from jax.experimental import pallas as pl
from jax.experimental.pallas import tpu as pltpu

F32 = jnp.float32
BF16 = jnp.bfloat16

D_MODEL = 1024
HEAD_DIM = 64
NSA_HEADS = 8
NSA_KV_HEADS = 2
NSA_GROUP = NSA_HEADS // NSA_KV_HEADS
NSA_WIDTH = NSA_HEADS * HEAD_DIM
KV_WIDTH = NSA_KV_HEADS * HEAD_DIM
N_BRANCH = 3
CMP_LEN = 32
CMP_STRIDE = 16
SEL_BLOCK = 64
SEL_TOPK = 16
WINDOW = 512
FORCE_SCORE = 1e4
N_FORCED = 3
POOL_WINDOWS = (2, 4, 8, 16)
POOL_WIDTH = 256
MEM_HEADS = 4
MEM_WIDTH = MEM_HEADS * HEAD_DIM
ROPE_THETA = 500000.0
ROPE_DIM = HEAD_DIM // 4
ROPE_HALF = ROPE_DIM // 2
EPS = 1e-6
SCALE = HEAD_DIM ** -0.5
Q_SCALE = SCALE * float(np.log2(np.e))

LANES = 128
NS_PAD = LANES
MASK_BIAS = -1e9
POOL_HALO = 16
CMP_CHUNK = LANES
SEL_UNROLL = 4
ROPE_ROWS = 32
TINY = float(np.finfo(np.float32).tiny)

_SEG = dict(q=(0, 512), kv=(512, 768), gate=(1280, 256), zn=(1536, 512), vp=(2048, 256),
            zp=(2304, 256), qm=(2560, 256), zm=(2816, 256))
W_ALL_WIDTH = 3072

_NT = (((1,), (1,)), ((), ()))


def _dot(a, b):
    return jnp.dot(a, b, preferred_element_type=F32)


def _dot_nt(a, b):
    return lax.dot_general(a, b, _NT, preferred_element_type=F32)


def _vmem_params(sem):
    return pltpu.CompilerParams(dimension_semantics=sem, vmem_limit_bytes=56 * 1024 * 1024)


def _rope_tables(pos_f32, invf):
    ang = pos_f32 * invf
    c, s = jnp.cos(ang), jnp.sin(ang)
    d = lax.broadcasted_iota(jnp.int32, ang.shape, 1) % HEAD_DIM
    tc = jnp.where(d < ROPE_DIM, c, 1.0)
    ts_up = jnp.where((d >= ROPE_HALF) & (d < ROPE_DIM), s, 0.0)
    ts_dn = jnp.where(d < ROPE_HALF, -s, 0.0)
    return tc, ts_up, ts_dn


def _rope(x, tables):
    tc, ts_up, ts_dn = tables
    w = x.shape[1]
    reps = w // LANES
    if reps > 1:
        tc, ts_up, ts_dn = (jnp.concatenate([t] * reps, axis=1) for t in (tc, ts_up, ts_dn))
    return (x * tc + pltpu.roll(x, ROPE_HALF, 1) * ts_up
            + pltpu.roll(x, w - ROPE_HALF, 1) * ts_dn)


def _head_rms(x, ones_bd, gain):
    ss = _dot((x * x).astype(BF16), ones_bd)
    return x * lax.rsqrt(ss * (1.0 / HEAD_DIM) + EPS) * gain


def _dup_halves(x):
    low = lax.broadcasted_iota(jnp.int32, x.shape, 1) < HEAD_DIM
    r = pltpu.roll(x, HEAD_DIM, 1)
    return jnp.where(low, x, r), jnp.where(low, r, x)


def _values_t(v):
    return _values_rows(v.T)


def _values_rows(vt):
    vt = vt.astype(BF16)
    ones = jnp.ones((HEAD_DIM, vt.shape[1]), BF16)
    return (jnp.concatenate([vt[0:HEAD_DIM], ones], axis=0),
            jnp.concatenate([vt[HEAD_DIM:], ones], axis=0))


def _silu(z):
    return z * jax.nn.sigmoid(z)


def _in_proj_kernel(x_ref, cs_ref, gn_ref, w_ref, bd512_ref, bd256_ref, gq_ref, gks_ref, gkw_ref,
                    gqm_ref, rope_e_ref,
                    q_ref, kc_ref, vc_ref, ks_ref, vs_ref, kw_ref, vw_ref, sig_ref, zn_ref, vp_ref,
                    zp_ref, qm_ref, zm_ref):
    x = x_ref[...]
    h = x * lax.rsqrt(jnp.mean(x * x, axis=-1, keepdims=True) + EPS) * gn_ref[...]
    hb = h.astype(BF16)

    def seg(name):
        o, w = _SEG[name]
        return _dot(hb, w_ref[:, o:o + w])

    cs = cs_ref[...]
    cs_hi = cs.astype(BF16)
    cs_lo = (cs - cs_hi.astype(F32)).astype(BF16)
    tables = tuple((_dot(rope_e_ref[k], cs_hi) + _dot(rope_e_ref[k], cs_lo)).T for k in range(3))

    q = _head_rms(seg('q'), bd512_ref[...], gq_ref[...])
    q_ref[...] = (_rope(q, tables) * Q_SCALE).astype(BF16)

    kv = seg('kv')
    kc_ref[...] = kv[:, 0:128]
    vc_ref[...] = kv[:, 128:256]

    def norm_dup(k, gain):
        k = k * lax.rsqrt(jnp.mean(k * k, axis=-1, keepdims=True) + EPS) * gain
        return _rope(k, tables).astype(BF16)

    ks0, ks1 = _dup_halves(kv[:, 256:384])
    ks_ref[0] = norm_dup(ks0, gks_ref[...])
    ks_ref[1] = norm_dup(ks1, gks_ref[...])
    vs_ref[0, 0], vs_ref[1, 0] = _values_t(kv[:, 384:512])
    kw0, kw1 = _dup_halves(kv[:, 512:640])
    kw_ref[0] = norm_dup(kw0, gkw_ref[...])
    kw_ref[1] = norm_dup(kw1, gkw_ref[...])
    vw = _values_t(kv[:, 640:768])
    wt = vw_ref.shape[3]
    for g in range(NSA_KV_HEADS):
        for c in range(vw_ref.shape[1]):
            vw_ref[g, c] = vw[g][:, c * wt:(c + 1) * wt]

    sig_ref[...] = jax.nn.sigmoid(seg('gate'))
    zn_ref[...] = _silu(seg('zn')).astype(BF16)
    vp_ref[...] = seg('vp')
    zp_ref[...] = _silu(seg('zp')).astype(BF16)
    qm = _head_rms(seg('qm'), bd256_ref[...], gqm_ref[...])
    qm_ref[...] = (qm * Q_SCALE).astype(BF16)
    zm_ref[...] = _silu(seg('zm')).astype(BF16)


def _rope_cs_kernel(pos_ref, invf_ref, o_ref):
    ang = invf_ref[...] * pos_ref[...].astype(F32)
    tail = lax.broadcasted_iota(jnp.int32, (ROPE_ROWS - 2 * ROPE_HALF, ang.shape[1]), 0) == 0
    o_ref[...] = jnp.concatenate([jnp.cos(ang), jnp.sin(ang), tail.astype(F32)], axis=0)


def _rope_cs(pos_row, invf_col, blk):
    n = pos_row.shape[1]
    return pl.pallas_call(
        _rope_cs_kernel,
        grid=(n // blk,),
        in_specs=[pl.BlockSpec((1, blk), lambda i: (0, i)), pl.BlockSpec(invf_col.shape, lambda i: (0, 0))],
        out_specs=pl.BlockSpec((ROPE_ROWS, blk), lambda i: (0, i)),
        out_shape=jax.ShapeDtypeStruct((ROPE_ROWS, n), F32),
        compiler_params=_vmem_params(("parallel",)),
    )(pos_row, invf_col)


def _rope_expanders():
    e = np.zeros((3, LANES, ROPE_ROWS), np.float32)
    for lane in range(LANES):
        d = lane % HEAD_DIM
        if d < ROPE_DIM:
            e[0, lane, d % ROPE_HALF] = 1.0
        else:
            e[0, lane, 2 * ROPE_HALF] = 1.0
        if ROPE_HALF <= d < ROPE_DIM:
            e[1, lane, ROPE_HALF + d % ROPE_HALF] = 1.0
        if d < ROPE_HALF:
            e[2, lane, ROPE_HALF + d] = -1.0
    return jnp.asarray(e, dtype=BF16)


def _in_proj(x2, cs, gn, w_all, bd512, bd256, gq, gks, gkw, gqm, rope_e, tm, t_win):
    n = x2.shape[0]
    row = lambda w: pl.BlockSpec((tm, w), lambda i: (i, 0))
    full = lambda a: pl.BlockSpec(a.shape, lambda i: (0,) * a.ndim)
    dup = pl.BlockSpec((NSA_KV_HEADS, tm, LANES), lambda i: (0, i, 0))
    sd = jax.ShapeDtypeStruct
    vst = pl.BlockSpec((NSA_KV_HEADS, 1, LANES, tm), lambda i: (0, i, 0, 0))
    vwt = pl.BlockSpec((NSA_KV_HEADS, tm // t_win, LANES, t_win), lambda i: (0, i, 0, 0))
    out_shape = [sd((n, 512), BF16), sd((n, 128), F32), sd((n, 128), F32),
                 sd((2, n, 128), BF16), sd((2, n // tm, LANES, tm), BF16), sd((2, n, 128), BF16),
                 sd((2, n // t_win, LANES, t_win), BF16), sd((n, 256), F32), sd((n, 512), BF16), sd((n, 256), F32),
                 sd((n, 256), BF16), sd((n, 256), BF16), sd((n, 256), BF16)]
    out_specs = [row(512), row(128), row(128), dup, vst, dup, vwt, row(256), row(512), row(256),
                 row(256), row(256), row(256)]
    consts = (gn, w_all, bd512, bd256, gq, gks, gkw, gqm, rope_e)
    return pl.pallas_call(
        _in_proj_kernel,
        grid=(n // tm,),
        in_specs=[row(D_MODEL), pl.BlockSpec((ROPE_ROWS, tm), lambda i: (0, i))] + [full(a) for a in consts],
        out_specs=out_specs,
        out_shape=out_shape,
        compiler_params=_vmem_params(("parallel",)),
    )(x2, cs, *consts)


def _compress_kernel(zk_ref, zv_ref, pend_ref, pak_ref, pbk_ref, w1ak_ref, w1bk_ref, w2k_ref,
                     pav_ref, pbv_ref, w1av_ref, w1bv_ref, w2v_ref, bd128_ref, gk_ref, invf_ref,
                     kc_ref, vc_ref):
    ncp = zk_ref.shape[1] // CMP_STRIDE

    def mlp(z_ref, pa, pb, w1a, w1b, w2, transposed=False):
        a = jnp.zeros((ncp, KV_WIDTH), F32)
        b = jnp.zeros((ncp, KV_WIDTH), F32)
        for o in range(CMP_STRIDE):
            z = z_ref[0, pl.ds(o, ncp, stride=CMP_STRIDE), :]
            a = a + _dot((z + pa[o:o + 1, :]).astype(BF16), w1a[o])
            b = b + _dot((z + pb[o:o + 1, :]).astype(BF16), w1b[o])
        pre = a + pltpu.roll(b, ncp - 1, 0)
        hid = jax.nn.gelu(pre).astype(BF16)
        return _dot_nt(w2, hid) if transposed else _dot(hid, w2)

    kc = mlp(zk_ref, pak_ref, pbk_ref, w1ak_ref, w1bk_ref, w2k_ref[...])
    kc = _head_rms(kc, bd128_ref[...], gk_ref[...])
    kc = _rope(kc, _rope_tables(pend_ref[0].astype(F32), invf_ref[...]))
    k0, k1 = _dup_halves(kc)
    kc_ref[0, 0] = k0.astype(BF16)
    kc_ref[0, 1] = k1.astype(BF16)
    vc_t = mlp(zv_ref, pav_ref, pbv_ref, w1av_ref, w1bv_ref, w2v_ref[...], True)
    vc_ref[0, 0], vc_ref[0, 1] = _values_rows(vc_t)


def _compress(zk, zv, pend, consts):
    b, s, zw = zk.shape
    ncp = s // CMP_STRIDE
    full = lambda a: pl.BlockSpec(a.shape, lambda i: (0,) * a.ndim)
    zspec = pl.BlockSpec((1, s, zw), lambda i: (i, 0, 0))
    ospec = pl.BlockSpec((1, NSA_KV_HEADS, ncp, LANES), lambda i: (i, 0, 0, 0))
    sd = jax.ShapeDtypeStruct((b, NSA_KV_HEADS, ncp, LANES), BF16)
    return pl.pallas_call(
        _compress_kernel,
        grid=(b,),
        in_specs=[zspec, zspec, pl.BlockSpec((1, ncp, 1), lambda i: (i, 0, 0))] + [full(a) for a in consts],
        out_specs=[ospec, pl.BlockSpec((1, NSA_KV_HEADS, LANES, ncp), lambda i: (i, 0, 0, 0))],
        out_shape=[sd, jax.ShapeDtypeStruct((b, NSA_KV_HEADS, LANES, ncp), BF16)],
        compiler_params=_vmem_params(("parallel",)),
    )(zk, zv, pend, *consts)


def _gate_col(sig_blk, r, branch):
    c = branch * NSA_GROUP + r
    return sig_blk[:, c:c + 1]


def _heads_t(q_blk):
    out = []
    for rp in range(NSA_GROUP // 2):
        pair_t = q_blk[:, rp * LANES:(rp + 1) * LANES].astype(F32).T
        low = lax.broadcasted_iota(jnp.int32, pair_t.shape, 0) < HEAD_DIM
        out.append(jnp.where(low, pair_t, 0.0).astype(BF16))
        out.append(jnp.where(low, 0.0, pair_t).astype(BF16))
    return out


def _finish_heads(accs, sig, branch):
    outs = []
    for r, acc in enumerate(accs):
        a = acc.T
        o = a / jnp.maximum(pltpu.roll(a, HEAD_DIM, 1), TINY)
        outs.append(o if sig is None else o * _gate_col(sig, r, branch))
    low = lax.broadcasted_iota(jnp.int32, outs[0].shape, 1) < HEAD_DIM
    return jnp.concatenate(
        [jnp.where(low, outs[0], pltpu.roll(outs[1], HEAD_DIM, 1)),
         jnp.where(low, outs[2], pltpu.roll(outs[3], HEAD_DIM, 1))], axis=1)


def _cmp_kernel(q_ref, kc_ref, vc_ref, sig_ref, selt_ref, o_ref, bias_ref, psum_ref, acc_ref):
    tq = q_ref.shape[1]
    ncp = kc_ref.shape[2]
    t0 = pl.program_id(2) * tq
    q_heads = _heads_t(q_ref[0])

    def attend(nk):
        kc = kc_ref[0, 0, 0:nk, :]
        vc_t = vc_ref[0, 0, :, 0:nk]
        t = t0 + lax.broadcasted_iota(jnp.int32, (nk, tq), 1)
        n = lax.broadcasted_iota(jnp.int32, (nk, tq), 0)
        cmask = n * CMP_STRIDE + (CMP_LEN - 1) <= t
        for r, q_t in enumerate(q_heads):
            s = jnp.where(cmask, _dot(kc, q_t), -jnp.inf)
            m = jnp.max(s, axis=0, keepdims=True)
            m = jnp.where(m == -jnp.inf, 0.0, m)
            e = jnp.exp2(s - m)
            p = e * (1.0 / jnp.maximum(jnp.sum(e, axis=0, keepdims=True), TINY))
            psum_ref[0:nk, :] = p if r == 0 else psum_ref[0:nk, :] + p
            acc_ref[r] = _dot(vc_t, e.astype(BF16))
        if nk < ncp:
            psum_ref[nk:ncp, :] = jnp.zeros((ncp - nk, tq), F32)

    last_block = (t0 + tq - CMP_LEN) // CMP_STRIDE
    n_chunks = jnp.minimum(last_block // CMP_CHUNK + 1, ncp // CMP_CHUNK)
    for v in range(1, ncp // CMP_CHUNK + 1):
        pl.when(n_chunks == v)(functools.partial(attend, v * CMP_CHUNK))
    o_ref[0] = _finish_heads([acc_ref[r] for r in range(NSA_GROUP)], sig_ref[0], 0).astype(BF16)

    psum = psum_ref[...]
    ps_hi = psum.astype(BF16)
    ps_lo = (psum - ps_hi.astype(F32)).astype(BF16)
    selt = selt_ref[...]
    imp = _dot(selt, ps_hi) + _dot(selt, ps_lo)
    j = lax.broadcasted_iota(jnp.int32, (NS_PAD, tq), 0)
    cur = (t0 + lax.broadcasted_iota(jnp.int32, (NS_PAD, tq), 1)) >> 6
    forced = (j == 0) | (j == cur) | (j == cur - 1)
    valid = j <= cur
    vals = jnp.where(forced, -jnp.inf, jnp.where(valid, imp, -1.0))
    neg_j = -j.astype(F32)
    for _ in range(SEL_TOPK - N_FORCED):
        m = jnp.max(vals, axis=0, keepdims=True)
        first = jnp.max(jnp.where(vals == m, neg_j, -jnp.inf), axis=0, keepdims=True)
        vals = jnp.where(neg_j == first, -jnp.inf, vals)
    bias_t = jnp.where((vals == -jnp.inf) & valid, 0.0, MASK_BIAS)
    bias_ref[0, 0] = bias_t.astype(BF16)


def _cmp_attention(q3, kc, vc, sig3, selt, tq):
    b, s, _ = q3.shape
    ncp = kc.shape[2]
    assert ncp % CMP_CHUNK == 0 and tq % SEL_BLOCK == 0
    grid = (b, NSA_KV_HEADS, s // tq)
    return pl.pallas_call(
        _cmp_kernel,
        grid=grid,
        in_specs=[pl.BlockSpec((1, tq, 256), lambda bi, g, i: (bi, i, g)),
                  pl.BlockSpec((1, 1, ncp, LANES), lambda bi, g, i: (bi, g, 0, 0)),
                  pl.BlockSpec((1, 1, LANES, ncp), lambda bi, g, i: (bi, g, 0, 0)),
                  pl.BlockSpec((1, tq, LANES), lambda bi, g, i: (bi, i, g)),
                  pl.BlockSpec(selt.shape, lambda bi, g, i: (0, 0))],
        out_specs=[pl.BlockSpec((1, tq, 256), lambda bi, g, i: (bi, i, g)),
                   pl.BlockSpec((1, 1, NS_PAD, tq), lambda bi, g, i: (bi, g, 0, i))],
        out_shape=[jax.ShapeDtypeStruct((b, s, NSA_WIDTH), BF16),
                   jax.ShapeDtypeStruct((b, NSA_KV_HEADS, NS_PAD, s), BF16)],
        scratch_shapes=[pltpu.VMEM((ncp, tq), F32), pltpu.VMEM((NSA_GROUP, LANES, tq), F32)],
        compiler_params=_vmem_params(("parallel", "parallel", "parallel")),
    )(q3, kc, vc, sig3, selt)


def _sel_kernel(q_ref, bias_ref, et_ref, ks_ref, vs_ref, sig_ref, oc_ref, ow_ref, zn_ref, o_ref,
                qaug_ref, s_ref, m_ref, acc_ref):
    tq = q_ref.shape[1]
    tk = vs_ref.shape[4]
    cols = NSA_GROUP * tq
    qt = pl.program_id(2)

    bias = bias_ref[0, 0]
    for r, q_t in enumerate(_heads_t(q_ref[0])):
        qaug_ref[0:LANES, r * tq:(r + 1) * tq] = bias
        qaug_ref[LANES:2 * LANES, r * tq:(r + 1) * tq] = q_t
    m_ref[...] = jnp.full(m_ref.shape, -jnp.inf, F32)
    acc_ref[...] = jnp.zeros(acc_ref.shape, F32)
    heads = [slice(r * tq, (r + 1) * tq) for r in range(NSA_GROUP)]

    def keys(kt):
        k0 = pl.multiple_of(kt * tk, tk)
        return jnp.concatenate([et_ref[pl.ds(k0, tk), :], ks_ref[0, 0, pl.ds(k0, tk), :]], axis=1)

    def softmax_pv(kt, h, s):
        rows = s.shape[0]
        m_prev = m_ref[:, h]
        m_new = jnp.maximum(m_prev, jnp.max(s, axis=0, keepdims=True))
        p = jnp.exp2(s - m_new).astype(BF16)
        acc_ref[:, h] = jnp.exp2(m_prev - m_new) * acc_ref[:, h] + _dot(vs_ref[0, 0, kt, :, 0:rows], p)
        m_ref[:, h] = m_new

    k_first = keys(0)
    for h in heads:
        s_ref[:, h] = _dot(k_first, qaug_ref[:, h])

    def step(kt):
        k_next = keys(kt + 1)
        for h in heads:
            s = s_ref[:, h]
            s_next = _dot(k_next, qaug_ref[:, h])
            softmax_pv(kt, h, s)
            s_ref[:, h] = s_next

    def unrolled_steps(j, first):
        for u in range(SEL_UNROLL):
            step(first + SEL_UNROLL * j + u)
        return first

    def two_steps(first):
        step(first)
        step(first + 1)

    last = (qt * tq) // tk
    rest = last % SEL_UNROLL
    pl.when(rest % 2 == 1)(lambda: step(0))
    pl.when(rest >= 2)(functools.partial(two_steps, rest % 2))
    lax.fori_loop(0, last // SEL_UNROLL, unrolled_steps, rest)

    def last_tile(rows):
        kpos = last * tk + lax.broadcasted_iota(jnp.int32, (rows, tq), 0)
        causal = kpos <= qt * tq + lax.broadcasted_iota(jnp.int32, (rows, tq), 1)
        for h in heads:
            softmax_pv(last, h, jnp.where(causal, s_ref[0:rows, h], MASK_BIAS))

    visible = (qt + 1) * tq - last * tk
    for rows in range(tq, tk + 1, tq):
        pl.when(visible == rows)(functools.partial(last_tile, rows))

    acc = acc_ref[...]
    o_sel = _finish_heads([acc[:, h] for h in heads], sig_ref[0], 1)
    o_nsa = oc_ref[0].astype(F32) + o_sel + ow_ref[0].astype(F32)
    o_ref[0] = (o_nsa * zn_ref[0].astype(F32)).astype(BF16)


def _sel_attention(q3, bias, et, ks, vst, sig3, o_c, o_w, zn3, tq):
    b, s, _ = q3.shape
    tk = vst.shape[4]
    cols = NSA_GROUP * tq
    assert SEL_UNROLL == 4 and tk % tq == 0
    group_blk = pl.BlockSpec((1, tq, 256), lambda bi, g, i: (bi, i, g))
    return pl.pallas_call(
        _sel_kernel,
        grid=(b, NSA_KV_HEADS, s // tq),
        in_specs=[group_blk,
                  pl.BlockSpec((1, 1, NS_PAD, tq), lambda bi, g, i: (bi, g, 0, i)),
                  pl.BlockSpec((s, LANES), lambda bi, g, i: (0, 0)),
                  pl.BlockSpec((1, 1, s, LANES), lambda bi, g, i: (g, bi, 0, 0)),
                  pl.BlockSpec((1, 1, s // tk, LANES, tk), lambda bi, g, i: (g, bi, 0, 0, 0)),
                  pl.BlockSpec((1, tq, LANES), lambda bi, g, i: (bi, i, g)),
                  group_blk, group_blk, group_blk],
        out_specs=group_blk,
        out_shape=jax.ShapeDtypeStruct((b, s, NSA_WIDTH), BF16),
        scratch_shapes=[pltpu.VMEM((2 * LANES, cols), BF16), pltpu.VMEM((tk, cols), F32),
                        pltpu.VMEM((1, cols), F32), pltpu.VMEM((LANES, cols), F32)],
        compiler_params=_vmem_params(("parallel", "parallel", "parallel")),
    )(q3, bias, et, ks, vst, sig3, o_c, o_w, zn3)


def _win_kernel(q_ref, kp_ref, kc_ref, vp_ref, vc_ref, sig_ref, o_ref):
    tq = q_ref.shape[1]
    qt = pl.program_id(2)
    c = lax.broadcasted_iota(jnp.int32, (tq, tq), 0)
    i = lax.broadcasted_iota(jnp.int32, (tq, tq), 1)
    mask_cur = c <= i
    mask_prev = c > i + jnp.where(qt > 0, 0, tq)
    accs = []
    for q_t in _heads_t(q_ref[0]):
        s_p = jnp.where(mask_prev, _dot(kp_ref[0, 0], q_t), -jnp.inf)
        s_c = jnp.where(mask_cur, _dot(kc_ref[0, 0], q_t), -jnp.inf)
        m = jnp.maximum(jnp.max(s_p, axis=0, keepdims=True), jnp.max(s_c, axis=0, keepdims=True))
        e_p = jnp.exp2(s_p - m).astype(BF16)
        e_c = jnp.exp2(s_c - m).astype(BF16)
        accs.append(_dot(vp_ref[0, 0], e_p) + _dot(vc_ref[0, 0], e_c))
    o_ref[0] = _finish_heads(accs, sig_ref[0], 2).astype(BF16)


def _win_attention(q3, kw, vw, sig3, tq):
    b, s, _ = q3.shape
    assert tq == WINDOW
    nt = s // tq
    kspec = lambda back: pl.BlockSpec((1, 1, tq, LANES),
                                      lambda bi, g, i: (g, bi, jnp.maximum(i - back, 0), 0))
    vspec = lambda back: pl.BlockSpec((1, 1, LANES, tq),
                                      lambda bi, g, i: (g, bi * nt + jnp.maximum(i - back, 0), 0, 0))
    return pl.pallas_call(
        _win_kernel,
        grid=(b, NSA_KV_HEADS, nt),
        in_specs=[pl.BlockSpec((1, tq, 256), lambda bi, g, i: (bi, i, g)), kspec(1), kspec(0),
                  vspec(1), vspec(0), pl.BlockSpec((1, tq, LANES), lambda bi, g, i: (bi, i, g))],
        out_specs=pl.BlockSpec((1, tq, 256), lambda bi, g, i: (bi, i, g)),
        out_shape=jax.ShapeDtypeStruct((b, s, NSA_WIDTH), BF16),
        compiler_params=_vmem_params(("parallel", "parallel", "parallel")),
    )(q3, kw, kw, vw, vw, sig3)


def _mem_kv_kernel(mem_ref, gm_ref, w_ref, bd256_ref, gk_ref, mk_ref, mv_ref):
    x = mem_ref[0]
    h = x * lax.rsqrt(jnp.mean(x * x, axis=-1, keepdims=True) + EPS) * gm_ref[...]
    mkv = _dot(h.astype(BF16), w_ref[...])
    mk_ref[0] = _head_rms(mkv[:, 0:MEM_WIDTH], bd256_ref[...], gk_ref[...]).astype(BF16)
    for hp in range(MEM_HEADS // 2):
        lo = MEM_WIDTH + hp * LANES
        mv_ref[0, 2 * hp], mv_ref[0, 2 * hp + 1] = _values_t(mkv[:, lo:lo + LANES])


def _mem_kv(mem, gm, w, bd256, gk):
    b, m, _ = mem.shape
    full = lambda a: pl.BlockSpec(a.shape, lambda i: (0,) * a.ndim)
    return pl.pallas_call(
        _mem_kv_kernel,
        grid=(b,),
        in_specs=[pl.BlockSpec((1, m, D_MODEL), lambda i: (i, 0, 0)), full(gm), full(w), full(bd256), full(gk)],
        out_specs=[pl.BlockSpec((1, m, MEM_WIDTH), lambda i: (i, 0, 0)),
                   pl.BlockSpec((1, MEM_HEADS, LANES, m), lambda i: (i, 0, 0, 0))],
        out_shape=[jax.ShapeDtypeStruct((b, m, MEM_WIDTH), BF16),
                   jax.ShapeDtypeStruct((b, MEM_HEADS, LANES, m), BF16)],
        compiler_params=_vmem_params(("parallel",)),
    )(mem, gm, w, bd256, gk)


def _mem_attention_block(q_blk, mk, mv_t):
    accs = []
    for h, q_t in enumerate(_heads_t(q_blk)):
        hp = h // 2
        s = _dot(mk[:, hp * LANES:(hp + 1) * LANES], q_t)
        e = jnp.exp2(s - jnp.max(s, axis=0, keepdims=True))
        accs.append(_dot(mv_t[h], e.astype(BF16)))
    return _finish_heads(accs, None, 0)


def _out_kernel(seq_len, x_ref, ynsa_ref, vp_ref, halo_ref, zp_ref, qm_ref, mk_ref, mv_ref, zm_ref, wo_ref,
                wpool_ref, pscale_ref, o_ref):
    tm = x_ref.shape[0]
    t0 = (pl.program_id(0) * tm) % seq_len

    v = vp_ref[...]
    halo = halo_ref[...] * (t0 > 0).astype(F32)
    e = jnp.concatenate([halo, v], axis=0)
    w2 = e + pltpu.roll(e, 1, 0)
    w4 = w2 + pltpu.roll(w2, 2, 0)
    w8 = w4 + pltpu.roll(w4, 4, 0)
    w16 = w8 + pltpu.roll(w8, 8, 0)
    shape = (tm, POOL_WIDTH)
    grp = lax.broadcasted_iota(jnp.int32, shape, 1) >> 6
    wsum = jnp.where(grp == 0, w2[POOL_HALO:],
                     jnp.where(grp == 1, w4[POOL_HALO:], jnp.where(grp == 2, w8[POOL_HALO:], w16[POOL_HALO:])))
    width = jnp.left_shift(2, grp)
    cnt = jnp.minimum(t0 + lax.broadcasted_iota(jnp.int32, shape, 0) + 1, width).astype(F32)
    pooled = wsum / cnt - v
    y_pool = _dot(pooled.astype(BF16), wpool_ref[...]) * pscale_ref[...] * zp_ref[...].astype(F32)

    y_mem = _mem_attention_block(qm_ref[...], mk_ref[0], mv_ref[0]) * zm_ref[...].astype(F32)
    out = x_ref[...] + _dot(ynsa_ref[...], wo_ref[0:512, :])
    out = out + _dot(y_pool.astype(BF16), wo_ref[512:768, :])
    out = out + _dot(y_mem.astype(BF16), wo_ref[768:1024, :])
    o_ref[...] = out


def _out_proj(x2, ynsa, vp, zp, qm, mk, mv_t, zm, wo, wpool, pscale, seq_len, tm):
    n = x2.shape[0]
    row = lambda w: pl.BlockSpec((tm, w), lambda i: (i, 0))
    full = lambda a: pl.BlockSpec(a.shape, lambda i: (0,) * a.ndim)
    per_batch = lambda a: pl.BlockSpec((1,) + a.shape[1:],
                                       lambda i: ((i * tm) // seq_len,) + (0,) * (a.ndim - 1))
    hb = tm // POOL_HALO
    halo = pl.BlockSpec((POOL_HALO, POOL_WIDTH), lambda i: (jnp.maximum(i * hb - 1, 0), 0))
    return pl.pallas_call(
        functools.partial(_out_kernel, seq_len),
        grid=(n // tm,),
        in_specs=[row(D_MODEL), row(512), row(256), halo, row(256),
                  row(256), per_batch(mk), per_batch(mv_t), row(256), full(wo), full(wpool), full(pscale)],
        out_specs=row(D_MODEL),
        out_shape=jax.ShapeDtypeStruct((n, D_MODEL), F32),
        compiler_params=_vmem_params(("parallel",)),
    )(x2, ynsa, vp, vp, zp, qm, mk, mv_t, zm, wo, wpool, pscale)


def _block_diag_ones(width):
    i = np.arange(width) // HEAD_DIM
    return jnp.asarray((i[:, None] == i[None, :]).astype(np.float32), dtype=BF16)


def _pack_w_in(w_in):
    q, kv, gate, zn, vp, zp, qm, zm = jnp.split(
        w_in, np.cumsum([512, 768, 24, 512, 256, 256, 256, 256])[:-1].tolist(), axis=1)
    cols = np.zeros((NSA_KV_HEADS, LANES), np.int64)
    used = np.zeros((NSA_KV_HEADS, LANES), bool)
    for g in range(NSA_KV_HEADS):
        for k in range(N_BRANCH):
            for r in range(NSA_GROUP):
                cols[g, k * NSA_GROUP + r] = (g * NSA_GROUP + r) * N_BRANCH + k
                used[g, k * NSA_GROUP + r] = True
    gate_p = jnp.where(jnp.asarray(used.reshape(-1))[None, :], gate[:, cols.reshape(-1)], 0.0)
    return jnp.concatenate([q, kv, gate_p, zn, vp, zp, qm, zm], axis=1).astype(BF16)


def _pack_cmp(pos, w1, w2):
    half = CMP_LEN // 2

    def w1_half(wh):
        z = jnp.zeros((half, NSA_KV_HEADS, HEAD_DIM, NSA_KV_HEADS, HEAD_DIM), F32)
        for g in range(NSA_KV_HEADS):
            z = z.at[:, g, :, g, :].set(wh)
        return z.reshape(half, KV_WIDTH, KV_WIDTH).astype(BF16)

    def pos_half(ph):
        return jnp.tile(ph[:, None, :], (1, NSA_KV_HEADS, 1)).reshape(half, KV_WIDTH)

    w2_bd = jnp.zeros((NSA_KV_HEADS, HEAD_DIM, NSA_KV_HEADS, HEAD_DIM), F32)
    for g in range(NSA_KV_HEADS):
        w2_bd = w2_bd.at[g, :, g, :].set(w2)
    return (pos_half(pos[:half]), pos_half(pos[half:]), w1_half(w1[:half]), w1_half(w1[half:]),
            w2_bd.reshape(KV_WIDTH, KV_WIDTH).astype(BF16))


def _sel_map_t(ncp, s):
    nc = (s - CMP_LEN) // CMP_STRIDE + 1
    ns = s // SEL_BLOCK
    c0 = np.arange(nc) * CMP_STRIDE
    c1 = c0 + CMP_LEN
    s0 = np.arange(ns) * SEL_BLOCK
    s1 = s0 + SEL_BLOCK
    ov = np.clip(np.minimum(c1[:, None], s1[None, :]) - np.maximum(c0[:, None], s0[None, :]), 0, None)
    m = np.zeros((NS_PAD, ncp), np.float32)
    m[:ns, :nc] = (ov / CMP_LEN).T
    return jnp.asarray(m, dtype=BF16)


def kernel(x, mem, positions, g_norm, w_in, g_q_nsa, g_k_cmp, g_k_slc, g_k_win, cmp_pos_k, w_cmp_k1,
           w_cmp_k2, cmp_pos_v, w_cmp_v1, w_cmp_v2, w_pool, pool_scale, g_mem, w_mem_kv, g_q_mem,
           g_k_mem, w_out):
    b, s, d = x.shape
    depth = g_norm.shape[0]
    assert d == D_MODEL and s % WINDOW == 0 and s // SEL_BLOCK >= SEL_TOPK and s // SEL_BLOCK <= NS_PAD
    n = b * s
    ncp = s // CMP_STRIDE
    tm = 512
    t_cmp = 256
    t_sel = 256
    t_win = WINDOW
    assert s % tm == 0 and tm % t_sel == 0 and tm == WINDOW

    inv_freq = ROPE_THETA ** (-jnp.arange(ROPE_HALF, dtype=F32) / ROPE_HALF)
    invf = inv_freq[(np.arange(LANES) % HEAD_DIM) % ROPE_HALF][None, :]
    bd512, bd256, bd128 = _block_diag_ones(512), _block_diag_ones(256), _block_diag_ones(128)
    cs = _rope_cs(positions.reshape(1, n), inv_freq[:, None], min(n, 8192))
    rope_e = _rope_expanders()
    end_idx = np.minimum(np.arange(ncp) * CMP_STRIDE + CMP_LEN - 1, s - 1)
    pend = positions[:, end_idx][:, :, None]
    selt = _sel_map_t(ncp, s)
    et = jnp.asarray((np.arange(s)[:, None] // SEL_BLOCK == np.arange(NS_PAD)[None, :]).astype(np.float32),
                     dtype=BF16)
    tile = lambda g, reps: jnp.tile(g[None, :], (1, reps))

    for l in range(depth):
        x2 = x.reshape(n, d)
        w_all = _pack_w_in(w_in[l])
        (q, kc_raw, vc_raw, ks, vs, kw, vw, sig, zn, vp, zp, qm, zm) = _in_proj(
            x2, cs, g_norm[l][None, :], w_all, bd512, bd256, tile(g_q_nsa[l], 8), tile(g_k_slc[l], 2),
            tile(g_k_win[l], 2), tile(g_q_mem[l], 4), rope_e, tm, t_win)

        cmp_consts = (_pack_cmp(cmp_pos_k[l], w_cmp_k1[l], w_cmp_k2[l])
                      + _pack_cmp(cmp_pos_v[l], w_cmp_v1[l], w_cmp_v2[l].T)
                      + (bd128, tile(g_k_cmp[l], 2), invf))
        kc, vc = _compress(kc_raw.reshape(b, s, KV_WIDTH), vc_raw.reshape(b, s, KV_WIDTH), pend, cmp_consts)

        q3 = q.reshape(b, s, NSA_WIDTH)
        sig3 = sig.reshape(b, s, 256)
        o_c, bias = _cmp_attention(q3, kc, vc, sig3, selt, t_cmp)
        dup4 = lambda a: a.reshape(NSA_KV_HEADS, b, s, LANES)
        vt5 = lambda a: a.reshape(NSA_KV_HEADS, b, s // tm, LANES, tm)
        o_w = _win_attention(q3, dup4(kw), vw, sig3, t_win)
        y_nsa = _sel_attention(q3, bias, et, dup4(ks), vt5(vs), sig3, o_c, o_w,
                               zn.reshape(b, s, NSA_WIDTH), t_sel)

        mk, mv_t = _mem_kv(mem, g_mem[l][None, :], w_mem_kv[l].astype(BF16), bd256, tile(g_k_mem[l], 4))

        wpool = jnp.zeros((4, HEAD_DIM, 4, HEAD_DIM), F32)
        for g in range(4):
            wpool = wpool.at[g, :, g, :].set(w_pool[l, g])
        out = _out_proj(x2, y_nsa.reshape(n, NSA_WIDTH), vp, zp, qm, mk, mv_t, zm, w_out[l].astype(BF16),
                        wpool.reshape(POOL_WIDTH, POOL_WIDTH).astype(BF16), pool_scale[l][None, :], s, tm)
        x = out.reshape(b, s, d)
    return x
```

```python
import functools

import numpy as np
import jax
import jax.numpy as jnp
from jax import lax
from jax.experimental import pallas as pl
from jax.experimental.pallas import tpu as pltpu

F32 = jnp.float32
BF16 = jnp.bfloat16

D_MODEL = 1024
HEAD_DIM = 64
NSA_HEADS = 8
NSA_KV_HEADS = 2
NSA_GROUP = NSA_HEADS // NSA_KV_HEADS
NSA_WIDTH = NSA_HEADS * HEAD_DIM
KV_WIDTH = NSA_KV_HEADS * HEAD_DIM
N_BRANCH = 3
CMP_LEN = 32
CMP_STRIDE = 16
SEL_BLOCK = 64
SEL_TOPK = 16
WINDOW = 512
FORCE_SCORE = 1e4
N_FORCED = 3
POOL_WINDOWS = (2, 4, 8, 16)
POOL_WIDTH = 256
MEM_HEADS = 4
MEM_WIDTH = MEM_HEADS * HEAD_DIM
ROPE_THETA = 500000.0
ROPE_DIM = HEAD_DIM // 4
ROPE_HALF = ROPE_DIM // 2
EPS = 1e-6
SCALE = HEAD_DIM ** -0.5
Q_SCALE = SCALE * float(np.log2(np.e))

LANES = 128
NS_PAD = LANES
MASK_BIAS = -1e9
POOL_HALO = 16
CMP_CHUNK = LANES
SEL_UNROLL = 4
ROPE_ROWS = 32
TINY = float(np.finfo(np.float32).tiny)

_SEG = dict(q=(0, 512), kv=(512, 768), gate=(1280, 256), zn=(1536, 512), vp=(2048, 256),
            zp=(2304, 256), qm=(2560, 256), zm=(2816, 256))
W_ALL_WIDTH = 3072

_NT = (((1,), (1,)), ((), ()))


def _dot(a, b):
    return jnp.dot(a, b, preferred_element_type=F32)


def _dot_nt(a, b):
    return lax.dot_general(a, b, _NT, preferred_element_type=F32)


def _vmem_params(sem):
    return pltpu.CompilerParams(dimension_semantics=sem, vmem_limit_bytes=56 * 1024 * 1024)


def _rope_tables(pos_f32, invf):
    ang = pos_f32 * invf
    c, s = jnp.cos(ang), jnp.sin(ang)
    d = lax.broadcasted_iota(jnp.int32, ang.shape, 1) % HEAD_DIM
    tc = jnp.where(d < ROPE_DIM, c, 1.0)
    ts_up = jnp.where((d >= ROPE_HALF) & (d < ROPE_DIM), s, 0.0)
    ts_dn = jnp.where(d < ROPE_HALF, -s, 0.0)
    return tc, ts_up, ts_dn


def _rope(x, tables):
    tc, ts_up, ts_dn = tables
    w = x.shape[1]
    reps = w // LANES
    if reps > 1:
        tc, ts_up, ts_dn = (jnp.concatenate([t] * reps, axis=1) for t in (tc, ts_up, ts_dn))
    return (x * tc + pltpu.roll(x, ROPE_HALF, 1) * ts_up
            + pltpu.roll(x, w - ROPE_HALF, 1) * ts_dn)


def _head_rms(x, ones_bd, gain):
    ss = _dot((x * x).astype(BF16), ones_bd)
    return x * lax.rsqrt(ss * (1.0 / HEAD_DIM) + EPS) * gain


def _dup_halves(x):
    low = lax.broadcasted_iota(jnp.int32, x.shape, 1) < HEAD_DIM
    r = pltpu.roll(x, HEAD_DIM, 1)
    return jnp.where(low, x, r), jnp.where(low, r, x)


def _values_t(v):
    return _values_rows(v.T)


def _values_rows(vt):
    vt = vt.astype(BF16)
    ones = jnp.ones((HEAD_DIM, vt.shape[1]), BF16)
    return (jnp.concatenate([vt[0:HEAD_DIM], ones], axis=0),
            jnp.concatenate([vt[HEAD_DIM:], ones], axis=0))


def _silu(z):
    return z * jax.nn.sigmoid(z)


def _in_proj_kernel(x_ref, cs_ref, gn_ref, w_ref, bd512_ref, bd256_ref, gq_ref, gks_ref, gkw_ref,
                    gqm_ref, rope_e_ref,
                    q_ref, kc_ref, vc_ref, ks_ref, vs_ref, kw_ref, vw_ref, sig_ref, zn_ref, vp_ref,
                    zp_ref, qm_ref, zm_ref):
    x = x_ref[...]
    h = x * lax.rsqrt(jnp.mean(x * x, axis=-1, keepdims=True) + EPS) * gn_ref[...]
    hb = h.astype(BF16)

    def seg(name):
        o, w = _SEG[name]
        return _dot(hb, w_ref[:, o:o + w])

    cs = cs_ref[...]
    cs_hi = cs.astype(BF16)
    cs_lo = (cs - cs_hi.astype(F32)).astype(BF16)
    tables = tuple((_dot(rope_e_ref[k], cs_hi) + _dot(rope_e_ref[k], cs_lo)).T for k in range(3))

    q = _head_rms(seg('q'), bd512_ref[...], gq_ref[...])
    q_ref[...] = (_rope(q, tables) * Q_SCALE).astype(BF16)

    kv = seg('kv')
    kc_ref[...] = kv[:, 0:128]
    vc_ref[...] = kv[:, 128:256]

    def norm_dup(k, gain):
        k = k * lax.rsqrt(jnp.mean(k * k, axis=-1, keepdims=True) + EPS) * gain
        return _rope(k, tables).astype(BF16)

    ks0, ks1 = _dup_halves(kv[:, 256:384])
    ks_ref[0] = norm_dup(ks0, gks_ref[...])
    ks_ref[1] = norm_dup(ks1, gks_ref[...])
    vs_ref[0, 0], vs_ref[1, 0] = _values_t(kv[:, 384:512])
    kw0, kw1 = _dup_halves(kv[:, 512:640])
    kw_ref[0] = norm_dup(kw0, gkw_ref[...])
    kw_ref[1] = norm_dup(kw1, gkw_ref[...])
    vw = _values_t(kv[:, 640:768])
    wt = vw_ref.shape[3]
    for g in range(NSA_KV_HEADS):
        for c in range(vw_ref.shape[1]):
            vw_ref[g, c] = vw[g][:, c * wt:(c + 1) * wt]

    sig_ref[...] = jax.nn.sigmoid(seg('gate'))
    zn_ref[...] = _silu(seg('zn')).astype(BF16)
    vp_ref[...] = seg('vp')
    zp_ref[...] = _silu(seg('zp')).astype(BF16)
    qm = _head_rms(seg('qm'), bd256_ref[...], gqm_ref[...])
    qm_ref[...] = (qm * Q_SCALE).astype(BF16)
    zm_ref[...] = _silu(seg('zm')).astype(BF16)


def _rope_cs_kernel(pos_ref, invf_ref, o_ref):
    ang = invf_ref[...] * pos_ref[...].astype(F32)
    tail = lax.broadcasted_iota(jnp.int32, (ROPE_ROWS - 2 * ROPE_HALF, ang.shape[1]), 0) == 0
    o_ref[...] = jnp.concatenate([jnp.cos(ang), jnp.sin(ang), tail.astype(F32)], axis=0)


def _rope_cs(pos_row, invf_col, blk):
    n = pos_row.shape[1]
    return pl.pallas_call(
        _rope_cs_kernel,
        grid=(n // blk,),
        in_specs=[pl.BlockSpec((1, blk), lambda i: (0, i)), pl.BlockSpec(invf_col.shape, lambda i: (0, 0))],
        out_specs=pl.BlockSpec((ROPE_ROWS, blk), lambda i: (0, i)),
        out_shape=jax.ShapeDtypeStruct((ROPE_ROWS, n), F32),
        compiler_params=_vmem_params(("parallel",)),
    )(pos_row, invf_col)


def _rope_expanders():
    e = np.zeros((3, LANES, ROPE_ROWS), np.float32)
    for lane in range(LANES):
        d = lane % HEAD_DIM
        if d < ROPE_DIM:
            e[0, lane, d % ROPE_HALF] = 1.0
        else:
            e[0, lane, 2 * ROPE_HALF] = 1.0
        if ROPE_HALF <= d < ROPE_DIM:
            e[1, lane, ROPE_HALF + d % ROPE_HALF] = 1.0
        if d < ROPE_HALF:
            e[2, lane, ROPE_HALF + d] = -1.0
    return jnp.asarray(e, dtype=BF16)


def _in_proj(x2, cs, gn, w_all, bd512, bd256, gq, gks, gkw, gqm, rope_e, tm, t_win):
    n = x2.shape[0]
    row = lambda w: pl.BlockSpec((tm, w), lambda i: (i, 0))
    full = lambda a: pl.BlockSpec(a.shape, lambda i: (0,) * a.ndim)
    dup = pl.BlockSpec((NSA_KV_HEADS, tm, LANES), lambda i: (0, i, 0))
    sd = jax.ShapeDtypeStruct
    vst = pl.BlockSpec((NSA_KV_HEADS, 1, LANES, tm), lambda i: (0, i, 0, 0))
    vwt = pl.BlockSpec((NSA_KV_HEADS, tm // t_win, LANES, t_win), lambda i: (0, i, 0, 0))
    out_shape = [sd((n, 512), BF16), sd((n, 128), F32), sd((n, 128), F32),
                 sd((2, n, 128), BF16), sd((2, n // tm, LANES, tm), BF16), sd((2, n, 128), BF16),
                 sd((2, n // t_win, LANES, t_win), BF16), sd((n, 256), F32), sd((n, 512), BF16), sd((n, 256), F32),
                 sd((n, 256), BF16), sd((n, 256), BF16), sd((n, 256), BF16)]
    out_specs = [row(512), row(128), row(128), dup, vst, dup, vwt, row(256), row(512), row(256),
                 row(256), row(256), row(256)]
    consts = (gn, w_all, bd512, bd256, gq, gks, gkw, gqm, rope_e)
    return pl.pallas_call(
        _in_proj_kernel,
        grid=(n // tm,),
        in_specs=[row(D_MODEL), pl.BlockSpec((ROPE_ROWS, tm), lambda i: (0, i))] + [full(a) for a in consts],
        out_specs=out_specs,
        out_shape=out_shape,
        compiler_params=_vmem_params(("parallel",)),
    )(x2, cs, *consts)


def _compress_kernel(zk_ref, zv_ref, pend_ref, pak_ref, pbk_ref, w1ak_ref, w1bk_ref, w2k_ref,
                     pav_ref, pbv_ref, w1av_ref, w1bv_ref, w2v_ref, bd128_ref, gk_ref, invf_ref,
                     kc_ref, vc_ref):
    ncp = zk_ref.shape[1] // CMP_STRIDE

    def mlp(z_ref, pa, pb, w1a, w1b, w2, transposed=False):
        a = jnp.zeros((ncp, KV_WIDTH), F32)
        b = jnp.zeros((ncp, KV_WIDTH), F32)
        for o in range(CMP_STRIDE):
            z = z_ref[0, pl.ds(o, ncp, stride=CMP_STRIDE), :]
            a = a + _dot((z + pa[o:o + 1, :]).astype(BF16), w1a[o])
            b = b + _dot((z + pb[o:o + 1, :]).astype(BF16), w1b[o])
        pre = a + pltpu.roll(b, ncp - 1, 0)
        hid = jax.nn.gelu(pre).astype(BF16)
        return _dot_nt(w2, hid) if transposed else _dot(hid, w2)

    kc = mlp(zk_ref, pak_ref, pbk_ref, w1ak_ref, w1bk_ref, w2k_ref[...])
    kc = _head_rms(kc, bd128_ref[...], gk_ref[...])
    kc = _rope(kc, _rope_tables(pend_ref[0].astype(F32), invf_ref[...]))
    k0, k1 = _dup_halves(kc)
    kc_ref[0, 0] = k0.astype(BF16)
    kc_ref[0, 1] = k1.astype(BF16)
    vc_t = mlp(zv_ref, pav_ref, pbv_ref, w1av_ref, w1bv_ref, w2v_ref[...], True)
    vc_ref[0, 0], vc_ref[0, 1] = _values_rows(vc_t)


def _compress(zk, zv, pend, consts):
    b, s, zw = zk.shape
    ncp = s // CMP_STRIDE
    full = lambda a: pl.BlockSpec(a.shape, lambda i: (0,) * a.ndim)
    zspec = pl.BlockSpec((1, s, zw), lambda i: (i, 0, 0))
    ospec = pl.BlockSpec((1, NSA_KV_HEADS, ncp, LANES), lambda i: (i, 0, 0, 0))
    sd = jax.ShapeDtypeStruct((b, NSA_KV_HEADS, ncp, LANES), BF16)
    return pl.pallas_call(
        _compress_kernel,
        grid=(b,),
        in_specs=[zspec, zspec, pl.BlockSpec((1, ncp, 1), lambda i: (i, 0, 0))] + [full(a) for a in consts],
        out_specs=[ospec, pl.BlockSpec((1, NSA_KV_HEADS, LANES, ncp), lambda i: (i, 0, 0, 0))],
        out_shape=[sd, jax.ShapeDtypeStruct((b, NSA_KV_HEADS, LANES, ncp), BF16)],
        compiler_params=_vmem_params(("parallel",)),
    )(zk, zv, pend, *consts)


def _gate_col(sig_blk, r, branch):
    c = branch * NSA_GROUP + r
    return sig_blk[:, c:c + 1]


def _heads_t(q_blk):
    out = []
    for rp in range(NSA_GROUP // 2):
        pair_t = q_blk[:, rp * LANES:(rp + 1) * LANES].astype(F32).T
        low = lax.broadcasted_iota(jnp.int32, pair_t.shape, 0) < HEAD_DIM
        out.append(jnp.where(low, pair_t, 0.0).astype(BF16))
        out.append(jnp.where(low, 0.0, pair_t).astype(BF16))
    return out


def _finish_heads(accs, sig, branch):
    outs = []
    for r, acc in enumerate(accs):
        a = acc.T
        o = a / jnp.maximum(pltpu.roll(a, HEAD_DIM, 1), TINY)
        outs.append(o if sig is None else o * _gate_col(sig, r, branch))
    low = lax.broadcasted_iota(jnp.int32, outs[0].shape, 1) < HEAD_DIM
    return jnp.concatenate(
        [jnp.where(low, outs[0], pltpu.roll(outs[1], HEAD_DIM, 1)),
         jnp.where(low, outs[2], pltpu.roll(outs[3], HEAD_DIM, 1))], axis=1)


def _cmp_kernel(q_ref, kc_ref, vc_ref, sig_ref, selt_ref, o_ref, bias_ref, psum_ref, acc_ref):
    tq = q_ref.shape[1]
    ncp = kc_ref.shape[2]
    t0 = pl.program_id(2) * tq
    q_heads = _heads_t(q_ref[0])

    def attend(nk):
        kc = kc_ref[0, 0, 0:nk, :]
        vc_t = vc_ref[0, 0, :, 0:nk]
        t = t0 + lax.broadcasted_iota(jnp.int32, (nk, tq), 1)
        n = lax.broadcasted_iota(jnp.int32, (nk, tq), 0)
        cmask = n * CMP_STRIDE + (CMP_LEN - 1) <= t
        for r, q_t in enumerate(q_heads):
            s = jnp.where(cmask, _dot(kc, q_t), -jnp.inf)
            m = jnp.max(s, axis=0, keepdims=True)
            m = jnp.where(m == -jnp.inf, 0.0, m)
            e = jnp.exp2(s - m)
            p = e * (1.0 / jnp.maximum(jnp.sum(e, axis=0, keepdims=True), TINY))
            psum_ref[0:nk, :] = p if r == 0 else psum_ref[0:nk, :] + p
            acc_ref[r] = _dot(vc_t, e.astype(BF16))
        if nk < ncp:
            psum_ref[nk:ncp, :] = jnp.zeros((ncp - nk, tq), F32)

    last_block = (t0 + tq - CMP_LEN) // CMP_STRIDE
    n_chunks = jnp.minimum(last_block // CMP_CHUNK + 1, ncp // CMP_CHUNK)
    for v in range(1, ncp // CMP_CHUNK + 1):
        pl.when(n_chunks == v)(functools.partial(attend, v * CMP_CHUNK))
    o_ref[0] = _finish_heads([acc_ref[r] for r in range(NSA_GROUP)], sig_ref[0], 0).astype(BF16)

    psum = psum_ref[...]
    ps_hi = psum.astype(BF16)
    ps_lo = (psum - ps_hi.astype(F32)).astype(BF16)
    selt = selt_ref[...]
    imp = _dot(selt, ps_hi) + _dot(selt, ps_lo)
    j = lax.broadcasted_iota(jnp.int32, (NS_PAD, tq), 0)
    cur = (t0 + lax.broadcasted_iota(jnp.int32, (NS_PAD, tq), 1)) >> 6
    forced = (j == 0) | (j == cur) | (j == cur - 1)
    valid = j <= cur
    vals = jnp.where(forced, -jnp.inf, jnp.where(valid, imp, -1.0))
    neg_j = -j.astype(F32)
    for _ in range(SEL_TOPK - N_FORCED):
        m = jnp.max(vals, axis=0, keepdims=True)
        first = jnp.max(jnp.where(vals == m, neg_j, -jnp.inf), axis=0, keepdims=True)
        vals = jnp.where(neg_j == first, -jnp.inf, vals)
    bias_t = jnp.where((vals == -jnp.inf) & valid, 0.0, MASK_BIAS)
    bias_ref[0, 0] = bias_t.astype(BF16)


def _cmp_attention(q3, kc, vc, sig3, selt, tq):
    b, s, _ = q3.shape
    ncp = kc.shape[2]
    assert ncp % CMP_CHUNK == 0 and tq % SEL_BLOCK == 0
    grid = (b, NSA_KV_HEADS, s // tq)
    return pl.pallas_call(
        _cmp_kernel,
        grid=grid,
        in_specs=[pl.BlockSpec((1, tq, 256), lambda bi, g, i: (bi, i, g)),
                  pl.BlockSpec((1, 1, ncp, LANES), lambda bi, g, i: (bi, g, 0, 0)),
                  pl.BlockSpec((1, 1, LANES, ncp), lambda bi, g, i: (bi, g, 0, 0)),
                  pl.BlockSpec((1, tq, LANES), lambda bi, g, i: (bi, i, g)),
                  pl.BlockSpec(selt.shape, lambda bi, g, i: (0, 0))],
        out_specs=[pl.BlockSpec((1, tq, 256), lambda bi, g, i: (bi, i, g)),
                   pl.BlockSpec((1, 1, NS_PAD, tq), lambda bi, g, i: (bi, g, 0, i))],
        out_shape=[jax.ShapeDtypeStruct((b, s, NSA_WIDTH), BF16),
                   jax.ShapeDtypeStruct((b, NSA_KV_HEADS, NS_PAD, s), BF16)],
        scratch_shapes=[pltpu.VMEM((ncp, tq), F32), pltpu.VMEM((NSA_GROUP, LANES, tq), F32)],
        compiler_params=_vmem_params(("parallel", "parallel", "parallel")),
    )(q3, kc, vc, sig3, selt)


def _sel_kernel(q_ref, bias_ref, et_ref, ks_ref, vs_ref, sig_ref, oc_ref, ow_ref, zn_ref, o_ref,
                qaug_ref, s_ref, m_ref, acc_ref):
    tq = q_ref.shape[1]
    tk = vs_ref.shape[4]
    cols = NSA_GROUP * tq
    qt = pl.program_id(2)

    bias = bias_ref[0, 0]
    for r, q_t in enumerate(_heads_t(q_ref[0])):
        qaug_ref[0:LANES, r * tq:(r + 1) * tq] = bias
        qaug_ref[LANES:2 * LANES, r * tq:(r + 1) * tq] = q_t
    m_ref[...] = jnp.full(m_ref.shape, -jnp.inf, F32)
    acc_ref[...] = jnp.zeros(acc_ref.shape, F32)
    heads = [slice(r * tq, (r + 1) * tq) for r in range(NSA_GROUP)]

    def keys(kt):
        k0 = pl.multiple_of(kt * tk, tk)
        return jnp.concatenate([et_ref[pl.ds(k0, tk), :], ks_ref[0, 0, pl.ds(k0, tk), :]], axis=1)

    def softmax_pv(kt, h, s):
        rows = s.shape[0]
        m_prev = m_ref[:, h]
        m_new = jnp.maximum(m_prev, jnp.max(s, axis=0, keepdims=True))
        p = jnp.exp2(s - m_new).astype(BF16)
        acc_ref[:, h] = jnp.exp2(m_prev - m_new) * acc_ref[:, h] + _dot(vs_ref[0, 0, kt, :, 0:rows], p)
        m_ref[:, h] = m_new

    k_first = keys(0)
    for h in heads:
        s_ref[:, h] = _dot(k_first, qaug_ref[:, h])

    def step(kt):
        k_next = keys(kt + 1)
        for h in heads:
            s = s_ref[:, h]
            s_next = _dot(k_next, qaug_ref[:, h])
            softmax_pv(kt, h, s)
            s_ref[:, h] = s_next

    def unrolled_steps(j, first):
        for u in range(SEL_UNROLL):
            step(first + SEL_UNROLL * j + u)
        return first

    def two_steps(first):
        step(first)
        step(first + 1)

    last = (qt * tq) // tk
    rest = last % SEL_UNROLL
    pl.when(rest % 2 == 1)(lambda: step(0))
    pl.when(rest >= 2)(functools.partial(two_steps, rest % 2))
    lax.fori_loop(0, last // SEL_UNROLL, unrolled_steps, rest)

    def last_tile(rows):
        kpos = last * tk + lax.broadcasted_iota(jnp.int32, (rows, tq), 0)
        causal = kpos <= qt * tq + lax.broadcasted_iota(jnp.int32, (rows, tq), 1)
        for h in heads:
            softmax_pv(last, h, jnp.where(causal, s_ref[0:rows, h], MASK_BIAS))

    visible = (qt + 1) * tq - last * tk
    for rows in range(tq, tk + 1, tq):
        pl.when(visible == rows)(functools.partial(last_tile, rows))

    acc = acc_ref[...]
    o_sel = _finish_heads([acc[:, h] for h in heads], sig_ref[0], 1)
    o_nsa = oc_ref[0].astype(F32) + o_sel + ow_ref[0].astype(F32)
    o_ref[0] = (o_nsa * zn_ref[0].astype(F32)).astype(BF16)


def _sel_attention(q3, bias, et, ks, vst, sig3, o_c, o_w, zn3, tq):
    b, s, _ = q3.shape
    tk = vst.shape[4]
    cols = NSA_GROUP * tq
    assert SEL_UNROLL == 4 and tk % tq == 0
    group_blk = pl.BlockSpec((1, tq, 256), lambda bi, g, i: (bi, i, g))
    return pl.pallas_call(
        _sel_kernel,
        grid=(b, NSA_KV_HEADS, s // tq),
        in_specs=[group_blk,
                  pl.BlockSpec((1, 1, NS_PAD, tq), lambda bi, g, i: (bi, g, 0, i)),
                  pl.BlockSpec((s, LANES), lambda bi, g, i: (0, 0)),
                  pl.BlockSpec((1, 1, s, LANES), lambda bi, g, i: (g, bi, 0, 0)),
                  pl.BlockSpec((1, 1, s // tk, LANES, tk), lambda bi, g, i: (g, bi, 0, 0, 0)),
                  pl.BlockSpec((1, tq, LANES), lambda bi, g, i: (bi, i, g)),
                  group_blk, group_blk, group_blk],
        out_specs=group_blk,
        out_shape=jax.ShapeDtypeStruct((b, s, NSA_WIDTH), BF16),
        scratch_shapes=[pltpu.VMEM((2 * LANES, cols), BF16), pltpu.VMEM((tk, cols), F32),
                        pltpu.VMEM((1, cols), F32), pltpu.VMEM((LANES, cols), F32)],
        compiler_params=_vmem_params(("parallel", "parallel", "parallel")),
    )(q3, bias, et, ks, vst, sig3, o_c, o_w, zn3)


def _win_kernel(q_ref, kp_ref, kc_ref, vp_ref, vc_ref, sig_ref, o_ref):
    tq = q_ref.shape[1]
    qt = pl.program_id(2)
    c = lax.broadcasted_iota(jnp.int32, (tq, tq), 0)
    i = lax.broadcasted_iota(jnp.int32, (tq, tq), 1)
    mask_cur = c <= i
    mask_prev = c > i + jnp.where(qt > 0, 0, tq)
    accs = []
    for q_t in _heads_t(q_ref[0]):
        s_p = jnp.where(mask_prev, _dot(kp_ref[0, 0], q_t), -jnp.inf)
        s_c = jnp.where(mask_cur, _dot(kc_ref[0, 0], q_t), -jnp.inf)
        m = jnp.maximum(jnp.max(s_p, axis=0, keepdims=True), jnp.max(s_c, axis=0, keepdims=True))
        e_p = jnp.exp2(s_p - m).astype(BF16)
        e_c = jnp.exp2(s_c - m).astype(BF16)
        accs.append(_dot(vp_ref[0, 0], e_p) + _dot(vc_ref[0, 0], e_c))
    o_ref[0] = _finish_heads(accs, sig_ref[0], 2).astype(BF16)


def _win_attention(q3, kw, vw, sig3, tq):
    b, s, _ = q3.shape
    assert tq == WINDOW
    nt = s // tq
    kspec = lambda back: pl.BlockSpec((1, 1, tq, LANES),
                                      lambda bi, g, i: (g, bi, jnp.maximum(i - back, 0), 0))
    vspec = lambda back: pl.BlockSpec((1, 1, LANES, tq),
                                      lambda bi, g, i: (g, bi * nt + jnp.maximum(i - back, 0), 0, 0))
    return pl.pallas_call(
        _win_kernel,
        grid=(b, NSA_KV_HEADS, nt),
        in_specs=[pl.BlockSpec((1, tq, 256), lambda bi, g, i: (bi, i, g)), kspec(1), kspec(0),
                  vspec(1), vspec(0), pl.BlockSpec((1, tq, LANES), lambda bi, g, i: (bi, i, g))],
        out_specs=pl.BlockSpec((1, tq, 256), lambda bi, g, i: (bi, i, g)),
        out_shape=jax.ShapeDtypeStruct((b, s, NSA_WIDTH), BF16),
        compiler_params=_vmem_params(("parallel", "parallel", "parallel")),
    )(q3, kw, kw, vw, vw, sig3)


def _mem_kv_kernel(mem_ref, gm_ref, w_ref, bd256_ref, gk_ref, mk_ref, mv_ref):
    x = mem_ref[0]
    h = x * lax.rsqrt(jnp.mean(x * x, axis=-1, keepdims=True) + EPS) * gm_ref[...]
    mkv = _dot(h.astype(BF16), w_ref[...])
    mk_ref[0] = _head_rms(mkv[:, 0:MEM_WIDTH], bd256_ref[...], gk_ref[...]).astype(BF16)
    for hp in range(MEM_HEADS // 2):
        lo = MEM_WIDTH + hp * LANES
        mv_ref[0, 2 * hp], mv_ref[0, 2 * hp + 1] = _values_t(mkv[:, lo:lo + LANES])


def _mem_kv(mem, gm, w, bd256, gk):
    b, m, _ = mem.shape
    full = lambda a: pl.BlockSpec(a.shape, lambda i: (0,) * a.ndim)
    return pl.pallas_call(
        _mem_kv_kernel,
        grid=(b,),
        in_specs=[pl.BlockSpec((1, m, D_MODEL), lambda i: (i, 0, 0)), full(gm), full(w), full(bd256), full(gk)],
        out_specs=[pl.BlockSpec((1, m, MEM_WIDTH), lambda i: (i, 0, 0)),
                   pl.BlockSpec((1, MEM_HEADS, LANES, m), lambda i: (i, 0, 0, 0))],
        out_shape=[jax.ShapeDtypeStruct((b, m, MEM_WIDTH), BF16),
                   jax.ShapeDtypeStruct((b, MEM_HEADS, LANES, m), BF16)],
        compiler_params=_vmem_params(("parallel",)),
    )(mem, gm, w, bd256, gk)


def _mem_attention_block(q_blk, mk, mv_t):
    accs = []
    for h, q_t in enumerate(_heads_t(q_blk)):
        hp = h // 2
        s = _dot(mk[:, hp * LANES:(hp + 1) * LANES], q_t)
        e = jnp.exp2(s - jnp.max(s, axis=0, keepdims=True))
        accs.append(_dot(mv_t[h], e.astype(BF16)))
    return _finish_heads(accs, None, 0)


def _out_kernel(seq_len, x_ref, ynsa_ref, vp_ref, halo_ref, zp_ref, qm_ref, mk_ref, mv_ref, zm_ref, wo_ref,
                wpool_ref, pscale_ref, o_ref):
    tm = x_ref.shape[0]
    t0 = (pl.program_id(0) * tm) % seq_len

    v = vp_ref[...]
    halo = halo_ref[...] * (t0 > 0).astype(F32)
    e = jnp.concatenate([halo, v], axis=0)
    w2 = e + pltpu.roll(e, 1, 0)
    w4 = w2 + pltpu.roll(w2, 2, 0)
    w8 = w4 + pltpu.roll(w4, 4, 0)
    w16 = w8 + pltpu.roll(w8, 8, 0)
    shape = (tm, POOL_WIDTH)
    grp = lax.broadcasted_iota(jnp.int32, shape, 1) >> 6
    wsum = jnp.where(grp == 0, w2[POOL_HALO:],
                     jnp.where(grp == 1, w4[POOL_HALO:], jnp.where(grp == 2, w8[POOL_HALO:], w16[POOL_HALO:])))
    width = jnp.left_shift(2, grp)
    cnt = jnp.minimum(t0 + lax.broadcasted_iota(jnp.int32, shape, 0) + 1, width).astype(F32)
    pooled = wsum / cnt - v
    y_pool = _dot(pooled.astype(BF16), wpool_ref[...]) * pscale_ref[...] * zp_ref[...].astype(F32)

    y_mem = _mem_attention_block(qm_ref[...], mk_ref[0], mv_ref[0]) * zm_ref[...].astype(F32)
    out = x_ref[...] + _dot(ynsa_ref[...], wo_ref[0:512, :])
    out = out + _dot(y_pool.astype(BF16), wo_ref[512:768, :])
    out = out + _dot(y_mem.astype(BF16), wo_ref[768:1024, :])
    o_ref[...] = out


def _out_proj(x2, ynsa, vp, zp, qm, mk, mv_t, zm, wo, wpool, pscale, seq_len, tm):
    n = x2.shape[0]
    row = lambda w: pl.BlockSpec((tm, w), lambda i: (i, 0))
    full = lambda a: pl.BlockSpec(a.shape, lambda i: (0,) * a.ndim)
    per_batch = lambda a: pl.BlockSpec((1,) + a.shape[1:],
                                       lambda i: ((i * tm) // seq_len,) + (0,) * (a.ndim - 1))
    hb = tm // POOL_HALO
    halo = pl.BlockSpec((POOL_HALO, POOL_WIDTH), lambda i: (jnp.maximum(i * hb - 1, 0), 0))
    return pl.pallas_call(
        functools.partial(_out_kernel, seq_len),
        grid=(n // tm,),
        in_specs=[row(D_MODEL), row(512), row(256), halo, row(256),
                  row(256), per_batch(mk), per_batch(mv_t), row(256), full(wo), full(wpool), full(pscale)],
        out_specs=row(D_MODEL),
        out_shape=jax.ShapeDtypeStruct((n, D_MODEL), F32),
        compiler_params=_vmem_params(("parallel",)),
    )(x2, ynsa, vp, vp, zp, qm, mk, mv_t, zm, wo, wpool, pscale)


def _block_diag_ones(width):
    i = np.arange(width) // HEAD_DIM
    return jnp.asarray((i[:, None] == i[None, :]).astype(np.float32), dtype=BF16)


def _pack_w_in(w_in):
    q, kv, gate, zn, vp, zp, qm, zm = jnp.split(
        w_in, np.cumsum([512, 768, 24, 512, 256, 256, 256, 256])[:-1].tolist(), axis=1)
    cols = np.zeros((NSA_KV_HEADS, LANES), np.int64)
    used = np.zeros((NSA_KV_HEADS, LANES), bool)
    for g in range(NSA_KV_HEADS):
        for k in range(N_BRANCH):
            for r in range(NSA_GROUP):
                cols[g, k * NSA_GROUP + r] = (g * NSA_GROUP + r) * N_BRANCH + k
                used[g, k * NSA_GROUP + r] = True
    gate_p = jnp.where(jnp.asarray(used.reshape(-1))[None, :], gate[:, cols.reshape(-1)], 0.0)
    return jnp.concatenate([q, kv, gate_p, zn, vp, zp, qm, zm], axis=1).astype(BF16)


def _pack_cmp(pos, w1, w2):
    half = CMP_LEN // 2

    def w1_half(wh):
        z = jnp.zeros((half, NSA_KV_HEADS, HEAD_DIM, NSA_KV_HEADS, HEAD_DIM), F32)
        for g in range(NSA_KV_HEADS):
            z = z.at[:, g, :, g, :].set(wh)
        return z.reshape(half, KV_WIDTH, KV_WIDTH).astype(BF16)

    def pos_half(ph):
        return jnp.tile(ph[:, None, :], (1, NSA_KV_HEADS, 1)).reshape(half, KV_WIDTH)

    w2_bd = jnp.zeros((NSA_KV_HEADS, HEAD_DIM, NSA_KV_HEADS, HEAD_DIM), F32)
    for g in range(NSA_KV_HEADS):
        w2_bd = w2_bd.at[g, :, g, :].set(w2)
    return (pos_half(pos[:half]), pos_half(pos[half:]), w1_half(w1[:half]), w1_half(w1[half:]),
            w2_bd.reshape(KV_WIDTH, KV_WIDTH).astype(BF16))


def _sel_map_t(ncp, s):
    nc = (s - CMP_LEN) // CMP_STRIDE + 1
    ns = s // SEL_BLOCK
    c0 = np.arange(nc) * CMP_STRIDE
    c1 = c0 + CMP_LEN
    s0 = np.arange(ns) * SEL_BLOCK
    s1 = s0 + SEL_BLOCK
    ov = np.clip(np.minimum(c1[:, None], s1[None, :]) - np.maximum(c0[:, None], s0[None, :]), 0, None)
    m = np.zeros((NS_PAD, ncp), np.float32)
    m[:ns, :nc] = (ov / CMP_LEN).T
    return jnp.asarray(m, dtype=BF16)


def kernel(x, mem, positions, g_norm, w_in, g_q_nsa, g_k_cmp, g_k_slc, g_k_win, cmp_pos_k, w_cmp_k1,
           w_cmp_k2, cmp_pos_v, w_cmp_v1, w_cmp_v2, w_pool, pool_scale, g_mem, w_mem_kv, g_q_mem,
           g_k_mem, w_out):
    b, s, d = x.shape
    depth = g_norm.shape[0]
    assert d == D_MODEL and s % WINDOW == 0 and s // SEL_BLOCK >= SEL_TOPK and s // SEL_BLOCK <= NS_PAD
    n = b * s
    ncp = s // CMP_STRIDE
    tm = 512
    t_cmp = 1024
    t_sel = 512
    t_win = WINDOW
    assert s % tm == 0 and tm % t_sel == 0 and tm == WINDOW

    inv_freq = ROPE_THETA ** (-jnp.arange(ROPE_HALF, dtype=F32) / ROPE_HALF)
    invf = inv_freq[(np.arange(LANES) % HEAD_DIM) % ROPE_HALF][None, :]
    bd512, bd256, bd128 = _block_diag_ones(512), _block_diag_ones(256), _block_diag_ones(128)
    cs = _rope_cs(positions.reshape(1, n), inv_freq[:, None], min(n, 8192))
    rope_e = _rope_expanders()
    end_idx = np.minimum(np.arange(ncp) * CMP_STRIDE + CMP_LEN - 1, s - 1)
    pend = positions[:, end_idx][:, :, None]
    selt = _sel_map_t(ncp, s)
    et = jnp.asarray((np.arange(s)[:, None] // SEL_BLOCK == np.arange(NS_PAD)[None, :]).astype(np.float32),
                     dtype=BF16)
    tile = lambda g, reps: jnp.tile(g[None, :], (1, reps))

    for l in range(depth):
        x2 = x.reshape(n, d)
        w_all = _pack_w_in(w_in[l])
        (q, kc_raw, vc_raw, ks, vs, kw, vw, sig, zn, vp, zp, qm, zm) = _in_proj(
            x2, cs, g_norm[l][None, :], w_all, bd512, bd256, tile(g_q_nsa[l], 8), tile(g_k_slc[l], 2),
            tile(g_k_win[l], 2), tile(g_q_mem[l], 4), rope_e, tm, t_win)

        cmp_consts = (_pack_cmp(cmp_pos_k[l], w_cmp_k1[l], w_cmp_k2[l])
                      + _pack_cmp(cmp_pos_v[l], w_cmp_v1[l], w_cmp_v2[l].T)
                      + (bd128, tile(g_k_cmp[l], 2), invf))
        kc, vc = _compress(kc_raw.reshape(b, s, KV_WIDTH), vc_raw.reshape(b, s, KV_WIDTH), pend, cmp_consts)

        q3 = q.reshape(b, s, NSA_WIDTH)
        sig3 = sig.reshape(b, s, 256)
        o_c, bias = _cmp_attention(q3, kc, vc, sig3, selt, t_cmp)
        dup4 = lambda a: a.reshape(NSA_KV_HEADS, b, s, LANES)
        vt5 = lambda a: a.reshape(NSA_KV_HEADS, b, s // tm, LANES, tm)
        o_w = _win_attention(q3, dup4(kw), vw, sig3, t_win)
        y_nsa = _sel_attention(q3, bias, et, dup4(ks), vt5(vs), sig3, o_c, o_w,
                               zn.reshape(b, s, NSA_WIDTH), t_sel)

        mk, mv_t = _mem_kv(mem, g_mem[l][None, :], w_mem_kv[l].astype(BF16), bd256, tile(g_k_mem[l], 4))

        wpool = jnp.zeros((4, HEAD_DIM, 4, HEAD_DIM), F32)
        for g in range(4):
            wpool = wpool.at[g, :, g, :].set(w_pool[l, g])
        out = _out_proj(x2, y_nsa.reshape(n, NSA_WIDTH), vp, zp, qm, mk, mv_t, zm, w_out[l].astype(BF16),
                        wpool.reshape(POOL_WIDTH, POOL_WIDTH).astype(BF16), pool_scale[l][None, :], s, tm)
        x = out.reshape(b, s, d)
    return x
```

```python
import functools

import numpy as np
import jax
import jax.numpy as jnp
from jax import lax
from jax.experimental import pallas as pl
from jax.experimental.pallas import tpu as pltpu

F32 = jnp.float32
BF16 = jnp.bfloat16

D_MODEL = 1024
HEAD_DIM = 64
NSA_HEADS = 8
NSA_KV_HEADS = 2
NSA_GROUP = NSA_HEADS // NSA_KV_HEADS
NSA_WIDTH = NSA_HEADS * HEAD_DIM
KV_WIDTH = NSA_KV_HEADS * HEAD_DIM
N_BRANCH = 3
CMP_LEN = 32
CMP_STRIDE = 16
SEL_BLOCK = 64
SEL_TOPK = 16
WINDOW = 512
FORCE_SCORE = 1e4
N_FORCED = 3
POOL_WINDOWS = (2, 4, 8, 16)
POOL_WIDTH = 256
MEM_HEADS = 4
MEM_WIDTH = MEM_HEADS * HEAD_DIM
ROPE_THETA = 500000.0
ROPE_DIM = HEAD_DIM // 4
ROPE_HALF = ROPE_DIM // 2
EPS = 1e-6
SCALE = HEAD_DIM ** -0.5
Q_SCALE = SCALE * float(np.log2(np.e))

LANES = 128
NS_PAD = LANES
MASK_BIAS = -1e9
POOL_HALO = 16
CMP_CHUNK = LANES
SEL_UNROLL = 4
ROPE_ROWS = 32
TINY = float(np.finfo(np.float32).tiny)

GROUP_WIDTH = NSA_GROUP * HEAD_DIM
GATE_WIDTH = NSA_KV_HEADS * LANES
SEL_BLOCK_LOG2 = SEL_BLOCK.bit_length() - 1
HEAD_DIM_LOG2 = HEAD_DIM.bit_length() - 1
assert 1 << SEL_BLOCK_LOG2 == SEL_BLOCK and 1 << HEAD_DIM_LOG2 == HEAD_DIM
V7X_VMEM_BYTES = 64 * 1024 * 1024
VMEM_LIMIT_BYTES = V7X_VMEM_BYTES * 7 // 8

_SEG_WIDTHS = (('q', NSA_WIDTH), ('kv', 2 * N_BRANCH * KV_WIDTH), ('gate', GATE_WIDTH), ('zn', NSA_WIDTH),
               ('vp', POOL_WIDTH), ('zp', POOL_WIDTH), ('qm', MEM_WIDTH), ('zm', MEM_WIDTH))
_SEG = {name: (sum(w for _, w in _SEG_WIDTHS[:k]), width) for k, (name, width) in enumerate(_SEG_WIDTHS)}

_NT = (((1,), (1,)), ((), ()))


def _dot(a, b):
    return jnp.dot(a, b, preferred_element_type=F32)


def _dot_nt(a, b):
    return lax.dot_general(a, b, _NT, preferred_element_type=F32)


def _vmem_params(sem):
    return pltpu.CompilerParams(dimension_semantics=sem, vmem_limit_bytes=VMEM_LIMIT_BYTES)


def _rope_tables(pos_f32, invf):
    ang = pos_f32 * invf
    c, s = jnp.cos(ang), jnp.sin(ang)
    d = lax.broadcasted_iota(jnp.int32, ang.shape, 1) % HEAD_DIM
    tc = jnp.where(d < ROPE_DIM, c, 1.0)
    ts_up = jnp.where((d >= ROPE_HALF) & (d < ROPE_DIM), s, 0.0)
    ts_dn = jnp.where(d < ROPE_HALF, -s, 0.0)
    return tc, ts_up, ts_dn


def _rope(x, tables):
    tc, ts_up, ts_dn = tables
    w = x.shape[1]
    reps = w // LANES
    if reps > 1:
        tc, ts_up, ts_dn = (jnp.concatenate([t] * reps, axis=1) for t in (tc, ts_up, ts_dn))
    return (x * tc + pltpu.roll(x, ROPE_HALF, 1) * ts_up
            + pltpu.roll(x, w - ROPE_HALF, 1) * ts_dn)


def _head_rms(x, ones_bd, gain):
    ss = _dot((x * x).astype(BF16), ones_bd)
    return x * lax.rsqrt(ss * (1.0 / HEAD_DIM) + EPS) * gain


def _dup_halves(x):
    low = lax.broadcasted_iota(jnp.int32, x.shape, 1) < HEAD_DIM
    r = pltpu.roll(x, HEAD_DIM, 1)
    return jnp.where(low, x, r), jnp.where(low, r, x)


def _values_t(v):
    return _values_rows(v.T)


def _values_rows(vt):
    vt = vt.astype(BF16)
    ones = jnp.ones((HEAD_DIM, vt.shape[1]), BF16)
    return (jnp.concatenate([vt[0:HEAD_DIM], ones], axis=0),
            jnp.concatenate([vt[HEAD_DIM:], ones], axis=0))


def _silu(z):
    return z * jax.nn.sigmoid(z)


def _in_proj_kernel(x_ref, cs_ref, gn_ref, w_ref, bd512_ref, bd256_ref, gq_ref, gks_ref, gkw_ref,
                    gqm_ref, rope_e_ref,
                    q_ref, kc_ref, vc_ref, ks_ref, vs_ref, kw_ref, vw_ref, sig_ref, zn_ref, vp_ref,
                    zp_ref, qm_ref, zm_ref):
    x = x_ref[...]
    h = x * lax.rsqrt(jnp.mean(x * x, axis=-1, keepdims=True) + EPS) * gn_ref[...]
    hb = h.astype(BF16)

    def seg(name):
        o, w = _SEG[name]
        return _dot(hb, w_ref[:, o:o + w])

    cs = cs_ref[...]
    cs_hi = cs.astype(BF16)
    cs_lo = (cs - cs_hi.astype(F32)).astype(BF16)
    tables = tuple((_dot(rope_e_ref[k], cs_hi) + _dot(rope_e_ref[k], cs_lo)).T for k in range(3))

    q = _head_rms(seg('q'), bd512_ref[...], gq_ref[...])
    q_ref[...] = (_rope(q, tables) * Q_SCALE).astype(BF16)

    kv_all = seg('kv')
    kv = lambda j: kv_all[:, j * KV_WIDTH:(j + 1) * KV_WIDTH]
    kc_ref[...] = kv(0)
    vc_ref[...] = kv(1)

    def norm_dup(k, gain):
        k = k * lax.rsqrt(jnp.mean(k * k, axis=-1, keepdims=True) + EPS) * gain
        return _rope(k, tables).astype(BF16)

    ks0, ks1 = _dup_halves(kv(2))
    ks_ref[0] = norm_dup(ks0, gks_ref[...])
    ks_ref[1] = norm_dup(ks1, gks_ref[...])
    vs_ref[0, 0], vs_ref[1, 0] = _values_t(kv(3))
    kw0, kw1 = _dup_halves(kv(4))
    kw_ref[0] = norm_dup(kw0, gkw_ref[...])
    kw_ref[1] = norm_dup(kw1, gkw_ref[...])
    vw = _values_t(kv(5))
    wt = vw_ref.shape[3]
    for g in range(NSA_KV_HEADS):
        for c in range(vw_ref.shape[1]):
            vw_ref[g, c] = vw[g][:, c * wt:(c + 1) * wt]

    sig_ref[...] = jax.nn.sigmoid(seg('gate'))
    zn_ref[...] = _silu(seg('zn')).astype(BF16)
    vp_ref[...] = seg('vp')
    zp_ref[...] = _silu(seg('zp')).astype(BF16)
    qm = _head_rms(seg('qm'), bd256_ref[...], gqm_ref[...])
    qm_ref[...] = (qm * Q_SCALE).astype(BF16)
    zm_ref[...] = _silu(seg('zm')).astype(BF16)


def _rope_cs_kernel(pos_ref, invf_ref, o_ref):
    ang = invf_ref[...] * pos_ref[...].astype(F32)
    tail = lax.broadcasted_iota(jnp.int32, (ROPE_ROWS - 2 * ROPE_HALF, ang.shape[1]), 0) == 0
    o_ref[...] = jnp.concatenate([jnp.cos(ang), jnp.sin(ang), tail.astype(F32)], axis=0)


def _rope_cs(pos_row, invf_col, blk):
    n = pos_row.shape[1]
    return pl.pallas_call(
        _rope_cs_kernel,
        grid=(n // blk,),
        in_specs=[pl.BlockSpec((1, blk), lambda i: (0, i)), pl.BlockSpec(invf_col.shape, lambda i: (0, 0))],
        out_specs=pl.BlockSpec((ROPE_ROWS, blk), lambda i: (0, i)),
        out_shape=jax.ShapeDtypeStruct((ROPE_ROWS, n), F32),
        compiler_params=_vmem_params(("parallel",)),
    )(pos_row, invf_col)


def _rope_expanders():
    e = np.zeros((3, LANES, ROPE_ROWS), np.float32)
    for lane in range(LANES):
        d = lane % HEAD_DIM
        if d < ROPE_DIM:
            e[0, lane, d % ROPE_HALF] = 1.0
        else:
            e[0, lane, 2 * ROPE_HALF] = 1.0
        if ROPE_HALF <= d < ROPE_DIM:
            e[1, lane, ROPE_HALF + d % ROPE_HALF] = 1.0
        if d < ROPE_HALF:
            e[2, lane, ROPE_HALF + d] = -1.0
    return jnp.asarray(e, dtype=BF16)


def _in_proj(x2, cs, gn, w_all, bd512, bd256, gq, gks, gkw, gqm, rope_e, tm, t_win):
    n = x2.shape[0]
    row = lambda w: pl.BlockSpec((tm, w), lambda i: (i, 0))
    full = lambda a: pl.BlockSpec(a.shape, lambda i: (0,) * a.ndim)
    dup = pl.BlockSpec((NSA_KV_HEADS, tm, LANES), lambda i: (0, i, 0))
    sd = jax.ShapeDtypeStruct
    vst = pl.BlockSpec((NSA_KV_HEADS, 1, LANES, tm), lambda i: (0, i, 0, 0))
    vwt = pl.BlockSpec((NSA_KV_HEADS, tm // t_win, LANES, t_win), lambda i: (0, i, 0, 0))
    kvh = NSA_KV_HEADS
    out_shape = [sd((n, NSA_WIDTH), BF16), sd((n, KV_WIDTH), F32), sd((n, KV_WIDTH), F32),
                 sd((kvh, n, LANES), BF16), sd((kvh, n // tm, LANES, tm), BF16), sd((kvh, n, LANES), BF16),
                 sd((kvh, n // t_win, LANES, t_win), BF16), sd((n, GATE_WIDTH), F32), sd((n, NSA_WIDTH), BF16),
                 sd((n, POOL_WIDTH), F32), sd((n, POOL_WIDTH), BF16), sd((n, MEM_WIDTH), BF16),
                 sd((n, MEM_WIDTH), BF16)]
    out_specs = [row(NSA_WIDTH), row(KV_WIDTH), row(KV_WIDTH), dup, vst, dup, vwt, row(GATE_WIDTH),
                 row(NSA_WIDTH), row(POOL_WIDTH), row(POOL_WIDTH), row(MEM_WIDTH), row(MEM_WIDTH)]
    consts = (gn, w_all, bd512, bd256, gq, gks, gkw, gqm, rope_e)
    return pl.pallas_call(
        _in_proj_kernel,
        grid=(n // tm,),
        in_specs=[row(D_MODEL), pl.BlockSpec((ROPE_ROWS, tm), lambda i: (0, i))] + [full(a) for a in consts],
        out_specs=out_specs,
        out_shape=out_shape,
        compiler_params=_vmem_params(("parallel",)),
    )(x2, cs, *consts)


def _compress_kernel(zk_ref, zv_ref, pend_ref, pak_ref, pbk_ref, w1ak_ref, w1bk_ref, w2k_ref,
                     pav_ref, pbv_ref, w1av_ref, w1bv_ref, w2v_ref, bd128_ref, gk_ref, invf_ref,
                     kc_ref, vc_ref):
    ncp = zk_ref.shape[1] // CMP_STRIDE

    def mlp(z_ref, pa, pb, w1a, w1b, w2, transposed=False):
        a = jnp.zeros((ncp, KV_WIDTH), F32)
        b = jnp.zeros((ncp, KV_WIDTH), F32)
        for o in range(CMP_STRIDE):
            z = z_ref[0, pl.ds(o, ncp, stride=CMP_STRIDE), :]
            a = a + _dot((z + pa[o:o + 1, :]).astype(BF16), w1a[o])
            b = b + _dot((z + pb[o:o + 1, :]).astype(BF16), w1b[o])
        pre = a + pltpu.roll(b, ncp - 1, 0)
        hid = jax.nn.gelu(pre).astype(BF16)
        return _dot_nt(w2, hid) if transposed else _dot(hid, w2)

    kc = mlp(zk_ref, pak_ref, pbk_ref, w1ak_ref, w1bk_ref, w2k_ref[...])
    kc = _head_rms(kc, bd128_ref[...], gk_ref[...])
    kc = _rope(kc, _rope_tables(pend_ref[0].astype(F32), invf_ref[...]))
    k0, k1 = _dup_halves(kc)
    kc_ref[0, 0] = k0.astype(BF16)
    kc_ref[0, 1] = k1.astype(BF16)
    vc_t = mlp(zv_ref, pav_ref, pbv_ref, w1av_ref, w1bv_ref, w2v_ref[...], True)
    vc_ref[0, 0], vc_ref[0, 1] = _values_rows(vc_t)


def _compress(zk, zv, pend, consts):
    b, s, zw = zk.shape
    ncp = s // CMP_STRIDE
    full = lambda a: pl.BlockSpec(a.shape, lambda i: (0,) * a.ndim)
    zspec = pl.BlockSpec((1, s, zw), lambda i: (i, 0, 0))
    ospec = pl.BlockSpec((1, NSA_KV_HEADS, ncp, LANES), lambda i: (i, 0, 0, 0))
    sd = jax.ShapeDtypeStruct((b, NSA_KV_HEADS, ncp, LANES), BF16)
    return pl.pallas_call(
        _compress_kernel,
        grid=(b,),
        in_specs=[zspec, zspec, pl.BlockSpec((1, ncp, 1), lambda i: (i, 0, 0))] + [full(a) for a in consts],
        out_specs=[ospec, pl.BlockSpec((1, NSA_KV_HEADS, LANES, ncp), lambda i: (i, 0, 0, 0))],
        out_shape=[sd, jax.ShapeDtypeStruct((b, NSA_KV_HEADS, LANES, ncp), BF16)],
        compiler_params=_vmem_params(("parallel",)),
    )(zk, zv, pend, *consts)


def _gate_col(sig_blk, r, branch):
    c = branch * NSA_GROUP + r
    return sig_blk[:, c:c + 1]


def _heads_t(q_blk):
    out = []
    for rp in range(NSA_GROUP // 2):
        pair_t = q_blk[:, rp * LANES:(rp + 1) * LANES].astype(F32).T
        low = lax.broadcasted_iota(jnp.int32, pair_t.shape, 0) < HEAD_DIM
        out.append(jnp.where(low, pair_t, 0.0).astype(BF16))
        out.append(jnp.where(low, 0.0, pair_t).astype(BF16))
    return out


def _finish_heads(accs, sig, branch):
    outs = []
    for r, acc in enumerate(accs):
        a = acc.T
        o = a / jnp.maximum(pltpu.roll(a, HEAD_DIM, 1), TINY)
        outs.append(o if sig is None else o * _gate_col(sig, r, branch))
    low = lax.broadcasted_iota(jnp.int32, outs[0].shape, 1) < HEAD_DIM
    return jnp.concatenate(
        [jnp.where(low, outs[0], pltpu.roll(outs[1], HEAD_DIM, 1)),
         jnp.where(low, outs[2], pltpu.roll(outs[3], HEAD_DIM, 1))], axis=1)


def _cmp_kernel(q_ref, kc_ref, vc_ref, sig_ref, selt_ref, o_ref, bias_ref, psum_ref, acc_ref):
    tq = q_ref.shape[1]
    ncp = kc_ref.shape[2]
    t0 = pl.program_id(2) * tq
    q_heads = _heads_t(q_ref[0])

    def attend(nk):
        kc = kc_ref[0, 0, 0:nk, :]
        vc_t = vc_ref[0, 0, :, 0:nk]
        t = t0 + lax.broadcasted_iota(jnp.int32, (nk, tq), 1)
        n = lax.broadcasted_iota(jnp.int32, (nk, tq), 0)
        cmask = n * CMP_STRIDE + (CMP_LEN - 1) <= t
        for r, q_t in enumerate(q_heads):
            s = jnp.where(cmask, _dot(kc, q_t), -jnp.inf)
            m = jnp.max(s, axis=0, keepdims=True)
            m = jnp.where(m == -jnp.inf, 0.0, m)
            e = jnp.exp2(s - m)
            p = e * (1.0 / jnp.maximum(jnp.sum(e, axis=0, keepdims=True), TINY))
            psum_ref[0:nk, :] = p if r == 0 else psum_ref[0:nk, :] + p
            acc_ref[r] = _dot(vc_t, e.astype(BF16))
        if nk < ncp:
            psum_ref[nk:ncp, :] = jnp.zeros((ncp - nk, tq), F32)

    last_block = (t0 + tq - CMP_LEN) // CMP_STRIDE
    n_chunks = jnp.minimum(last_block // CMP_CHUNK + 1, ncp // CMP_CHUNK)
    for v in range(1, ncp // CMP_CHUNK + 1):
        pl.when(n_chunks == v)(functools.partial(attend, v * CMP_CHUNK))
    o_ref[0] = _finish_heads([acc_ref[r] for r in range(NSA_GROUP)], sig_ref[0], 0).astype(BF16)

    psum = psum_ref[...]
    ps_hi = psum.astype(BF16)
    ps_lo = (psum - ps_hi.astype(F32)).astype(BF16)
    selt = selt_ref[...]
    imp = _dot(selt, ps_hi) + _dot(selt, ps_lo)
    j = lax.broadcasted_iota(jnp.int32, (NS_PAD, tq), 0)
    cur = (t0 + lax.broadcasted_iota(jnp.int32, (NS_PAD, tq), 1)) >> SEL_BLOCK_LOG2
    forced = (j == 0) | (j == cur) | (j == cur - 1)
    valid = j <= cur
    vals = jnp.where(forced, -jnp.inf, jnp.where(valid, imp, -1.0))
    neg_j = -j.astype(F32)
    for _ in range(SEL_TOPK - N_FORCED):
        m = jnp.max(vals, axis=0, keepdims=True)
        first = jnp.max(jnp.where(vals == m, neg_j, -jnp.inf), axis=0, keepdims=True)
        vals = jnp.where(neg_j == first, -jnp.inf, vals)
    bias_t = jnp.where((vals == -jnp.inf) & valid, 0.0, MASK_BIAS)
    bias_ref[0, 0] = bias_t.astype(BF16)


def _cmp_attention(q3, kc, vc, sig3, selt, tq):
    b, s, _ = q3.shape
    ncp = kc.shape[2]
    assert ncp % CMP_CHUNK == 0 and tq % SEL_BLOCK == 0
    grid = (b, NSA_KV_HEADS, s // tq)
    return pl.pallas_call(
        _cmp_kernel,
        grid=grid,
        in_specs=[pl.BlockSpec((1, tq, GROUP_WIDTH), lambda bi, g, i: (bi, i, g)),
                  pl.BlockSpec((1, 1, ncp, LANES), lambda bi, g, i: (bi, g, 0, 0)),
                  pl.BlockSpec((1, 1, LANES, ncp), lambda bi, g, i: (bi, g, 0, 0)),
                  pl.BlockSpec((1, tq, LANES), lambda bi, g, i: (bi, i, g)),
                  pl.BlockSpec(selt.shape, lambda bi, g, i: (0, 0))],
        out_specs=[pl.BlockSpec((1, tq, GROUP_WIDTH), lambda bi, g, i: (bi, i, g)),
                   pl.BlockSpec((1, 1, NS_PAD, tq), lambda bi, g, i: (bi, g, 0, i))],
        out_shape=[jax.ShapeDtypeStruct((b, s, NSA_WIDTH), BF16),
                   jax.ShapeDtypeStruct((b, NSA_KV_HEADS, NS_PAD, s), BF16)],
        scratch_shapes=[pltpu.VMEM((ncp, tq), F32), pltpu.VMEM((NSA_GROUP, LANES, tq), F32)],
        compiler_params=_vmem_params(("parallel", "parallel", "parallel")),
    )(q3, kc, vc, sig3, selt)


def _sel_kernel(q_ref, bias_ref, et_ref, ks_ref, vs_ref, sig_ref, oc_ref, ow_ref, zn_ref, o_ref,
                qaug_ref, s_ref, m_ref, acc_ref):
    tq = q_ref.shape[1]
    tk = vs_ref.shape[4]
    cols = NSA_GROUP * tq
    qt = pl.program_id(2)

    bias = bias_ref[0, 0]
    for r, q_t in enumerate(_heads_t(q_ref[0])):
        qaug_ref[0:LANES, r * tq:(r + 1) * tq] = bias
        qaug_ref[LANES:2 * LANES, r * tq:(r + 1) * tq] = q_t
    m_ref[...] = jnp.full(m_ref.shape, -jnp.inf, F32)
    acc_ref[...] = jnp.zeros(acc_ref.shape, F32)
    heads = [slice(r * tq, (r + 1) * tq) for r in range(NSA_GROUP)]

    def keys(kt):
        k0 = pl.multiple_of(kt * tk, tk)
        return jnp.concatenate([et_ref[pl.ds(k0, tk), :], ks_ref[0, 0, pl.ds(k0, tk), :]], axis=1)

    def softmax_pv(kt, h, s):
        rows = s.shape[0]
        m_prev = m_ref[:, h]
        m_new = jnp.maximum(m_prev, jnp.max(s, axis=0, keepdims=True))
        p = jnp.exp2(s - m_new).astype(BF16)
        acc_ref[:, h] = jnp.exp2(m_prev - m_new) * acc_ref[:, h] + _dot(vs_ref[0, 0, kt, :, 0:rows], p)
        m_ref[:, h] = m_new

    k_first = keys(0)
    for h in heads:
        s_ref[:, h] = _dot(k_first, qaug_ref[:, h])

    def step(kt):
        k_next = keys(kt + 1)
        for h in heads:
            s = s_ref[:, h]
            s_next = _dot(k_next, qaug_ref[:, h])
            softmax_pv(kt, h, s)
            s_ref[:, h] = s_next

    def unrolled_steps(j, first):
        for u in range(SEL_UNROLL):
            step(first + SEL_UNROLL * j + u)
        return first

    def two_steps(first):
        step(first)
        step(first + 1)

    last = (qt * tq) // tk
    rest = last % SEL_UNROLL
    pl.when(rest % 2 == 1)(lambda: step(0))
    pl.when(rest >= 2)(functools.partial(two_steps, rest % 2))
    lax.fori_loop(0, last // SEL_UNROLL, unrolled_steps, rest)

    def last_tile(rows):
        kpos = last * tk + lax.broadcasted_iota(jnp.int32, (rows, tq), 0)
        causal = kpos <= qt * tq + lax.broadcasted_iota(jnp.int32, (rows, tq), 1)
        for h in heads:
            softmax_pv(last, h, jnp.where(causal, s_ref[0:rows, h], MASK_BIAS))

    visible = (qt + 1) * tq - last * tk
    for rows in range(tq, tk + 1, tq):
        pl.when(visible == rows)(functools.partial(last_tile, rows))

    acc = acc_ref[...]
    o_sel = _finish_heads([acc[:, h] for h in heads], sig_ref[0], 1)
    o_nsa = oc_ref[0].astype(F32) + o_sel + ow_ref[0].astype(F32)
    o_ref[0] = (o_nsa * zn_ref[0].astype(F32)).astype(BF16)


def _sel_attention(q3, bias, et, ks, vst, sig3, o_c, o_w, zn3, tq):
    b, s, _ = q3.shape
    tk = vst.shape[4]
    cols = NSA_GROUP * tq
    assert SEL_UNROLL == 4 and tk % tq == 0
    group_blk = pl.BlockSpec((1, tq, GROUP_WIDTH), lambda bi, g, i: (bi, i, g))
    return pl.pallas_call(
        _sel_kernel,
        grid=(b, NSA_KV_HEADS, s // tq),
        in_specs=[group_blk,
                  pl.BlockSpec((1, 1, NS_PAD, tq), lambda bi, g, i: (bi, g, 0, i)),
                  pl.BlockSpec((s, LANES), lambda bi, g, i: (0, 0)),
                  pl.BlockSpec((1, 1, s, LANES), lambda bi, g, i: (g, bi, 0, 0)),
                  pl.BlockSpec((1, 1, s // tk, LANES, tk), lambda bi, g, i: (g, bi, 0, 0, 0)),
                  pl.BlockSpec((1, tq, LANES), lambda bi, g, i: (bi, i, g)),
                  group_blk, group_blk, group_blk],
        out_specs=group_blk,
        out_shape=jax.ShapeDtypeStruct((b, s, NSA_WIDTH), BF16),
        scratch_shapes=[pltpu.VMEM((2 * LANES, cols), BF16), pltpu.VMEM((tk, cols), F32),
                        pltpu.VMEM((1, cols), F32), pltpu.VMEM((LANES, cols), F32)],
        compiler_params=_vmem_params(("parallel", "parallel", "parallel")),
    )(q3, bias, et, ks, vst, sig3, o_c, o_w, zn3)


def _win_kernel(q_ref, kp_ref, kc_ref, vp_ref, vc_ref, sig_ref, o_ref):
    tq = q_ref.shape[1]
    qt = pl.program_id(1)
    c = lax.broadcasted_iota(jnp.int32, (tq, tq), 0)
    i = lax.broadcasted_iota(jnp.int32, (tq, tq), 1)
    mask_cur = c <= i
    mask_prev = c > i + jnp.where(qt > 0, 0, tq)
    for g in range(NSA_KV_HEADS):
        lanes = slice(g * GROUP_WIDTH, (g + 1) * GROUP_WIDTH)
        accs = []
        for q_t in _heads_t(q_ref[0, :, lanes]):
            s_p = jnp.where(mask_prev, _dot(kp_ref[g, 0], q_t), -jnp.inf)
            s_c = jnp.where(mask_cur, _dot(kc_ref[g, 0], q_t), -jnp.inf)
            m = jnp.maximum(jnp.max(s_p, axis=0, keepdims=True), jnp.max(s_c, axis=0, keepdims=True))
            e_p = jnp.exp2(s_p - m).astype(BF16)
            e_c = jnp.exp2(s_c - m).astype(BF16)
            accs.append(_dot(vp_ref[g, 0], e_p) + _dot(vc_ref[g, 0], e_c))
        o_ref[0, :, lanes] = _finish_heads(accs, sig_ref[0, :, g * LANES:(g + 1) * LANES], 2).astype(BF16)


def _win_attention(q3, kw, vw, sig3, tq):
    b, s, _ = q3.shape
    assert tq == WINDOW
    nt = s // tq
    kvh = NSA_KV_HEADS
    kspec = lambda back: pl.BlockSpec((kvh, 1, tq, LANES), lambda bi, i: (0, bi, jnp.maximum(i - back, 0), 0))
    vspec = lambda back: pl.BlockSpec((kvh, 1, LANES, tq),
                                      lambda bi, i: (0, bi * nt + jnp.maximum(i - back, 0), 0, 0))
    return pl.pallas_call(
        _win_kernel,
        grid=(b, nt),
        in_specs=[pl.BlockSpec((1, tq, NSA_WIDTH), lambda bi, i: (bi, i, 0)), kspec(1), kspec(0),
                  vspec(1), vspec(0), pl.BlockSpec((1, tq, GATE_WIDTH), lambda bi, i: (bi, i, 0))],
        out_specs=pl.BlockSpec((1, tq, NSA_WIDTH), lambda bi, i: (bi, i, 0)),
        out_shape=jax.ShapeDtypeStruct((b, s, NSA_WIDTH), BF16),
        compiler_params=_vmem_params(("parallel", "parallel")),
    )(q3, kw, kw, vw, vw, sig3)


def _mem_kv_kernel(mem_ref, gm_ref, w_ref, bd256_ref, gk_ref, mk_ref, mv_ref):
    x = mem_ref[0]
    h = x * lax.rsqrt(jnp.mean(x * x, axis=-1, keepdims=True) + EPS) * gm_ref[...]
    mkv = _dot(h.astype(BF16), w_ref[...])
    mk_ref[0] = _head_rms(mkv[:, 0:MEM_WIDTH], bd256_ref[...], gk_ref[...]).astype(BF16)
    for hp in range(MEM_HEADS // 2):
        lo = MEM_WIDTH + hp * LANES
        mv_ref[0, 2 * hp], mv_ref[0, 2 * hp + 1] = _values_t(mkv[:, lo:lo + LANES])


def _mem_kv(mem, gm, w, bd256, gk):
    b, m, _ = mem.shape
    full = lambda a: pl.BlockSpec(a.shape, lambda i: (0,) * a.ndim)
    return pl.pallas_call(
        _mem_kv_kernel,
        grid=(b,),
        in_specs=[pl.BlockSpec((1, m, D_MODEL), lambda i: (i, 0, 0)), full(gm), full(w), full(bd256), full(gk)],
        out_specs=[pl.BlockSpec((1, m, MEM_WIDTH), lambda i: (i, 0, 0)),
                   pl.BlockSpec((1, MEM_HEADS, LANES, m), lambda i: (i, 0, 0, 0))],
        out_shape=[jax.ShapeDtypeStruct((b, m, MEM_WIDTH), BF16),
                   jax.ShapeDtypeStruct((b, MEM_HEADS, LANES, m), BF16)],
        compiler_params=_vmem_params(("parallel",)),
    )(mem, gm, w, bd256, gk)


def _mem_attention_block(q_blk, mk, mv_t):
    accs = []
    for h, q_t in enumerate(_heads_t(q_blk)):
        hp = h // 2
        s = _dot(mk[:, hp * LANES:(hp + 1) * LANES], q_t)
        e = jnp.exp2(s - jnp.max(s, axis=0, keepdims=True))
        accs.append(_dot(mv_t[h], e.astype(BF16)))
    return _finish_heads(accs, None, 0)


def _out_kernel(seq_len, x_ref, ynsa_ref, vp_ref, halo_ref, zp_ref, qm_ref, mk_ref, mv_ref, zm_ref, wo_ref,
                wpool_ref, pscale_ref, o_ref):
    tm = x_ref.shape[0]
    t0 = (pl.program_id(0) * tm) % seq_len

    v = vp_ref[...]
    halo = halo_ref[...] * (t0 > 0).astype(F32)
    e = jnp.concatenate([halo, v], axis=0)
    w2 = e + pltpu.roll(e, 1, 0)
    w4 = w2 + pltpu.roll(w2, 2, 0)
    w8 = w4 + pltpu.roll(w4, 4, 0)
    w16 = w8 + pltpu.roll(w8, 8, 0)
    shape = (tm, POOL_WIDTH)
    grp = lax.broadcasted_iota(jnp.int32, shape, 1) >> HEAD_DIM_LOG2
    wsum = jnp.where(grp == 0, w2[POOL_HALO:],
                     jnp.where(grp == 1, w4[POOL_HALO:], jnp.where(grp == 2, w8[POOL_HALO:], w16[POOL_HALO:])))
    width = jnp.left_shift(2, grp)
    cnt = jnp.minimum(t0 + lax.broadcasted_iota(jnp.int32, shape, 0) + 1, width).astype(F32)
    pooled = wsum / cnt - v
    y_pool = _dot(pooled.astype(BF16), wpool_ref[...]) * pscale_ref[...] * zp_ref[...].astype(F32)

    y_mem = _mem_attention_block(qm_ref[...], mk_ref[0], mv_ref[0]) * zm_ref[...].astype(F32)
    pool_lo, mem_lo = NSA_WIDTH, NSA_WIDTH + POOL_WIDTH
    out = x_ref[...] + _dot(ynsa_ref[...], wo_ref[0:pool_lo, :])
    out = out + _dot(y_pool.astype(BF16), wo_ref[pool_lo:mem_lo, :])
    out = out + _dot(y_mem.astype(BF16), wo_ref[mem_lo:mem_lo + MEM_WIDTH, :])
    o_ref[...] = out


def _out_proj(x2, ynsa, vp, zp, qm, mk, mv_t, zm, wo, wpool, pscale, seq_len, tm):
    n = x2.shape[0]
    row = lambda w: pl.BlockSpec((tm, w), lambda i: (i, 0))
    full = lambda a: pl.BlockSpec(a.shape, lambda i: (0,) * a.ndim)
    per_batch = lambda a: pl.BlockSpec((1,) + a.shape[1:],
                                       lambda i: ((i * tm) // seq_len,) + (0,) * (a.ndim - 1))
    hb = tm // POOL_HALO
    halo = pl.BlockSpec((POOL_HALO, POOL_WIDTH), lambda i: (jnp.maximum(i * hb - 1, 0), 0))
    return pl.pallas_call(
        functools.partial(_out_kernel, seq_len),
        grid=(n // tm,),
        in_specs=[row(D_MODEL), row(NSA_WIDTH), row(POOL_WIDTH), halo, row(POOL_WIDTH), row(MEM_WIDTH),
                  per_batch(mk), per_batch(mv_t), row(MEM_WIDTH), full(wo), full(wpool), full(pscale)],
        out_specs=row(D_MODEL),
        out_shape=jax.ShapeDtypeStruct((n, D_MODEL), F32),
        compiler_params=_vmem_params(("parallel",)),
    )(x2, ynsa, vp, vp, zp, qm, mk, mv_t, zm, wo, wpool, pscale)


def _block_diag_ones(width):
    i = np.arange(width) // HEAD_DIM
    return jnp.asarray((i[:, None] == i[None, :]).astype(np.float32), dtype=BF16)


def _pack_w_in(w_in):
    q, kv, gate, zn, vp, zp, qm, zm = jnp.split(
        w_in, np.cumsum([NSA_WIDTH, 2 * N_BRANCH * KV_WIDTH, NSA_HEADS * N_BRANCH, NSA_WIDTH, POOL_WIDTH,
                         POOL_WIDTH, MEM_WIDTH, MEM_WIDTH])[:-1].tolist(), axis=1)
    cols = np.zeros((NSA_KV_HEADS, LANES), np.int64)
    used = np.zeros((NSA_KV_HEADS, LANES), bool)
    for g in range(NSA_KV_HEADS):
        for k in range(N_BRANCH):
            for r in range(NSA_GROUP):
                cols[g, k * NSA_GROUP + r] = (g * NSA_GROUP + r) * N_BRANCH + k
                used[g, k * NSA_GROUP + r] = True
    gate_p = jnp.where(jnp.asarray(used.reshape(-1))[None, :], gate[:, cols.reshape(-1)], 0.0)
    return jnp.concatenate([q, kv, gate_p, zn, vp, zp, qm, zm], axis=1).astype(BF16)


def _pack_cmp(pos, w1, w2):
    half = CMP_LEN // 2

    def w1_half(wh):
        z = jnp.zeros((half, NSA_KV_HEADS, HEAD_DIM, NSA_KV_HEADS, HEAD_DIM), F32)
        for g in range(NSA_KV_HEADS):
            z = z.at[:, g, :, g, :].set(wh)
        return z.reshape(half, KV_WIDTH, KV_WIDTH).astype(BF16)

    def pos_half(ph):
        return jnp.tile(ph[:, None, :], (1, NSA_KV_HEADS, 1)).reshape(half, KV_WIDTH)

    w2_bd = jnp.zeros((NSA_KV_HEADS, HEAD_DIM, NSA_KV_HEADS, HEAD_DIM), F32)
    for g in range(NSA_KV_HEADS):
        w2_bd = w2_bd.at[g, :, g, :].set(w2)
    return (pos_half(pos[:half]), pos_half(pos[half:]), w1_half(w1[:half]), w1_half(w1[half:]),
            w2_bd.reshape(KV_WIDTH, KV_WIDTH).astype(BF16))


def _sel_map_t(ncp, s):
    nc = (s - CMP_LEN) // CMP_STRIDE + 1
    ns = s // SEL_BLOCK
    c0 = np.arange(nc) * CMP_STRIDE
    c1 = c0 + CMP_LEN
    s0 = np.arange(ns) * SEL_BLOCK
    s1 = s0 + SEL_BLOCK
    ov = np.clip(np.minimum(c1[:, None], s1[None, :]) - np.maximum(c0[:, None], s0[None, :]), 0, None)
    m = np.zeros((NS_PAD, ncp), np.float32)
    m[:ns, :nc] = (ov / CMP_LEN).T
    return jnp.asarray(m, dtype=BF16)


def kernel(x, mem, positions, g_norm, w_in, g_q_nsa, g_k_cmp, g_k_slc, g_k_win, cmp_pos_k, w_cmp_k1,
           w_cmp_k2, cmp_pos_v, w_cmp_v1, w_cmp_v2, w_pool, pool_scale, g_mem, w_mem_kv, g_q_mem,
           g_k_mem, w_out):
    b, s, d = x.shape
    depth = g_norm.shape[0]
    assert d == D_MODEL and s % WINDOW == 0 and s // SEL_BLOCK >= SEL_TOPK and s // SEL_BLOCK <= NS_PAD
    n = b * s
    ncp = s // CMP_STRIDE
    tm = 512
    t_cmp = 1024
    t_sel = 512
    t_win = WINDOW
    assert s % tm == 0 and tm % t_sel == 0 and tm == WINDOW

    inv_freq = ROPE_THETA ** (-jnp.arange(ROPE_HALF, dtype=F32) / ROPE_HALF)
    invf = inv_freq[(np.arange(LANES) % HEAD_DIM) % ROPE_HALF][None, :]
    bd512, bd256, bd128 = (_block_diag_ones(w) for w in (NSA_WIDTH, MEM_WIDTH, KV_WIDTH))
    cs = _rope_cs(positions.reshape(1, n), inv_freq[:, None], min(n, 8192))
    rope_e = _rope_expanders()
    end_idx = np.minimum(np.arange(ncp) * CMP_STRIDE + CMP_LEN - 1, s - 1)
    pend = positions[:, end_idx][:, :, None]
    selt = _sel_map_t(ncp, s)
    et = jnp.asarray((np.arange(s)[:, None] // SEL_BLOCK == np.arange(NS_PAD)[None, :]).astype(np.float32),
                     dtype=BF16)
    tile = lambda g, reps: jnp.tile(g[None, :], (1, reps))

    for l in range(depth):
        x2 = x.reshape(n, d)
        w_all = _pack_w_in(w_in[l])
        (q, kc_raw, vc_raw, ks, vs, kw, vw, sig, zn, vp, zp, qm, zm) = _in_proj(
            x2, cs, g_norm[l][None, :], w_all, bd512, bd256, tile(g_q_nsa[l], 8), tile(g_k_slc[l], 2),
            tile(g_k_win[l], 2), tile(g_q_mem[l], 4), rope_e, tm, t_win)

        cmp_consts = (_pack_cmp(cmp_pos_k[l], w_cmp_k1[l], w_cmp_k2[l])
                      + _pack_cmp(cmp_pos_v[l], w_cmp_v1[l], w_cmp_v2[l].T)
                      + (bd128, tile(g_k_cmp[l], 2), invf))
        kc, vc = _compress(kc_raw.reshape(b, s, KV_WIDTH), vc_raw.reshape(b, s, KV_WIDTH), pend, cmp_consts)

        q3 = q.reshape(b, s, NSA_WIDTH)
        sig3 = sig.reshape(b, s, GATE_WIDTH)
        o_c, bias = _cmp_attention(q3, kc, vc, sig3, selt, t_cmp)
        dup4 = lambda a: a.reshape(NSA_KV_HEADS, b, s, LANES)
        vt5 = lambda a: a.reshape(NSA_KV_HEADS, b, s // tm, LANES, tm)
        o_w = _win_attention(q3, dup4(kw), vw, sig3, t_win)
        y_nsa = _sel_attention(q3, bias, et, dup4(ks), vt5(vs), sig3, o_c, o_w,
                               zn.reshape(b, s, NSA_WIDTH), t_sel)

        mk, mv_t = _mem_kv(mem, g_mem[l][None, :], w_mem_kv[l].astype(BF16), bd256, tile(g_k_mem[l], 4))

        wpool = jnp.zeros((4, HEAD_DIM, 4, HEAD_DIM), F32)
        for g in range(4):
            wpool = wpool.at[g, :, g, :].set(w_pool[l, g])
        out = _out_proj(x2, y_nsa.reshape(n, NSA_WIDTH), vp, zp, qm, mk, mv_t, zm, w_out[l].astype(BF16),
                        wpool.reshape(POOL_WIDTH, POOL_WIDTH).astype(BF16), pool_scale[l][None, :], s, tm)
        x = out.reshape(b, s, d)
    return x
```

```python
import functools

import numpy as np
import jax
import jax.numpy as jnp
from jax import lax
from jax.experimental import pallas as pl
from jax.experimental.pallas import tpu as pltpu

F32 = jnp.float32
BF16 = jnp.bfloat16

D_MODEL = 1024
HEAD_DIM = 64
NSA_HEADS = 8
NSA_KV_HEADS = 2
NSA_GROUP = NSA_HEADS // NSA_KV_HEADS
NSA_WIDTH = NSA_HEADS * HEAD_DIM
KV_WIDTH = NSA_KV_HEADS * HEAD_DIM
N_BRANCH = 3
CMP_LEN = 32
CMP_STRIDE = 16
SEL_BLOCK = 64
SEL_TOPK = 16
WINDOW = 512
FORCE_SCORE = 1e4
N_FORCED = 3
POOL_WINDOWS = (2, 4, 8, 16)
POOL_WIDTH = 256
MEM_HEADS = 4
MEM_WIDTH = MEM_HEADS * HEAD_DIM
ROPE_THETA = 500000.0
ROPE_DIM = HEAD_DIM // 4
ROPE_HALF = ROPE_DIM // 2
EPS = 1e-6
SCALE = HEAD_DIM ** -0.5
Q_SCALE = SCALE * float(np.log2(np.e))

LANES = 128
NS_PAD = LANES
MASK_BIAS = -1e30
POOL_HALO = 16
CMP_CHUNK = LANES
SEL_UNROLL = 4
ROPE_ROWS = 32
TINY = float(np.finfo(np.float32).tiny)

GROUP_WIDTH = NSA_GROUP * HEAD_DIM
GATE_WIDTH = NSA_KV_HEADS * LANES
SEL_BLOCK_LOG2 = SEL_BLOCK.bit_length() - 1
HEAD_DIM_LOG2 = HEAD_DIM.bit_length() - 1
assert 1 << SEL_BLOCK_LOG2 == SEL_BLOCK and 1 << HEAD_DIM_LOG2 == HEAD_DIM
V7X_VMEM_BYTES = 64 * 1024 * 1024
VMEM_LIMIT_BYTES = V7X_VMEM_BYTES * 7 // 8

_SEG_WIDTHS = (('q', NSA_WIDTH), ('kv', 2 * N_BRANCH * KV_WIDTH), ('gate', GATE_WIDTH), ('zn', NSA_WIDTH),
               ('vp', POOL_WIDTH), ('zp', POOL_WIDTH), ('qm', MEM_WIDTH), ('zm', MEM_WIDTH))
_SEG = {name: (sum(w for _, w in _SEG_WIDTHS[:k]), width) for k, (name, width) in enumerate(_SEG_WIDTHS)}

_NT = (((1,), (1,)), ((), ()))


def _dot(a, b):
    return jnp.dot(a, b, preferred_element_type=F32)


def _dot_nt(a, b):
    return lax.dot_general(a, b, _NT, preferred_element_type=F32)


def _vmem_params(sem):
    return pltpu.CompilerParams(dimension_semantics=sem, vmem_limit_bytes=VMEM_LIMIT_BYTES)


def _rope_tables(pos_f32, invf):
    ang = pos_f32 * invf
    c, s = jnp.cos(ang), jnp.sin(ang)
    d = lax.broadcasted_iota(jnp.int32, ang.shape, 1) % HEAD_DIM
    tc = jnp.where(d < ROPE_DIM, c, 1.0)
    ts_up = jnp.where((d >= ROPE_HALF) & (d < ROPE_DIM), s, 0.0)
    ts_dn = jnp.where(d < ROPE_HALF, -s, 0.0)
    return tc, ts_up, ts_dn


def _rope(x, tables):
    tc, ts_up, ts_dn = tables
    w = x.shape[1]
    reps = w // LANES
    if reps > 1:
        tc, ts_up, ts_dn = (jnp.concatenate([t] * reps, axis=1) for t in (tc, ts_up, ts_dn))
    return (x * tc + pltpu.roll(x, ROPE_HALF, 1) * ts_up
            + pltpu.roll(x, w - ROPE_HALF, 1) * ts_dn)


def _head_rms(x, ones_bd, gain):
    ss = _dot((x * x).astype(BF16), ones_bd)
    return x * lax.rsqrt(ss * (1.0 / HEAD_DIM) + EPS) * gain


def _dup_halves(x):
    low = lax.broadcasted_iota(jnp.int32, x.shape, 1) < HEAD_DIM
    r = pltpu.roll(x, HEAD_DIM, 1)
    return jnp.where(low, x, r), jnp.where(low, r, x)


def _values_t(v):
    return _values_rows(v.T)


def _values_rows(vt):
    vt = vt.astype(BF16)
    ones = jnp.ones((HEAD_DIM, vt.shape[1]), BF16)
    return (jnp.concatenate([vt[0:HEAD_DIM], ones], axis=0),
            jnp.concatenate([vt[HEAD_DIM:], ones], axis=0))


def _silu(z):
    return z * jax.nn.sigmoid(z)


def _in_proj_kernel(x_ref, cs_ref, gn_ref, w_ref, bd512_ref, bd256_ref, gq_ref, gks_ref, gkw_ref,
                    gqm_ref, rope_e_ref,
                    q_ref, kc_ref, vc_ref, ks_ref, vs_ref, kw_ref, vw_ref, sig_ref, zn_ref, vp_ref,
                    zp_ref, qm_ref, zm_ref):
    x = x_ref[...]
    h = x * lax.rsqrt(jnp.mean(x * x, axis=-1, keepdims=True) + EPS) * gn_ref[...]
    hb = h.astype(BF16)

    def seg(name):
        o, w = _SEG[name]
        return _dot(hb, w_ref[:, o:o + w])

    cs = cs_ref[...]
    cs_hi = cs.astype(BF16)
    cs_lo = (cs - cs_hi.astype(F32)).astype(BF16)
    tables = tuple((_dot(rope_e_ref[k], cs_hi) + _dot(rope_e_ref[k], cs_lo)).T for k in range(3))

    q = _head_rms(seg('q'), bd512_ref[...], gq_ref[...])
    q_ref[...] = (_rope(q, tables) * Q_SCALE).astype(BF16)

    kv_all = seg('kv')
    kv = lambda j: kv_all[:, j * KV_WIDTH:(j + 1) * KV_WIDTH]
    kc_ref[...] = kv(0)
    vc_ref[...] = kv(1)

    def norm_dup(k, gain):
        k = k * lax.rsqrt(jnp.mean(k * k, axis=-1, keepdims=True) + EPS) * gain
        return _rope(k, tables).astype(BF16)

    ks0, ks1 = _dup_halves(kv(2))
    ks_ref[0] = norm_dup(ks0, gks_ref[...])
    ks_ref[1] = norm_dup(ks1, gks_ref[...])
    vs_ref[0, 0], vs_ref[1, 0] = _values_t(kv(3))
    kw0, kw1 = _dup_halves(kv(4))
    kw_ref[0] = norm_dup(kw0, gkw_ref[...])
    kw_ref[1] = norm_dup(kw1, gkw_ref[...])
    vw = _values_t(kv(5))
    wt = vw_ref.shape[3]
    for g in range(NSA_KV_HEADS):
        for c in range(vw_ref.shape[1]):
            vw_ref[g, c] = vw[g][:, c * wt:(c + 1) * wt]

    sig_ref[...] = jax.nn.sigmoid(seg('gate'))
    zn_ref[...] = _silu(seg('zn')).astype(BF16)
    vp_ref[...] = seg('vp')
    zp_ref[...] = _silu(seg('zp')).astype(BF16)
    qm = _head_rms(seg('qm'), bd256_ref[...], gqm_ref[...])
    qm_ref[...] = (qm * Q_SCALE).astype(BF16)
    zm_ref[...] = _silu(seg('zm')).astype(BF16)


def _rope_cs_kernel(pos_ref, invf_ref, o_ref):
    ang = invf_ref[...] * pos_ref[...].astype(F32)
    tail = lax.broadcasted_iota(jnp.int32, (ROPE_ROWS - 2 * ROPE_HALF, ang.shape[1]), 0) == 0
    o_ref[...] = jnp.concatenate([jnp.cos(ang), jnp.sin(ang), tail.astype(F32)], axis=0)


def _rope_cs(pos_row, invf_col, blk):
    n = pos_row.shape[1]
    return pl.pallas_call(
        _rope_cs_kernel,
        grid=(n // blk,),
        in_specs=[pl.BlockSpec((1, blk), lambda i: (0, i)), pl.BlockSpec(invf_col.shape, lambda i: (0, 0))],
        out_specs=pl.BlockSpec((ROPE_ROWS, blk), lambda i: (0, i)),
        out_shape=jax.ShapeDtypeStruct((ROPE_ROWS, n), F32),
        compiler_params=_vmem_params(("parallel",)),
    )(pos_row, invf_col)


def _rope_expanders():
    e = np.zeros((3, LANES, ROPE_ROWS), np.float32)
    for lane in range(LANES):
        d = lane % HEAD_DIM
        if d < ROPE_DIM:
            e[0, lane, d % ROPE_HALF] = 1.0
        else:
            e[0, lane, 2 * ROPE_HALF] = 1.0
        if ROPE_HALF <= d < ROPE_DIM:
            e[1, lane, ROPE_HALF + d % ROPE_HALF] = 1.0
        if d < ROPE_HALF:
            e[2, lane, ROPE_HALF + d] = -1.0
    return jnp.asarray(e, dtype=BF16)


def _in_proj(x2, cs, gn, w_all, bd512, bd256, gq, gks, gkw, gqm, rope_e, tm, t_win):
    n = x2.shape[0]
    row = lambda w: pl.BlockSpec((tm, w), lambda i: (i, 0))
    full = lambda a: pl.BlockSpec(a.shape, lambda i: (0,) * a.ndim)
    dup = pl.BlockSpec((NSA_KV_HEADS, tm, LANES), lambda i: (0, i, 0))
    sd = jax.ShapeDtypeStruct
    vst = pl.BlockSpec((NSA_KV_HEADS, 1, LANES, tm), lambda i: (0, i, 0, 0))
    vwt = pl.BlockSpec((NSA_KV_HEADS, tm // t_win, LANES, t_win), lambda i: (0, i, 0, 0))
    kvh = NSA_KV_HEADS
    out_shape = [sd((n, NSA_WIDTH), BF16), sd((n, KV_WIDTH), F32), sd((n, KV_WIDTH), F32),
                 sd((kvh, n, LANES), BF16), sd((kvh, n // tm, LANES, tm), BF16), sd((kvh, n, LANES), BF16),
                 sd((kvh, n // t_win, LANES, t_win), BF16), sd((n, GATE_WIDTH), F32), sd((n, NSA_WIDTH), BF16),
                 sd((n, POOL_WIDTH), F32), sd((n, POOL_WIDTH), BF16), sd((n, MEM_WIDTH), BF16),
                 sd((n, MEM_WIDTH), BF16)]
    out_specs = [row(NSA_WIDTH), row(KV_WIDTH), row(KV_WIDTH), dup, vst, dup, vwt, row(GATE_WIDTH),
                 row(NSA_WIDTH), row(POOL_WIDTH), row(POOL_WIDTH), row(MEM_WIDTH), row(MEM_WIDTH)]
    consts = (gn, w_all, bd512, bd256, gq, gks, gkw, gqm, rope_e)
    return pl.pallas_call(
        _in_proj_kernel,
        grid=(n // tm,),
        in_specs=[row(D_MODEL), pl.BlockSpec((ROPE_ROWS, tm), lambda i: (0, i))] + [full(a) for a in consts],
        out_specs=out_specs,
        out_shape=out_shape,
        compiler_params=_vmem_params(("parallel",)),
    )(x2, cs, *consts)


def _compress_kernel(zk_ref, zv_ref, pend_ref, pak_ref, pbk_ref, w1ak_ref, w1bk_ref, w2k_ref,
                     pav_ref, pbv_ref, w1av_ref, w1bv_ref, w2v_ref, bd128_ref, gk_ref, invf_ref,
                     kc_ref, vc_ref):
    ncp = zk_ref.shape[1] // CMP_STRIDE

    def mlp(z_ref, pa, pb, w1a, w1b, w2, transposed=False):
        a = jnp.zeros((ncp, KV_WIDTH), F32)
        b = jnp.zeros((ncp, KV_WIDTH), F32)
        for o in range(CMP_STRIDE):
            z = z_ref[0, pl.ds(o, ncp, stride=CMP_STRIDE), :]
            a = a + _dot((z + pa[o:o + 1, :]).astype(BF16), w1a[o])
            b = b + _dot((z + pb[o:o + 1, :]).astype(BF16), w1b[o])
        pre = a + pltpu.roll(b, ncp - 1, 0)
        hid = jax.nn.gelu(pre).astype(BF16)
        return _dot_nt(w2, hid) if transposed else _dot(hid, w2)

    kc = mlp(zk_ref, pak_ref, pbk_ref, w1ak_ref, w1bk_ref, w2k_ref[...])
    kc = _head_rms(kc, bd128_ref[...], gk_ref[...])
    kc = _rope(kc, _rope_tables(pend_ref[0].astype(F32), invf_ref[...]))
    k0, k1 = _dup_halves(kc)
    kc_ref[0, 0] = k0.astype(BF16)
    kc_ref[0, 1] = k1.astype(BF16)
    vc_t = mlp(zv_ref, pav_ref, pbv_ref, w1av_ref, w1bv_ref, w2v_ref[...], True)
    vc_ref[0, 0], vc_ref[0, 1] = _values_rows(vc_t)


def _compress(zk, zv, pend, consts):
    b, s, zw = zk.shape
    ncp = s // CMP_STRIDE
    full = lambda a: pl.BlockSpec(a.shape, lambda i: (0,) * a.ndim)
    zspec = pl.BlockSpec((1, s, zw), lambda i: (i, 0, 0))
    ospec = pl.BlockSpec((1, NSA_KV_HEADS, ncp, LANES), lambda i: (i, 0, 0, 0))
    sd = jax.ShapeDtypeStruct((b, NSA_KV_HEADS, ncp, LANES), BF16)
    return pl.pallas_call(
        _compress_kernel,
        grid=(b,),
        in_specs=[zspec, zspec, pl.BlockSpec((1, ncp, 1), lambda i: (i, 0, 0))] + [full(a) for a in consts],
        out_specs=[ospec, pl.BlockSpec((1, NSA_KV_HEADS, LANES, ncp), lambda i: (i, 0, 0, 0))],
        out_shape=[sd, jax.ShapeDtypeStruct((b, NSA_KV_HEADS, LANES, ncp), BF16)],
        compiler_params=_vmem_params(("parallel",)),
    )(zk, zv, pend, *consts)


def _gate_col(sig_blk, r, branch):
    c = branch * NSA_GROUP + r
    return sig_blk[:, c:c + 1]


def _heads_t(q_blk):
    out = []
    for rp in range(NSA_GROUP // 2):
        pair_t = q_blk[:, rp * LANES:(rp + 1) * LANES].astype(F32).T
        low = lax.broadcasted_iota(jnp.int32, pair_t.shape, 0) < HEAD_DIM
        out.append(jnp.where(low, pair_t, 0.0).astype(BF16))
        out.append(jnp.where(low, 0.0, pair_t).astype(BF16))
    return out


def _finish_heads(accs, sig, branch):
    outs = []
    for r, acc in enumerate(accs):
        a = acc.T
        o = a / jnp.maximum(pltpu.roll(a, HEAD_DIM, 1), TINY)
        outs.append(o if sig is None else o * _gate_col(sig, r, branch))
    low = lax.broadcasted_iota(jnp.int32, outs[0].shape, 1) < HEAD_DIM
    return jnp.concatenate(
        [jnp.where(low, outs[0], pltpu.roll(outs[1], HEAD_DIM, 1)),
         jnp.where(low, outs[2], pltpu.roll(outs[3], HEAD_DIM, 1))], axis=1)


def _cmp_kernel(q_ref, kc_ref, vc_ref, sig_ref, selt_ref, o_ref, bias_ref, psum_ref, acc_ref):
    tq = q_ref.shape[1]
    ncp = kc_ref.shape[2]
    t0 = pl.program_id(2) * tq
    q_heads = _heads_t(q_ref[0])

    def attend(nk):
        kc = kc_ref[0, 0, 0:nk, :]
        vc_t = vc_ref[0, 0, :, 0:nk]
        t = t0 + lax.broadcasted_iota(jnp.int32, (nk, tq), 1)
        n = lax.broadcasted_iota(jnp.int32, (nk, tq), 0)
        cmask = n * CMP_STRIDE + (CMP_LEN - 1) <= t
        for r, q_t in enumerate(q_heads):
            s = jnp.where(cmask, _dot(kc, q_t), -jnp.inf)
            m = jnp.max(s, axis=0, keepdims=True)
            m = jnp.where(m == -jnp.inf, 0.0, m)
            e = jnp.exp2(s - m)
            p = e * (1.0 / jnp.maximum(jnp.sum(e, axis=0, keepdims=True), TINY))
            psum_ref[0:nk, :] = p if r == 0 else psum_ref[0:nk, :] + p
            acc_ref[r] = _dot(vc_t, e.astype(BF16))
        if nk < ncp:
            psum_ref[nk:ncp, :] = jnp.zeros((ncp - nk, tq), F32)

    last_block = (t0 + tq - CMP_LEN) // CMP_STRIDE
    n_chunks = jnp.minimum(last_block // CMP_CHUNK + 1, ncp // CMP_CHUNK)
    for v in range(1, ncp // CMP_CHUNK + 1):
        pl.when(n_chunks == v)(functools.partial(attend, v * CMP_CHUNK))
    o_ref[0] = _finish_heads([acc_ref[r] for r in range(NSA_GROUP)], sig_ref[0], 0).astype(BF16)

    psum = psum_ref[...]
    ps_hi = psum.astype(BF16)
    ps_lo = (psum - ps_hi.astype(F32)).astype(BF16)
    selt = selt_ref[...]
    imp = _dot(selt, ps_hi) + _dot(selt, ps_lo)
    j = lax.broadcasted_iota(jnp.int32, (NS_PAD, tq), 0)
    cur = (t0 + lax.broadcasted_iota(jnp.int32, (NS_PAD, tq), 1)) >> SEL_BLOCK_LOG2
    forced = (j == 0) | (j == cur) | (j == cur - 1)
    valid = j <= cur
    vals = jnp.where(forced, -jnp.inf, jnp.where(valid, imp, -1.0))
    neg_j = -j.astype(F32)
    for _ in range(SEL_TOPK - N_FORCED):
        m = jnp.max(vals, axis=0, keepdims=True)
        first = jnp.max(jnp.where(vals == m, neg_j, -jnp.inf), axis=0, keepdims=True)
        vals = jnp.where(neg_j == first, -jnp.inf, vals)
    bias_t = jnp.where((vals == -jnp.inf) & valid, 0.0, MASK_BIAS)
    bias_ref[0, 0] = bias_t.astype(BF16)


def _cmp_attention(q3, kc, vc, sig3, selt, tq):
    b, s, _ = q3.shape
    ncp = kc.shape[2]
    assert ncp % CMP_CHUNK == 0 and tq % SEL_BLOCK == 0
    grid = (b, NSA_KV_HEADS, s // tq)
    return pl.pallas_call(
        _cmp_kernel,
        grid=grid,
        in_specs=[pl.BlockSpec((1, tq, GROUP_WIDTH), lambda bi, g, i: (bi, i, g)),
                  pl.BlockSpec((1, 1, ncp, LANES), lambda bi, g, i: (bi, g, 0, 0)),
                  pl.BlockSpec((1, 1, LANES, ncp), lambda bi, g, i: (bi, g, 0, 0)),
                  pl.BlockSpec((1, tq, LANES), lambda bi, g, i: (bi, i, g)),
                  pl.BlockSpec(selt.shape, lambda bi, g, i: (0, 0))],
        out_specs=[pl.BlockSpec((1, tq, GROUP_WIDTH), lambda bi, g, i: (bi, i, g)),
                   pl.BlockSpec((1, 1, NS_PAD, tq), lambda bi, g, i: (bi, g, 0, i))],
        out_shape=[jax.ShapeDtypeStruct((b, s, NSA_WIDTH), BF16),
                   jax.ShapeDtypeStruct((b, NSA_KV_HEADS, NS_PAD, s), BF16)],
        scratch_shapes=[pltpu.VMEM((ncp, tq), F32), pltpu.VMEM((NSA_GROUP, LANES, tq), F32)],
        compiler_params=_vmem_params(("parallel", "parallel", "parallel")),
    )(q3, kc, vc, sig3, selt)


def _sel_kernel(q_ref, bias_ref, et_ref, ks_ref, vs_ref, sig_ref, oc_ref, ow_ref, zn_ref, o_ref,
                qaug_ref, s_ref, m_ref, acc_ref):
    tq = q_ref.shape[1]
    tk = vs_ref.shape[4]
    cols = NSA_GROUP * tq
    qt = pl.program_id(2)

    bias = bias_ref[0, 0]
    for r, q_t in enumerate(_heads_t(q_ref[0])):
        qaug_ref[0:LANES, r * tq:(r + 1) * tq] = bias
        qaug_ref[LANES:2 * LANES, r * tq:(r + 1) * tq] = q_t
    m_ref[...] = jnp.full(m_ref.shape, -jnp.inf, F32)
    acc_ref[...] = jnp.zeros(acc_ref.shape, F32)
    heads = [slice(r * tq, (r + 1) * tq) for r in range(NSA_GROUP)]

    def keys(kt):
        k0 = pl.multiple_of(kt * tk, tk)
        return jnp.concatenate([et_ref[pl.ds(k0, tk), :], ks_ref[0, 0, pl.ds(k0, tk), :]], axis=1)

    def softmax_pv(kt, h, s):
        rows = s.shape[0]
        m_prev = m_ref[:, h]
        m_new = jnp.maximum(m_prev, jnp.max(s, axis=0, keepdims=True))
        p = jnp.exp2(s - m_new).astype(BF16)
        acc_ref[:, h] = jnp.exp2(m_prev - m_new) * acc_ref[:, h] + _dot(vs_ref[0, 0, kt, :, 0:rows], p)
        m_ref[:, h] = m_new

    k_first = keys(0)
    for h in heads:
        s_ref[:, h] = _dot(k_first, qaug_ref[:, h])

    def step(kt):
        k_next = keys(kt + 1)
        for h in heads:
            s = s_ref[:, h]
            s_next = _dot(k_next, qaug_ref[:, h])
            softmax_pv(kt, h, s)
            s_ref[:, h] = s_next

    def unrolled_steps(j, first):
        for u in range(SEL_UNROLL):
            step(first + SEL_UNROLL * j + u)
        return first

    def two_steps(first):
        step(first)
        step(first + 1)

    last = (qt * tq) // tk
    rest = last % SEL_UNROLL
    pl.when(rest % 2 == 1)(lambda: step(0))
    pl.when(rest >= 2)(functools.partial(two_steps, rest % 2))
    lax.fori_loop(0, last // SEL_UNROLL, unrolled_steps, rest)

    def last_tile(rows):
        kpos = last * tk + lax.broadcasted_iota(jnp.int32, (rows, tq), 0)
        causal = kpos <= qt * tq + lax.broadcasted_iota(jnp.int32, (rows, tq), 1)
        for h in heads:
            softmax_pv(last, h, jnp.where(causal, s_ref[0:rows, h], MASK_BIAS))

    visible = (qt + 1) * tq - last * tk
    for rows in range(tq, tk + 1, tq):
        pl.when(visible == rows)(functools.partial(last_tile, rows))

    acc = acc_ref[...]
    o_sel = _finish_heads([acc[:, h] for h in heads], sig_ref[0], 1)
    o_nsa = oc_ref[0].astype(F32) + o_sel + ow_ref[0].astype(F32)
    o_ref[0] = (o_nsa * zn_ref[0].astype(F32)).astype(BF16)


def _sel_attention(q3, bias, et, ks, vst, sig3, o_c, o_w, zn3, tq):
    b, s, _ = q3.shape
    tk = vst.shape[4]
    cols = NSA_GROUP * tq
    assert SEL_UNROLL == 4 and tk % tq == 0
    group_blk = pl.BlockSpec((1, tq, GROUP_WIDTH), lambda bi, g, i: (bi, i, g))
    return pl.pallas_call(
        _sel_kernel,
        grid=(b, NSA_KV_HEADS, s // tq),
        in_specs=[group_blk,
                  pl.BlockSpec((1, 1, NS_PAD, tq), lambda bi, g, i: (bi, g, 0, i)),
                  pl.BlockSpec((s, LANES), lambda bi, g, i: (0, 0)),
                  pl.BlockSpec((1, 1, s, LANES), lambda bi, g, i: (g, bi, 0, 0)),
                  pl.BlockSpec((1, 1, s // tk, LANES, tk), lambda bi, g, i: (g, bi, 0, 0, 0)),
                  pl.BlockSpec((1, tq, LANES), lambda bi, g, i: (bi, i, g)),
                  group_blk, group_blk, group_blk],
        out_specs=group_blk,
        out_shape=jax.ShapeDtypeStruct((b, s, NSA_WIDTH), BF16),
        scratch_shapes=[pltpu.VMEM((2 * LANES, cols), BF16), pltpu.VMEM((tk, cols), F32),
                        pltpu.VMEM((1, cols), F32), pltpu.VMEM((LANES, cols), F32)],
        compiler_params=_vmem_params(("parallel", "parallel", "parallel")),
    )(q3, bias, et, ks, vst, sig3, o_c, o_w, zn3)


def _win_kernel(q_ref, kp_ref, kc_ref, vp_ref, vc_ref, sig_ref, o_ref):
    tq = q_ref.shape[1]
    qt = pl.program_id(1)
    c = lax.broadcasted_iota(jnp.int32, (tq, tq), 0)
    i = lax.broadcasted_iota(jnp.int32, (tq, tq), 1)
    mask_cur = c <= i
    mask_prev = c > i + jnp.where(qt > 0, 0, tq)
    for g in range(NSA_KV_HEADS):
        lanes = slice(g * GROUP_WIDTH, (g + 1) * GROUP_WIDTH)
        accs = []
        for q_t in _heads_t(q_ref[0, :, lanes]):
            s_p = jnp.where(mask_prev, _dot(kp_ref[g, 0], q_t), -jnp.inf)
            s_c = jnp.where(mask_cur, _dot(kc_ref[g, 0], q_t), -jnp.inf)
            m = jnp.maximum(jnp.max(s_p, axis=0, keepdims=True), jnp.max(s_c, axis=0, keepdims=True))
            e_p = jnp.exp2(s_p - m).astype(BF16)
            e_c = jnp.exp2(s_c - m).astype(BF16)
            accs.append(_dot(vp_ref[g, 0], e_p) + _dot(vc_ref[g, 0], e_c))
        o_ref[0, :, lanes] = _finish_heads(accs, sig_ref[0, :, g * LANES:(g + 1) * LANES], 2).astype(BF16)


def _win_attention(q3, kw, vw, sig3, tq):
    b, s, _ = q3.shape
    assert tq == WINDOW
    nt = s // tq
    kvh = NSA_KV_HEADS
    kspec = lambda back: pl.BlockSpec((kvh, 1, tq, LANES), lambda bi, i: (0, bi, jnp.maximum(i - back, 0), 0))
    vspec = lambda back: pl.BlockSpec((kvh, 1, LANES, tq),
                                      lambda bi, i: (0, bi * nt + jnp.maximum(i - back, 0), 0, 0))
    return pl.pallas_call(
        _win_kernel,
        grid=(b, nt),
        in_specs=[pl.BlockSpec((1, tq, NSA_WIDTH), lambda bi, i: (bi, i, 0)), kspec(1), kspec(0),
                  vspec(1), vspec(0), pl.BlockSpec((1, tq, GATE_WIDTH), lambda bi, i: (bi, i, 0))],
        out_specs=pl.BlockSpec((1, tq, NSA_WIDTH), lambda bi, i: (bi, i, 0)),
        out_shape=jax.ShapeDtypeStruct((b, s, NSA_WIDTH), BF16),
        compiler_params=_vmem_params(("parallel", "parallel")),
    )(q3, kw, kw, vw, vw, sig3)


def _mem_kv_kernel(mem_ref, gm_ref, w_ref, bd256_ref, gk_ref, mk_ref, mv_ref):
    x = mem_ref[0]
    h = x * lax.rsqrt(jnp.mean(x * x, axis=-1, keepdims=True) + EPS) * gm_ref[...]
    mkv = _dot(h.astype(BF16), w_ref[...])
    mk_ref[0] = _head_rms(mkv[:, 0:MEM_WIDTH], bd256_ref[...], gk_ref[...]).astype(BF16)
    for hp in range(MEM_HEADS // 2):
        lo = MEM_WIDTH + hp * LANES
        mv_ref[0, 2 * hp], mv_ref[0, 2 * hp + 1] = _values_t(mkv[:, lo:lo + LANES])


def _mem_kv(mem, gm, w, bd256, gk):
    b, m, _ = mem.shape
    full = lambda a: pl.BlockSpec(a.shape, lambda i: (0,) * a.ndim)
    return pl.pallas_call(
        _mem_kv_kernel,
        grid=(b,),
        in_specs=[pl.BlockSpec((1, m, D_MODEL), lambda i: (i, 0, 0)), full(gm), full(w), full(bd256), full(gk)],
        out_specs=[pl.BlockSpec((1, m, MEM_WIDTH), lambda i: (i, 0, 0)),
                   pl.BlockSpec((1, MEM_HEADS, LANES, m), lambda i: (i, 0, 0, 0))],
        out_shape=[jax.ShapeDtypeStruct((b, m, MEM_WIDTH), BF16),
                   jax.ShapeDtypeStruct((b, MEM_HEADS, LANES, m), BF16)],
        compiler_params=_vmem_params(("parallel",)),
    )(mem, gm, w, bd256, gk)


def _mem_attention_block(q_blk, mk, mv_t):
    accs = []
    for h, q_t in enumerate(_heads_t(q_blk)):
        hp = h // 2
        s = _dot(mk[:, hp * LANES:(hp + 1) * LANES], q_t)
        e = jnp.exp2(s - jnp.max(s, axis=0, keepdims=True))
        accs.append(_dot(mv_t[h], e.astype(BF16)))
    return _finish_heads(accs, None, 0)


def _out_kernel(seq_len, x_ref, ynsa_ref, vp_ref, halo_ref, zp_ref, qm_ref, mk_ref, mv_ref, zm_ref, wo_ref,
                wpool_ref, pscale_ref, o_ref):
    tm = x_ref.shape[0]
    t0 = (pl.program_id(0) * tm) % seq_len

    v = vp_ref[...]
    halo = halo_ref[...] * (t0 > 0).astype(F32)
    e = jnp.concatenate([halo, v], axis=0)
    w2 = e + pltpu.roll(e, 1, 0)
    w4 = w2 + pltpu.roll(w2, 2, 0)
    w8 = w4 + pltpu.roll(w4, 4, 0)
    w16 = w8 + pltpu.roll(w8, 8, 0)
    shape = (tm, POOL_WIDTH)
    grp = lax.broadcasted_iota(jnp.int32, shape, 1) >> HEAD_DIM_LOG2
    wsum = jnp.where(grp == 0, w2[POOL_HALO:],
                     jnp.where(grp == 1, w4[POOL_HALO:], jnp.where(grp == 2, w8[POOL_HALO:], w16[POOL_HALO:])))
    width = jnp.left_shift(2, grp)
    cnt = jnp.minimum(t0 + lax.broadcasted_iota(jnp.int32, shape, 0) + 1, width).astype(F32)
    pooled = wsum / cnt - v
    y_pool = _dot(pooled.astype(BF16), wpool_ref[...]) * pscale_ref[...] * zp_ref[...].astype(F32)

    y_mem = _mem_attention_block(qm_ref[...], mk_ref[0], mv_ref[0]) * zm_ref[...].astype(F32)
    pool_lo, mem_lo = NSA_WIDTH, NSA_WIDTH + POOL_WIDTH
    out = x_ref[...] + _dot(ynsa_ref[...], wo_ref[0:pool_lo, :])
    out = out + _dot(y_pool.astype(BF16), wo_ref[pool_lo:mem_lo, :])
    out = out + _dot(y_mem.astype(BF16), wo_ref[mem_lo:mem_lo + MEM_WIDTH, :])
    o_ref[...] = out


def _out_proj(x2, ynsa, vp, zp, qm, mk, mv_t, zm, wo, wpool, pscale, seq_len, tm):
    n = x2.shape[0]
    row = lambda w: pl.BlockSpec((tm, w), lambda i: (i, 0))
    full = lambda a: pl.BlockSpec(a.shape, lambda i: (0,) * a.ndim)
    per_batch = lambda a: pl.BlockSpec((1,) + a.shape[1:],
                                       lambda i: ((i * tm) // seq_len,) + (0,) * (a.ndim - 1))
    hb = tm // POOL_HALO
    halo = pl.BlockSpec((POOL_HALO, POOL_WIDTH), lambda i: (jnp.maximum(i * hb - 1, 0), 0))
    return pl.pallas_call(
        functools.partial(_out_kernel, seq_len),
        grid=(n // tm,),
        in_specs=[row(D_MODEL), row(NSA_WIDTH), row(POOL_WIDTH), halo, row(POOL_WIDTH), row(MEM_WIDTH),
                  per_batch(mk), per_batch(mv_t), row(MEM_WIDTH), full(wo), full(wpool), full(pscale)],
        out_specs=row(D_MODEL),
        out_shape=jax.ShapeDtypeStruct((n, D_MODEL), F32),
        compiler_params=_vmem_params(("parallel",)),
    )(x2, ynsa, vp, vp, zp, qm, mk, mv_t, zm, wo, wpool, pscale)


def _block_diag_ones(width):
    i = np.arange(width) // HEAD_DIM
    return jnp.asarray((i[:, None] == i[None, :]).astype(np.float32), dtype=BF16)


def _pack_w_in(w_in):
    q, kv, gate, zn, vp, zp, qm, zm = jnp.split(
        w_in, np.cumsum([NSA_WIDTH, 2 * N_BRANCH * KV_WIDTH, NSA_HEADS * N_BRANCH, NSA_WIDTH, POOL_WIDTH,
                         POOL_WIDTH, MEM_WIDTH, MEM_WIDTH])[:-1].tolist(), axis=1)
    cols = np.zeros((NSA_KV_HEADS, LANES), np.int64)
    used = np.zeros((NSA_KV_HEADS, LANES), bool)
    for g in range(NSA_KV_HEADS):
        for k in range(N_BRANCH):
            for r in range(NSA_GROUP):
                cols[g, k * NSA_GROUP + r] = (g * NSA_GROUP + r) * N_BRANCH + k
                used[g, k * NSA_GROUP + r] = True
    gate_p = jnp.where(jnp.asarray(used.reshape(-1))[None, :], gate[:, cols.reshape(-1)], 0.0)
    return jnp.concatenate([q, kv, gate_p, zn, vp, zp, qm, zm], axis=1).astype(BF16)


def _pack_cmp(pos, w1, w2):
    half = CMP_LEN // 2

    def w1_half(wh):
        z = jnp.zeros((half, NSA_KV_HEADS, HEAD_DIM, NSA_KV_HEADS, HEAD_DIM), F32)
        for g in range(NSA_KV_HEADS):
            z = z.at[:, g, :, g, :].set(wh)
        return z.reshape(half, KV_WIDTH, KV_WIDTH).astype(BF16)

    def pos_half(ph):
        return jnp.tile(ph[:, None, :], (1, NSA_KV_HEADS, 1)).reshape(half, KV_WIDTH)

    w2_bd = jnp.zeros((NSA_KV_HEADS, HEAD_DIM, NSA_KV_HEADS, HEAD_DIM), F32)
    for g in range(NSA_KV_HEADS):
        w2_bd = w2_bd.at[g, :, g, :].set(w2)
    return (pos_half(pos[:half]), pos_half(pos[half:]), w1_half(w1[:half]), w1_half(w1[half:]),
            w2_bd.reshape(KV_WIDTH, KV_WIDTH).astype(BF16))


def _sel_map_t(ncp, s):
    nc = (s - CMP_LEN) // CMP_STRIDE + 1
    ns = s // SEL_BLOCK
    c0 = np.arange(nc) * CMP_STRIDE
    c1 = c0 + CMP_LEN
    s0 = np.arange(ns) * SEL_BLOCK
    s1 = s0 + SEL_BLOCK
    ov = np.clip(np.minimum(c1[:, None], s1[None, :]) - np.maximum(c0[:, None], s0[None, :]), 0, None)
    m = np.zeros((NS_PAD, ncp), np.float32)
    m[:ns, :nc] = (ov / CMP_LEN).T
    return jnp.asarray(m, dtype=BF16)


def kernel(x, mem, positions, g_norm, w_in, g_q_nsa, g_k_cmp, g_k_slc, g_k_win, cmp_pos_k, w_cmp_k1,
           w_cmp_k2, cmp_pos_v, w_cmp_v1, w_cmp_v2, w_pool, pool_scale, g_mem, w_mem_kv, g_q_mem,
           g_k_mem, w_out):
    b, s, d = x.shape
    depth = g_norm.shape[0]
    assert d == D_MODEL and s % WINDOW == 0 and s // SEL_BLOCK >= SEL_TOPK and s // SEL_BLOCK <= NS_PAD
    n = b * s
    ncp = s // CMP_STRIDE
    tm = 512
    t_cmp = 1024
    t_sel = 512
    t_win = WINDOW
    assert s % tm == 0 and tm % t_sel == 0 and tm == WINDOW

    inv_freq = ROPE_THETA ** (-jnp.arange(ROPE_HALF, dtype=F32) / ROPE_HALF)
    invf = inv_freq[(np.arange(LANES) % HEAD_DIM) % ROPE_HALF][None, :]
    bd512, bd256, bd128 = (_block_diag_ones(w) for w in (NSA_WIDTH, MEM_WIDTH, KV_WIDTH))
    cs = _rope_cs(positions.reshape(1, n), inv_freq[:, None], min(n, 8192))
    rope_e = _rope_expanders()
    end_idx = np.minimum(np.arange(ncp) * CMP_STRIDE + CMP_LEN - 1, s - 1)
    pend = positions[:, end_idx][:, :, None]
    selt = _sel_map_t(ncp, s)
    et = jnp.asarray((np.arange(s)[:, None] // SEL_BLOCK == np.arange(NS_PAD)[None, :]).astype(np.float32),
                     dtype=BF16)
    tile = lambda g, reps: jnp.tile(g[None, :], (1, reps))

    for l in range(depth):
        x2 = x.reshape(n, d)
        w_all = _pack_w_in(w_in[l])
        (q, kc_raw, vc_raw, ks, vs, kw, vw, sig, zn, vp, zp, qm, zm) = _in_proj(
            x2, cs, g_norm[l][None, :], w_all, bd512, bd256, tile(g_q_nsa[l], 8), tile(g_k_slc[l], 2),
            tile(g_k_win[l], 2), tile(g_q_mem[l], 4), rope_e, tm, t_win)

        cmp_consts = (_pack_cmp(cmp_pos_k[l], w_cmp_k1[l], w_cmp_k2[l])
                      + _pack_cmp(cmp_pos_v[l], w_cmp_v1[l], w_cmp_v2[l].T)
                      + (bd128, tile(g_k_cmp[l], 2), invf))
        kc, vc = _compress(kc_raw.reshape(b, s, KV_WIDTH), vc_raw.reshape(b, s, KV_WIDTH), pend, cmp_consts)

        q3 = q.reshape(b, s, NSA_WIDTH)
        sig3 = sig.reshape(b, s, GATE_WIDTH)
        o_c, bias = _cmp_attention(q3, kc, vc, sig3, selt, t_cmp)
        dup4 = lambda a: a.reshape(NSA_KV_HEADS, b, s, LANES)
        vt5 = lambda a: a.reshape(NSA_KV_HEADS, b, s // tm, LANES, tm)
        o_w = _win_attention(q3, dup4(kw), vw, sig3, t_win)
        y_nsa = _sel_attention(q3, bias, et, dup4(ks), vt5(vs), sig3, o_c, o_w,
                               zn.reshape(b, s, NSA_WIDTH), t_sel)

        mk, mv_t = _mem_kv(mem, g_mem[l][None, :], w_mem_kv[l].astype(BF16), bd256, tile(g_k_mem[l], 4))

        wpool = jnp.zeros((4, HEAD_DIM, 4, HEAD_DIM), F32)
        for g in range(4):
            wpool = wpool.at[g, :, g, :].set(w_pool[l, g])
        out = _out_proj(x2, y_nsa.reshape(n, NSA_WIDTH), vp, zp, qm, mk, mv_t, zm, w_out[l].astype(BF16),
                        wpool.reshape(POOL_WIDTH, POOL_WIDTH).astype(BF16), pool_scale[l][None, :], s, tm)
        x = out.reshape(b, s, d)
    return x
```

```python
import functools

import numpy as np
import jax
import jax.numpy as jnp
from jax import lax
from jax.experimental import pallas as pl
from jax.experimental.pallas import tpu as pltpu

F32 = jnp.float32
BF16 = jnp.bfloat16

D_MODEL = 1024
HEAD_DIM = 64
NSA_HEADS = 8
NSA_KV_HEADS = 2
NSA_GROUP = NSA_HEADS // NSA_KV_HEADS
NSA_WIDTH = NSA_HEADS * HEAD_DIM
KV_WIDTH = NSA_KV_HEADS * HEAD_DIM
N_BRANCH = 3
CMP_LEN = 32
CMP_STRIDE = 16
SEL_BLOCK = 64
SEL_TOPK = 16
WINDOW = 512
FORCE_SCORE = 1e4
N_FORCED = 3
POOL_WINDOWS = (2, 4, 8, 16)
POOL_WIDTH = 256
MEM_HEADS = 4
MEM_WIDTH = MEM_HEADS * HEAD_DIM
ROPE_THETA = 500000.0
ROPE_DIM = HEAD_DIM // 4
ROPE_HALF = ROPE_DIM // 2
EPS = 1e-6
SCALE = HEAD_DIM ** -0.5
Q_SCALE = SCALE * float(np.log2(np.e))

LANES = 128
NS_PAD = LANES
MASK_BIAS = -1e30
POOL_HALO = 16
CMP_CHUNK = LANES
SEL_UNROLL = 4
ROPE_ROWS = 32
TINY = float(np.finfo(np.float32).tiny)

GROUP_WIDTH = NSA_GROUP * HEAD_DIM
GATE_WIDTH = NSA_KV_HEADS * LANES
SEL_BLOCK_LOG2 = SEL_BLOCK.bit_length() - 1
HEAD_DIM_LOG2 = HEAD_DIM.bit_length() - 1
assert 1 << SEL_BLOCK_LOG2 == SEL_BLOCK and 1 << HEAD_DIM_LOG2 == HEAD_DIM
V7X_VMEM_BYTES = 64 * 1024 * 1024
VMEM_LIMIT_BYTES = V7X_VMEM_BYTES * 7 // 8

_SEG_WIDTHS = (('q', NSA_WIDTH), ('kv', 2 * N_BRANCH * KV_WIDTH), ('gate', GATE_WIDTH), ('zn', NSA_WIDTH),
               ('vp', POOL_WIDTH), ('zp', POOL_WIDTH), ('qm', MEM_WIDTH), ('zm', MEM_WIDTH))
_SEG = {name: (sum(w for _, w in _SEG_WIDTHS[:k]), width) for k, (name, width) in enumerate(_SEG_WIDTHS)}

_NT = (((1,), (1,)), ((), ()))


def _dot(a, b):
    return jnp.dot(a, b, preferred_element_type=F32)


def _dot_nt(a, b):
    return lax.dot_general(a, b, _NT, preferred_element_type=F32)


def _vmem_params(sem):
    return pltpu.CompilerParams(dimension_semantics=sem, vmem_limit_bytes=VMEM_LIMIT_BYTES)


def _rope_tables(pos_f32, invf):
    ang = pos_f32 * invf
    c, s = jnp.cos(ang), jnp.sin(ang)
    d = lax.broadcasted_iota(jnp.int32, ang.shape, 1) % HEAD_DIM
    tc = jnp.where(d < ROPE_DIM, c, 1.0)
    ts_up = jnp.where((d >= ROPE_HALF) & (d < ROPE_DIM), s, 0.0)
    ts_dn = jnp.where(d < ROPE_HALF, -s, 0.0)
    return tc, ts_up, ts_dn


def _rope(x, tables):
    tc, ts_up, ts_dn = tables
    w = x.shape[1]
    reps = w // LANES
    if reps > 1:
        tc, ts_up, ts_dn = (jnp.concatenate([t] * reps, axis=1) for t in (tc, ts_up, ts_dn))
    return (x * tc + pltpu.roll(x, ROPE_HALF, 1) * ts_up
            + pltpu.roll(x, w - ROPE_HALF, 1) * ts_dn)


def _head_rms(x, ones_bd, gain):
    ss = _dot((x * x).astype(BF16), ones_bd)
    return x * lax.rsqrt(ss * (1.0 / HEAD_DIM) + EPS) * gain


def _dup_halves(x):
    low = lax.broadcasted_iota(jnp.int32, x.shape, 1) < HEAD_DIM
    r = pltpu.roll(x, HEAD_DIM, 1)
    return jnp.where(low, x, r), jnp.where(low, r, x)


def _values_t(v):
    return _values_rows(v.T)


def _values_rows(vt):
    vt = vt.astype(BF16)
    ones = jnp.ones((HEAD_DIM, vt.shape[1]), BF16)
    return (jnp.concatenate([vt[0:HEAD_DIM], ones], axis=0),
            jnp.concatenate([vt[HEAD_DIM:], ones], axis=0))


def _silu(z):
    return z * jax.nn.sigmoid(z)


def _in_proj_kernel(x_ref, cs_ref, gn_ref, w_ref, bd512_ref, bd256_ref, gq_ref, gks_ref, gkw_ref,
                    gqm_ref, rope_e_ref,
                    q_ref, kc_ref, vc_ref, ks_ref, vs_ref, kw_ref, vw_ref, sig_ref, zn_ref, vp_ref,
                    zp_ref, qm_ref, zm_ref):
    x = x_ref[...]
    h = x * lax.rsqrt(jnp.mean(x * x, axis=-1, keepdims=True) + EPS) * gn_ref[...]
    hb = h.astype(BF16)

    def seg(name):
        o, w = _SEG[name]
        return _dot(hb, w_ref[:, o:o + w])

    cs = cs_ref[...]
    cs_hi = cs.astype(BF16)
    cs_lo = (cs - cs_hi.astype(F32)).astype(BF16)
    tables = tuple((_dot(rope_e_ref[k], cs_hi) + _dot(rope_e_ref[k], cs_lo)).T for k in range(3))

    q = _head_rms(seg('q'), bd512_ref[...], gq_ref[...])
    q_ref[...] = (_rope(q, tables) * Q_SCALE).astype(BF16)

    kv_all = seg('kv')
    kv = lambda j: kv_all[:, j * KV_WIDTH:(j + 1) * KV_WIDTH]
    kc_ref[...] = kv(0)
    vc_ref[...] = kv(1)

    def norm_dup(k, gain):
        k = k * lax.rsqrt(jnp.mean(k * k, axis=-1, keepdims=True) + EPS) * gain
        return _rope(k, tables).astype(BF16)

    ks0, ks1 = _dup_halves(kv(2))
    ks_ref[0] = norm_dup(ks0, gks_ref[...])
    ks_ref[1] = norm_dup(ks1, gks_ref[...])
    vs_ref[0, 0], vs_ref[1, 0] = _values_t(kv(3))
    kw0, kw1 = _dup_halves(kv(4))
    kw_ref[0] = norm_dup(kw0, gkw_ref[...])
    kw_ref[1] = norm_dup(kw1, gkw_ref[...])
    vw = _values_t(kv(5))
    wt = vw_ref.shape[3]
    for g in range(NSA_KV_HEADS):
        for c in range(vw_ref.shape[1]):
            vw_ref[g, c] = vw[g][:, c * wt:(c + 1) * wt]

    sig_ref[...] = jax.nn.sigmoid(seg('gate'))
    zn_ref[...] = _silu(seg('zn')).astype(BF16)
    vp_ref[...] = seg('vp')
    zp_ref[...] = _silu(seg('zp')).astype(BF16)
    qm = _head_rms(seg('qm'), bd256_ref[...], gqm_ref[...])
    qm_ref[...] = (qm * Q_SCALE).astype(BF16)
    zm_ref[...] = _silu(seg('zm')).astype(BF16)


def _rope_cs_kernel(pos_ref, invf_ref, o_ref):
    ang = invf_ref[...] * pos_ref[...].astype(F32)
    tail = lax.broadcasted_iota(jnp.int32, (ROPE_ROWS - 2 * ROPE_HALF, ang.shape[1]), 0) == 0
    o_ref[...] = jnp.concatenate([jnp.cos(ang), jnp.sin(ang), tail.astype(F32)], axis=0)


def _rope_cs(pos_row, invf_col, blk):
    n = pos_row.shape[1]
    return pl.pallas_call(
        _rope_cs_kernel,
        grid=(n // blk,),
        in_specs=[pl.BlockSpec((1, blk), lambda i: (0, i)), pl.BlockSpec(invf_col.shape, lambda i: (0, 0))],
        out_specs=pl.BlockSpec((ROPE_ROWS, blk), lambda i: (0, i)),
        out_shape=jax.ShapeDtypeStruct((ROPE_ROWS, n), F32),
        compiler_params=_vmem_params(("parallel",)),
    )(pos_row, invf_col)


def _rope_expanders():
    e = np.zeros((3, LANES, ROPE_ROWS), np.float32)
    for lane in range(LANES):
        d = lane % HEAD_DIM
        if d < ROPE_DIM:
            e[0, lane, d % ROPE_HALF] = 1.0
        else:
            e[0, lane, 2 * ROPE_HALF] = 1.0
        if ROPE_HALF <= d < ROPE_DIM:
            e[1, lane, ROPE_HALF + d % ROPE_HALF] = 1.0
        if d < ROPE_HALF:
            e[2, lane, ROPE_HALF + d] = -1.0
    return jnp.asarray(e, dtype=BF16)


def _in_proj(x2, cs, gn, w_all, bd512, bd256, gq, gks, gkw, gqm, rope_e, tm, t_win):
    n = x2.shape[0]
    row = lambda w: pl.BlockSpec((tm, w), lambda i: (i, 0))
    full = lambda a: pl.BlockSpec(a.shape, lambda i: (0,) * a.ndim)
    dup = pl.BlockSpec((NSA_KV_HEADS, tm, LANES), lambda i: (0, i, 0))
    sd = jax.ShapeDtypeStruct
    vst = pl.BlockSpec((NSA_KV_HEADS, 1, LANES, tm), lambda i: (0, i, 0, 0))
    vwt = pl.BlockSpec((NSA_KV_HEADS, tm // t_win, LANES, t_win), lambda i: (0, i, 0, 0))
    kvh = NSA_KV_HEADS
    out_shape = [sd((n, NSA_WIDTH), BF16), sd((n, KV_WIDTH), F32), sd((n, KV_WIDTH), F32),
                 sd((kvh, n, LANES), BF16), sd((kvh, n // tm, LANES, tm), BF16), sd((kvh, n, LANES), BF16),
                 sd((kvh, n // t_win, LANES, t_win), BF16), sd((n, GATE_WIDTH), F32), sd((n, NSA_WIDTH), BF16),
                 sd((n, POOL_WIDTH), F32), sd((n, POOL_WIDTH), BF16), sd((n, MEM_WIDTH), BF16),
                 sd((n, MEM_WIDTH), BF16)]
    out_specs = [row(NSA_WIDTH), row(KV_WIDTH), row(KV_WIDTH), dup, vst, dup, vwt, row(GATE_WIDTH),
                 row(NSA_WIDTH), row(POOL_WIDTH), row(POOL_WIDTH), row(MEM_WIDTH), row(MEM_WIDTH)]
    consts = (gn, w_all, bd512, bd256, gq, gks, gkw, gqm, rope_e)
    return pl.pallas_call(
        _in_proj_kernel,
        grid=(n // tm,),
        in_specs=[row(D_MODEL), pl.BlockSpec((ROPE_ROWS, tm), lambda i: (0, i))] + [full(a) for a in consts],
        out_specs=out_specs,
        out_shape=out_shape,
        compiler_params=_vmem_params(("parallel",)),
    )(x2, cs, *consts)


def _compress_kernel(zk_ref, zv_ref, pend_ref, pak_ref, pbk_ref, w1ak_ref, w1bk_ref, w2k_ref,
                     pav_ref, pbv_ref, w1av_ref, w1bv_ref, w2v_ref, bd128_ref, gk_ref, invf_ref,
                     kc_ref, vc_ref):
    ncp = zk_ref.shape[1] // CMP_STRIDE

    def mlp(z_ref, pa, pb, w1a, w1b, w2, transposed=False):
        a = jnp.zeros((ncp, KV_WIDTH), F32)
        b = jnp.zeros((ncp, KV_WIDTH), F32)
        for o in range(CMP_STRIDE):
            z = z_ref[0, pl.ds(o, ncp, stride=CMP_STRIDE), :]
            a = a + _dot((z + pa[o:o + 1, :]).astype(BF16), w1a[o])
            b = b + _dot((z + pb[o:o + 1, :]).astype(BF16), w1b[o])
        pre = a + pltpu.roll(b, ncp - 1, 0)
        hid = jax.nn.gelu(pre).astype(BF16)
        return _dot_nt(w2, hid) if transposed else _dot(hid, w2)

    kc = mlp(zk_ref, pak_ref, pbk_ref, w1ak_ref, w1bk_ref, w2k_ref[...])
    kc = _head_rms(kc, bd128_ref[...], gk_ref[...])
    kc = _rope(kc, _rope_tables(pend_ref[0].astype(F32), invf_ref[...]))
    k0, k1 = _dup_halves(kc)
    kc_ref[0, 0] = k0.astype(BF16)
    kc_ref[0, 1] = k1.astype(BF16)
    vc_t = mlp(zv_ref, pav_ref, pbv_ref, w1av_ref, w1bv_ref, w2v_ref[...], True)
    vc_ref[0, 0], vc_ref[0, 1] = _values_rows(vc_t)


def _compress(zk, zv, pend, consts):
    b, s, zw = zk.shape
    ncp = s // CMP_STRIDE
    full = lambda a: pl.BlockSpec(a.shape, lambda i: (0,) * a.ndim)
    zspec = pl.BlockSpec((1, s, zw), lambda i: (i, 0, 0))
    ospec = pl.BlockSpec((1, NSA_KV_HEADS, ncp, LANES), lambda i: (i, 0, 0, 0))
    sd = jax.ShapeDtypeStruct((b, NSA_KV_HEADS, ncp, LANES), BF16)
    return pl.pallas_call(
        _compress_kernel,
        grid=(b,),
        in_specs=[zspec, zspec, pl.BlockSpec((1, ncp, 1), lambda i: (i, 0, 0))] + [full(a) for a in consts],
        out_specs=[ospec, pl.BlockSpec((1, NSA_KV_HEADS, LANES, ncp), lambda i: (i, 0, 0, 0))],
        out_shape=[sd, jax.ShapeDtypeStruct((b, NSA_KV_HEADS, LANES, ncp), BF16)],
        compiler_params=_vmem_params(("parallel",)),
    )(zk, zv, pend, *consts)


def _gate_col(sig_blk, r, branch):
    c = branch * NSA_GROUP + r
    return sig_blk[:, c:c + 1]


def _heads_t(q_blk):
    out = []
    for rp in range(NSA_GROUP // 2):
        pair_t = q_blk[:, rp * LANES:(rp + 1) * LANES].astype(F32).T
        low = lax.broadcasted_iota(jnp.int32, pair_t.shape, 0) < HEAD_DIM
        out.append(jnp.where(low, pair_t, 0.0).astype(BF16))
        out.append(jnp.where(low, 0.0, pair_t).astype(BF16))
    return out


def _finish_heads(accs, sig, branch):
    outs = []
    for r, acc in enumerate(accs):
        a = acc.T
        o = a / jnp.maximum(pltpu.roll(a, HEAD_DIM, 1), TINY)
        outs.append(o if sig is None else o * _gate_col(sig, r, branch))
    low = lax.broadcasted_iota(jnp.int32, outs[0].shape, 1) < HEAD_DIM
    return jnp.concatenate(
        [jnp.where(low, outs[0], pltpu.roll(outs[1], HEAD_DIM, 1)),
         jnp.where(low, outs[2], pltpu.roll(outs[3], HEAD_DIM, 1))], axis=1)


def _cmp_kernel(q_ref, kc_ref, vc_ref, sig_ref, selt_ref, o_ref, bias_ref, psum_ref, acc_ref):
    tq = q_ref.shape[1]
    ncp = kc_ref.shape[2]
    t0 = pl.program_id(2) * tq
    q_heads = _heads_t(q_ref[0])

    def attend(nk):
        kc = kc_ref[0, 0, 0:nk, :]
        vc_t = vc_ref[0, 0, :, 0:nk]
        t = t0 + lax.broadcasted_iota(jnp.int32, (nk, tq), 1)
        n = lax.broadcasted_iota(jnp.int32, (nk, tq), 0)
        cmask = n * CMP_STRIDE + (CMP_LEN - 1) <= t
        for r, q_t in enumerate(q_heads):
            s = jnp.where(cmask, _dot(kc, q_t), -jnp.inf)
            m = jnp.max(s, axis=0, keepdims=True)
            m = jnp.where(m == -jnp.inf, 0.0, m)
            e = jnp.exp2(s - m)
            p = e * (1.0 / jnp.maximum(jnp.sum(e, axis=0, keepdims=True), TINY))
            psum_ref[0:nk, :] = p if r == 0 else psum_ref[0:nk, :] + p
            acc_ref[r] = _dot(vc_t, e.astype(BF16))
        if nk < ncp:
            psum_ref[nk:ncp, :] = jnp.zeros((ncp - nk, tq), F32)

    last_block = (t0 + tq - CMP_LEN) // CMP_STRIDE
    n_chunks = jnp.minimum(last_block // CMP_CHUNK + 1, ncp // CMP_CHUNK)
    for v in range(1, ncp // CMP_CHUNK + 1):
        pl.when(n_chunks == v)(functools.partial(attend, v * CMP_CHUNK))
    o_ref[0] = _finish_heads([acc_ref[r] for r in range(NSA_GROUP)], sig_ref[0], 0).astype(BF16)

    psum = psum_ref[...]
    ps_hi = psum.astype(BF16)
    ps_lo = (psum - ps_hi.astype(F32)).astype(BF16)
    selt = selt_ref[...]
    imp = _dot(selt, ps_hi) + _dot(selt, ps_lo)
    j = lax.broadcasted_iota(jnp.int32, (NS_PAD, tq), 0)
    cur = (t0 + lax.broadcasted_iota(jnp.int32, (NS_PAD, tq), 1)) >> SEL_BLOCK_LOG2
    forced = (j == 0) | (j == cur) | (j == cur - 1)
    valid = j <= cur
    vals = jnp.where(forced, -jnp.inf, jnp.where(valid, imp, -1.0))
    neg_j = -j.astype(F32)
    for _ in range(SEL_TOPK - N_FORCED):
        m = jnp.max(vals, axis=0, keepdims=True)
        first = jnp.max(jnp.where(vals == m, neg_j, -jnp.inf), axis=0, keepdims=True)
        vals = jnp.where(neg_j == first, -jnp.inf, vals)
    bias_t = jnp.where((vals == -jnp.inf) & valid, 0.0, MASK_BIAS)
    bias_ref[0, 0] = bias_t.astype(BF16)


def _cmp_attention(q3, kc, vc, sig3, selt, tq):
    b, s, _ = q3.shape
    ncp = kc.shape[2]
    assert ncp % CMP_CHUNK == 0 and tq % SEL_BLOCK == 0
    grid = (b, NSA_KV_HEADS, s // tq)
    return pl.pallas_call(
        _cmp_kernel,
        grid=grid,
        in_specs=[pl.BlockSpec((1, tq, GROUP_WIDTH), lambda bi, g, i: (bi, i, g)),
                  pl.BlockSpec((1, 1, ncp, LANES), lambda bi, g, i: (bi, g, 0, 0)),
                  pl.BlockSpec((1, 1, LANES, ncp), lambda bi, g, i: (bi, g, 0, 0)),
                  pl.BlockSpec((1, tq, LANES), lambda bi, g, i: (bi, i, g)),
                  pl.BlockSpec(selt.shape, lambda bi, g, i: (0, 0))],
        out_specs=[pl.BlockSpec((1, tq, GROUP_WIDTH), lambda bi, g, i: (bi, i, g)),
                   pl.BlockSpec((1, 1, NS_PAD, tq), lambda bi, g, i: (bi, g, 0, i))],
        out_shape=[jax.ShapeDtypeStruct((b, s, NSA_WIDTH), BF16),
                   jax.ShapeDtypeStruct((b, NSA_KV_HEADS, NS_PAD, s), BF16)],
        scratch_shapes=[pltpu.VMEM((ncp, tq), F32), pltpu.VMEM((NSA_GROUP, LANES, tq), F32)],
        compiler_params=_vmem_params(("parallel", "parallel", "parallel")),
    )(q3, kc, vc, sig3, selt)


def _sel_kernel(q_ref, bias_ref, et_ref, ks_ref, vs_ref, sig_ref, oc_ref, ow_ref, zn_ref, o_ref,
                qaug_ref, s_ref, m_ref, acc_ref):
    tq = q_ref.shape[1]
    tk = vs_ref.shape[4]
    cols = NSA_GROUP * tq
    qt = pl.program_id(2)

    bias = bias_ref[0, 0]
    for r, q_t in enumerate(_heads_t(q_ref[0])):
        qaug_ref[0:LANES, r * tq:(r + 1) * tq] = bias
        qaug_ref[LANES:2 * LANES, r * tq:(r + 1) * tq] = q_t
    m_ref[...] = jnp.full(m_ref.shape, -jnp.inf, F32)
    acc_ref[...] = jnp.zeros(acc_ref.shape, F32)
    heads = [slice(r * tq, (r + 1) * tq) for r in range(NSA_GROUP)]

    def keys(kt):
        k0 = pl.multiple_of(kt * tk, tk)
        return jnp.concatenate([et_ref[pl.ds(k0, tk), :], ks_ref[0, 0, pl.ds(k0, tk), :]], axis=1)

    def softmax_pv(kt, h, s, row0=0):
        rows = s.shape[0]
        m_prev = m_ref[:, h]
        m_new = jnp.maximum(m_prev, jnp.max(s, axis=0, keepdims=True))
        p = jnp.exp2(s - m_new).astype(BF16)
        acc_ref[:, h] = (jnp.exp2(m_prev - m_new) * acc_ref[:, h]
                         + _dot(vs_ref[0, 0, kt, :, row0:row0 + rows], p))
        m_ref[:, h] = m_new

    k_first = keys(0)
    for h in heads:
        s_ref[:, h] = _dot(k_first, qaug_ref[:, h])

    def step(kt):
        k_next = keys(kt + 1)
        for h in heads:
            s = s_ref[:, h]
            s_next = _dot(k_next, qaug_ref[:, h])
            softmax_pv(kt, h, s)
            s_ref[:, h] = s_next

    def unrolled_steps(j, first):
        for u in range(SEL_UNROLL):
            step(first + SEL_UNROLL * j + u)
        return first

    def two_steps(first):
        step(first)
        step(first + 1)

    last = (qt * tq) // tk
    rest = last % SEL_UNROLL
    pl.when(rest % 2 == 1)(lambda: step(0))
    pl.when(rest >= 2)(functools.partial(two_steps, rest % 2))
    lax.fori_loop(0, last // SEL_UNROLL, unrolled_steps, rest)

    def last_tile(rows):
        if rows == tq == tk:
            hq = tq // 2
            tri = (lax.broadcasted_iota(jnp.int32, (hq, hq), 0) <= lax.broadcasted_iota(jnp.int32, (hq, hq), 1))
            for h in heads:
                early, late = slice(h.start, h.start + hq), slice(h.start + hq, h.stop)
                softmax_pv(last, early, jnp.where(tri, s_ref[0:hq, early], MASK_BIAS))
                softmax_pv(last, late, s_ref[0:hq, late])
                softmax_pv(last, late, jnp.where(tri, s_ref[hq:tq, late], MASK_BIAS), hq)
            return
        kpos = last * tk + lax.broadcasted_iota(jnp.int32, (rows, tq), 0)
        causal = kpos <= qt * tq + lax.broadcasted_iota(jnp.int32, (rows, tq), 1)
        for h in heads:
            softmax_pv(last, h, jnp.where(causal, s_ref[0:rows, h], MASK_BIAS))

    visible = (qt + 1) * tq - last * tk
    for rows in range(tq, tk + 1, tq):
        pl.when(visible == rows)(functools.partial(last_tile, rows))

    acc = acc_ref[...]
    o_sel = _finish_heads([acc[:, h] for h in heads], sig_ref[0], 1)
    o_nsa = oc_ref[0].astype(F32) + o_sel + ow_ref[0].astype(F32)
    o_ref[0] = (o_nsa * zn_ref[0].astype(F32)).astype(BF16)


def _sel_attention(q3, bias, et, ks, vst, sig3, o_c, o_w, zn3, tq):
    b, s, _ = q3.shape
    tk = vst.shape[4]
    cols = NSA_GROUP * tq
    assert SEL_UNROLL == 4 and tk % tq == 0
    group_blk = pl.BlockSpec((1, tq, GROUP_WIDTH), lambda bi, g, i: (bi, i, g))
    return pl.pallas_call(
        _sel_kernel,
        grid=(b, NSA_KV_HEADS, s // tq),
        in_specs=[group_blk,
                  pl.BlockSpec((1, 1, NS_PAD, tq), lambda bi, g, i: (bi, g, 0, i)),
                  pl.BlockSpec((s, LANES), lambda bi, g, i: (0, 0)),
                  pl.BlockSpec((1, 1, s, LANES), lambda bi, g, i: (g, bi, 0, 0)),
                  pl.BlockSpec((1, 1, s // tk, LANES, tk), lambda bi, g, i: (g, bi, 0, 0, 0)),
                  pl.BlockSpec((1, tq, LANES), lambda bi, g, i: (bi, i, g)),
                  group_blk, group_blk, group_blk],
        out_specs=group_blk,
        out_shape=jax.ShapeDtypeStruct((b, s, NSA_WIDTH), BF16),
        scratch_shapes=[pltpu.VMEM((2 * LANES, cols), BF16), pltpu.VMEM((tk, cols), F32),
                        pltpu.VMEM((1, cols), F32), pltpu.VMEM((LANES, cols), F32)],
        compiler_params=_vmem_params(("parallel", "parallel", "parallel")),
    )(q3, bias, et, ks, vst, sig3, o_c, o_w, zn3)


def _win_kernel(q_ref, kp_ref, kc_ref, vp_ref, vc_ref, sig_ref, o_ref):
    tq = q_ref.shape[1]
    qt = pl.program_id(1)
    c = lax.broadcasted_iota(jnp.int32, (tq, tq), 0)
    i = lax.broadcasted_iota(jnp.int32, (tq, tq), 1)
    mask_cur = c <= i
    mask_prev = c > i + jnp.where(qt > 0, 0, tq)
    for g in range(NSA_KV_HEADS):
        lanes = slice(g * GROUP_WIDTH, (g + 1) * GROUP_WIDTH)
        accs = []
        for q_t in _heads_t(q_ref[0, :, lanes]):
            s_p = jnp.where(mask_prev, _dot(kp_ref[g, 0], q_t), -jnp.inf)
            s_c = jnp.where(mask_cur, _dot(kc_ref[g, 0], q_t), -jnp.inf)
            m = jnp.maximum(jnp.max(s_p, axis=0, keepdims=True), jnp.max(s_c, axis=0, keepdims=True))
            e_p = jnp.exp2(s_p - m).astype(BF16)
            e_c = jnp.exp2(s_c - m).astype(BF16)
            accs.append(_dot(vp_ref[g, 0], e_p) + _dot(vc_ref[g, 0], e_c))
        o_ref[0, :, lanes] = _finish_heads(accs, sig_ref[0, :, g * LANES:(g + 1) * LANES], 2).astype(BF16)


def _win_attention(q3, kw, vw, sig3, tq):
    b, s, _ = q3.shape
    assert tq == WINDOW
    nt = s // tq
    kvh = NSA_KV_HEADS
    kspec = lambda back: pl.BlockSpec((kvh, 1, tq, LANES), lambda bi, i: (0, bi, jnp.maximum(i - back, 0), 0))
    vspec = lambda back: pl.BlockSpec((kvh, 1, LANES, tq),
                                      lambda bi, i: (0, bi * nt + jnp.maximum(i - back, 0), 0, 0))
    return pl.pallas_call(
        _win_kernel,
        grid=(b, nt),
        in_specs=[pl.BlockSpec((1, tq, NSA_WIDTH), lambda bi, i: (bi, i, 0)), kspec(1), kspec(0),
                  vspec(1), vspec(0), pl.BlockSpec((1, tq, GATE_WIDTH), lambda bi, i: (bi, i, 0))],
        out_specs=pl.BlockSpec((1, tq, NSA_WIDTH), lambda bi, i: (bi, i, 0)),
        out_shape=jax.ShapeDtypeStruct((b, s, NSA_WIDTH), BF16),
        compiler_params=_vmem_params(("parallel", "parallel")),
    )(q3, kw, kw, vw, vw, sig3)


def _mem_kv_kernel(mem_ref, gm_ref, w_ref, bd256_ref, gk_ref, mk_ref, mv_ref):
    x = mem_ref[0]
    h = x * lax.rsqrt(jnp.mean(x * x, axis=-1, keepdims=True) + EPS) * gm_ref[...]
    mkv = _dot(h.astype(BF16), w_ref[...])
    mk_ref[0] = _head_rms(mkv[:, 0:MEM_WIDTH], bd256_ref[...], gk_ref[...]).astype(BF16)
    for hp in range(MEM_HEADS // 2):
        lo = MEM_WIDTH + hp * LANES
        mv_ref[0, 2 * hp], mv_ref[0, 2 * hp + 1] = _values_t(mkv[:, lo:lo + LANES])


def _mem_kv(mem, gm, w, bd256, gk):
    b, m, _ = mem.shape
    full = lambda a: pl.BlockSpec(a.shape, lambda i: (0,) * a.ndim)
    return pl.pallas_call(
        _mem_kv_kernel,
        grid=(b,),
        in_specs=[pl.BlockSpec((1, m, D_MODEL), lambda i: (i, 0, 0)), full(gm), full(w), full(bd256), full(gk)],
        out_specs=[pl.BlockSpec((1, m, MEM_WIDTH), lambda i: (i, 0, 0)),
                   pl.BlockSpec((1, MEM_HEADS, LANES, m), lambda i: (i, 0, 0, 0))],
        out_shape=[jax.ShapeDtypeStruct((b, m, MEM_WIDTH), BF16),
                   jax.ShapeDtypeStruct((b, MEM_HEADS, LANES, m), BF16)],
        compiler_params=_vmem_params(("parallel",)),
    )(mem, gm, w, bd256, gk)


def _mem_attention_block(q_blk, mk, mv_t):
    accs = []
    for h, q_t in enumerate(_heads_t(q_blk)):
        hp = h // 2
        s = _dot(mk[:, hp * LANES:(hp + 1) * LANES], q_t)
        e = jnp.exp2(s - jnp.max(s, axis=0, keepdims=True))
        accs.append(_dot(mv_t[h], e.astype(BF16)))
    return _finish_heads(accs, None, 0)


def _out_kernel(seq_len, x_ref, ynsa_ref, vp_ref, halo_ref, zp_ref, qm_ref, mk_ref, mv_ref, zm_ref, wo_ref,
                wpool_ref, pscale_ref, o_ref):
    tm = x_ref.shape[0]
    t0 = (pl.program_id(0) * tm) % seq_len

    v = vp_ref[...]
    halo = halo_ref[...] * (t0 > 0).astype(F32)
    e = jnp.concatenate([halo, v], axis=0)
    w2 = e + pltpu.roll(e, 1, 0)
    w4 = w2 + pltpu.roll(w2, 2, 0)
    w8 = w4 + pltpu.roll(w4, 4, 0)
    w16 = w8 + pltpu.roll(w8, 8, 0)
    shape = (tm, POOL_WIDTH)
    grp = lax.broadcasted_iota(jnp.int32, shape, 1) >> HEAD_DIM_LOG2
    wsum = jnp.where(grp == 0, w2[POOL_HALO:],
                     jnp.where(grp == 1, w4[POOL_HALO:], jnp.where(grp == 2, w8[POOL_HALO:], w16[POOL_HALO:])))
    width = jnp.left_shift(2, grp)
    cnt = jnp.minimum(t0 + lax.broadcasted_iota(jnp.int32, shape, 0) + 1, width).astype(F32)
    pooled = wsum / cnt - v
    y_pool = _dot(pooled.astype(BF16), wpool_ref[...]) * pscale_ref[...] * zp_ref[...].astype(F32)

    y_mem = _mem_attention_block(qm_ref[...], mk_ref[0], mv_ref[0]) * zm_ref[...].astype(F32)
    pool_lo, mem_lo = NSA_WIDTH, NSA_WIDTH + POOL_WIDTH
    out = x_ref[...] + _dot(ynsa_ref[...], wo_ref[0:pool_lo, :])
    out = out + _dot(y_pool.astype(BF16), wo_ref[pool_lo:mem_lo, :])
    out = out + _dot(y_mem.astype(BF16), wo_ref[mem_lo:mem_lo + MEM_WIDTH, :])
    o_ref[...] = out


def _out_proj(x2, ynsa, vp, zp, qm, mk, mv_t, zm, wo, wpool, pscale, seq_len, tm):
    n = x2.shape[0]
    row = lambda w: pl.BlockSpec((tm, w), lambda i: (i, 0))
    full = lambda a: pl.BlockSpec(a.shape, lambda i: (0,) * a.ndim)
    per_batch = lambda a: pl.BlockSpec((1,) + a.shape[1:],
                                       lambda i: ((i * tm) // seq_len,) + (0,) * (a.ndim - 1))
    hb = tm // POOL_HALO
    halo = pl.BlockSpec((POOL_HALO, POOL_WIDTH), lambda i: (jnp.maximum(i * hb - 1, 0), 0))
    return pl.pallas_call(
        functools.partial(_out_kernel, seq_len),
        grid=(n // tm,),
        in_specs=[row(D_MODEL), row(NSA_WIDTH), row(POOL_WIDTH), halo, row(POOL_WIDTH), row(MEM_WIDTH),
                  per_batch(mk), per_batch(mv_t), row(MEM_WIDTH), full(wo), full(wpool), full(pscale)],
        out_specs=row(D_MODEL),
        out_shape=jax.ShapeDtypeStruct((n, D_MODEL), F32),
        compiler_params=_vmem_params(("parallel",)),
    )(x2, ynsa, vp, vp, zp, qm, mk, mv_t, zm, wo, wpool, pscale)


def _block_diag_ones(width):
    i = np.arange(width) // HEAD_DIM
    return jnp.asarray((i[:, None] == i[None, :]).astype(np.float32), dtype=BF16)


def _pack_w_in(w_in):
    q, kv, gate, zn, vp, zp, qm, zm = jnp.split(
        w_in, np.cumsum([NSA_WIDTH, 2 * N_BRANCH * KV_WIDTH, NSA_HEADS * N_BRANCH, NSA_WIDTH, POOL_WIDTH,
                         POOL_WIDTH, MEM_WIDTH, MEM_WIDTH])[:-1].tolist(), axis=1)
    cols = np.zeros((NSA_KV_HEADS, LANES), np.int64)
    used = np.zeros((NSA_KV_HEADS, LANES), bool)
    for g in range(NSA_KV_HEADS):
        for k in range(N_BRANCH):
            for r in range(NSA_GROUP):
                cols[g, k * NSA_GROUP + r] = (g * NSA_GROUP + r) * N_BRANCH + k
                used[g, k * NSA_GROUP + r] = True
    gate_p = jnp.where(jnp.asarray(used.reshape(-1))[None, :], gate[:, cols.reshape(-1)], 0.0)
    return jnp.concatenate([q, kv, gate_p, zn, vp, zp, qm, zm], axis=1).astype(BF16)


def _pack_cmp(pos, w1, w2):
    half = CMP_LEN // 2

    def w1_half(wh):
        z = jnp.zeros((half, NSA_KV_HEADS, HEAD_DIM, NSA_KV_HEADS, HEAD_DIM), F32)
        for g in range(NSA_KV_HEADS):
            z = z.at[:, g, :, g, :].set(wh)
        return z.reshape(half, KV_WIDTH, KV_WIDTH).astype(BF16)

    def pos_half(ph):
        return jnp.tile(ph[:, None, :], (1, NSA_KV_HEADS, 1)).reshape(half, KV_WIDTH)

    w2_bd = jnp.zeros((NSA_KV_HEADS, HEAD_DIM, NSA_KV_HEADS, HEAD_DIM), F32)
    for g in range(NSA_KV_HEADS):
        w2_bd = w2_bd.at[g, :, g, :].set(w2)
    return (pos_half(pos[:half]), pos_half(pos[half:]), w1_half(w1[:half]), w1_half(w1[half:]),
            w2_bd.reshape(KV_WIDTH, KV_WIDTH).astype(BF16))


def _sel_map_t(ncp, s):
    nc = (s - CMP_LEN) // CMP_STRIDE + 1
    ns = s // SEL_BLOCK
    c0 = np.arange(nc) * CMP_STRIDE
    c1 = c0 + CMP_LEN
    s0 = np.arange(ns) * SEL_BLOCK
    s1 = s0 + SEL_BLOCK
    ov = np.clip(np.minimum(c1[:, None], s1[None, :]) - np.maximum(c0[:, None], s0[None, :]), 0, None)
    m = np.zeros((NS_PAD, ncp), np.float32)
    m[:ns, :nc] = (ov / CMP_LEN).T
    return jnp.asarray(m, dtype=BF16)


def kernel(x, mem, positions, g_norm, w_in, g_q_nsa, g_k_cmp, g_k_slc, g_k_win, cmp_pos_k, w_cmp_k1,
           w_cmp_k2, cmp_pos_v, w_cmp_v1, w_cmp_v2, w_pool, pool_scale, g_mem, w_mem_kv, g_q_mem,
           g_k_mem, w_out):
    b, s, d = x.shape
    depth = g_norm.shape[0]
    assert d == D_MODEL and s % WINDOW == 0 and s // SEL_BLOCK >= SEL_TOPK and s // SEL_BLOCK <= NS_PAD
    n = b * s
    ncp = s // CMP_STRIDE
    tm = 512
    t_cmp = 1024
    t_sel = 512
    t_win = WINDOW
    assert s % tm == 0 and tm % t_sel == 0 and tm == WINDOW

    inv_freq = ROPE_THETA ** (-jnp.arange(ROPE_HALF, dtype=F32) / ROPE_HALF)
    invf = inv_freq[(np.arange(LANES) % HEAD_DIM) % ROPE_HALF][None, :]
    bd512, bd256, bd128 = (_block_diag_ones(w) for w in (NSA_WIDTH, MEM_WIDTH, KV_WIDTH))
    cs = _rope_cs(positions.reshape(1, n), inv_freq[:, None], min(n, 8192))
    rope_e = _rope_expanders()
    end_idx = np.minimum(np.arange(ncp) * CMP_STRIDE + CMP_LEN - 1, s - 1)
    pend = positions[:, end_idx][:, :, None]
    selt = _sel_map_t(ncp, s)
    et = jnp.asarray((np.arange(s)[:, None] // SEL_BLOCK == np.arange(NS_PAD)[None, :]).astype(np.float32),
                     dtype=BF16)
    tile = lambda g, reps: jnp.tile(g[None, :], (1, reps))

    for l in range(depth):
        x2 = x.reshape(n, d)
        w_all = _pack_w_in(w_in[l])
        (q, kc_raw, vc_raw, ks, vs, kw, vw, sig, zn, vp, zp, qm, zm) = _in_proj(
            x2, cs, g_norm[l][None, :], w_all, bd512, bd256, tile(g_q_nsa[l], 8), tile(g_k_slc[l], 2),
            tile(g_k_win[l], 2), tile(g_q_mem[l], 4), rope_e, tm, t_win)

        cmp_consts = (_pack_cmp(cmp_pos_k[l], w_cmp_k1[l], w_cmp_k2[l])
                      + _pack_cmp(cmp_pos_v[l], w_cmp_v1[l], w_cmp_v2[l].T)
                      + (bd128, tile(g_k_cmp[l], 2), invf))
        kc, vc = _compress(kc_raw.reshape(b, s, KV_WIDTH), vc_raw.reshape(b, s, KV_WIDTH), pend, cmp_consts)

        q3 = q.reshape(b, s, NSA_WIDTH)
        sig3 = sig.reshape(b, s, GATE_WIDTH)
        o_c, bias = _cmp_attention(q3, kc, vc, sig3, selt, t_cmp)
        dup4 = lambda a: a.reshape(NSA_KV_HEADS, b, s, LANES)
        vt5 = lambda a: a.reshape(NSA_KV_HEADS, b, s // tm, LANES, tm)
        o_w = _win_attention(q3, dup4(kw), vw, sig3, t_win)
        y_nsa = _sel_attention(q3, bias, et, dup4(ks), vt5(vs), sig3, o_c, o_w,
                               zn.reshape(b, s, NSA_WIDTH), t_sel)

        mk, mv_t = _mem_kv(mem, g_mem[l][None, :], w_mem_kv[l].astype(BF16), bd256, tile(g_k_mem[l], 4))

        wpool = jnp.zeros((4, HEAD_DIM, 4, HEAD_DIM), F32)
        for g in range(4):
            wpool = wpool.at[g, :, g, :].set(w_pool[l, g])
        out = _out_proj(x2, y_nsa.reshape(n, NSA_WIDTH), vp, zp, qm, mk, mv_t, zm, w_out[l].astype(BF16),
                        wpool.reshape(POOL_WIDTH, POOL_WIDTH).astype(BF16), pool_scale[l][None, :], s, tm)
        x = out.reshape(b, s, d)
    return x
```
